```python
import jax
import jax.numpy as jnp
from jax import lax
import numpy as np

D_MODEL = 1024
BATCH = 4
SEQ = 4096
DEPTH = 2
DEC_BATCH = 32
DEC_SEQ = 1
PAST_LEN = 16384
PAGE_SIZE = 128

HEAD_DIM = 64
D_MIX = D_MODEL
GROUP_W = D_MIX // 4
C_CONV = GROUP_W
H_NSA = GROUP_W // HEAD_DIM
H_MOBA = GROUP_W // HEAD_DIM
C_POOL = GROUP_W
CONV_W = 31
L_CMP = 32
CMP_STRIDE = 16
L_SEL = 64
N_SEL = 16
WINDOW = 512
MOBA_BLOCK = 256
MOBA_TOPK = 3
POOL_WINDOWS = (2, 4, 8, 16)
POOL_GROUPS = len(POOL_WINDOWS)
POOL_GW = C_POOL // POOL_GROUPS
POOL_MAX = max(POOL_WINDOWS)
D_FF = 4 * D_MODEL
ALPHA = (2 * DEPTH) ** 0.25
BETA = (8 * DEPTH) ** -0.25
LN_EPS = 1e-5
Q_CHUNK = 64
WIN_QBLOCK = 128
ATTN_SCALE = HEAD_DIM ** -0.5
N_CONV_IN = 2 * C_CONV
N_NSA_Q = H_NSA * HEAD_DIM
N_NSA_KV = 6 * HEAD_DIM
N_NSA_G = 3 * H_NSA
N_MOBA_IN = 3 * H_MOBA * HEAD_DIM
N_POOL_IN = C_POOL
N_IN = N_CONV_IN + N_NSA_Q + N_NSA_KV + N_NSA_G + N_MOBA_IN + N_POOL_IN
IN_SPLITS = tuple(int(s) for s in np.cumsum([N_CONV_IN, N_NSA_Q, N_NSA_KV, N_NSA_G, N_MOBA_IN]))

kernel_name = 'hymba_conv_nsa_moba_pool_decode_step'


def layer_norm(x, g, b):
    xf = x.astype(jnp.float32)
    mu = jnp.mean(xf, axis=-1, keepdims=True)
    var = jnp.mean(jnp.square(xf - mu), axis=-1, keepdims=True)
    return ((xf - mu) * lax.rsqrt(var + LN_EPS)).astype(x.dtype) * g + b


def masked_softmax(s, mask):
    s = jnp.where(mask, s.astype(jnp.float32), -jnp.inf)
    m = jnp.max(s, axis=-1, keepdims=True)
    m = jnp.where(jnp.isfinite(m), m, 0.0)
    e = jnp.exp(s - m)
    d = jnp.sum(e, axis=-1, keepdims=True)
    return e / jnp.where(d > 0, d, 1.0)


def map_query_chunks(fn, q, qpos, chunk):
    b, tq = q.shape[:2]
    if tq <= chunk:
        return fn(q, qpos)
    n = -(-tq // chunk)
    pad = n * chunk - tq
    qp = jnp.pad(q, [(0, 0), (0, pad)] + [(0, 0)] * (q.ndim - 2))
    pp = jnp.concatenate([qpos, jnp.full((pad,), qpos[-1], qpos.dtype)])
    qc = jnp.moveaxis(qp.reshape(b, n, chunk, *q.shape[2:]), 1, 0)
    out = lax.map(lambda a: fn(a[0], a[1]), (qc, pp.reshape(n, chunk)))
    out = jnp.moveaxis(out, 0, 1)
    return out.reshape(b, n * chunk, *out.shape[3:])[:, :tq]


def conv_module(u, buf, dw, dw_b, ln_g, ln_b, pw):
    a, g = jnp.split(u, 2, axis=-1)
    z = a * jax.nn.sigmoid(g)
    zc = jnp.concatenate([buf, z], axis=1)
    y = lax.conv_general_dilated(zc, dw[:, None, :], window_strides=(1,), padding='VALID',
                                 dimension_numbers=('NWC', 'WIO', 'NWC'),
                                 feature_group_count=C_CONV) + dw_b
    y = jax.nn.silu(layer_norm(y, ln_g, ln_b))
    return y @ pw, zc[:, -(CONV_W - 1):]


def pool_mixer(p, buf, qpos, pool_w, pool_scale):
    b, t, c = p.shape
    rows = jnp.concatenate([buf, p], axis=1)
    rf = rows.astype(jnp.float32)
    cs = jnp.concatenate([jnp.zeros((b, 1, c), jnp.float32), jnp.cumsum(rf, axis=1)], axis=1)
    means = []
    for g, w in enumerate(POOL_WINDOWS):
        lo, hi = g * POOL_GW, (g + 1) * POOL_GW
        s = cs[:, POOL_MAX:POOL_MAX + t, lo:hi] - cs[:, POOL_MAX - w:POOL_MAX - w + t, lo:hi]
        cnt = jnp.minimum(w, qpos + 1).astype(jnp.float32)[None, :, None]
        means.append(s / cnt)
    d = (jnp.concatenate(means, axis=-1) - rf[:, POOL_MAX - 1:]).astype(p.dtype)
    y = jnp.einsum('btgc,gcd->btgd', d.reshape(b, t, POOL_GROUPS, POOL_GW), pool_w).reshape(b, t, c)
    return y * pool_scale, rows[:, -(POOL_MAX - 1):]


def nsa_compress(k, pos_emb, w1, w2):
    b, tk, d = k.shape
    n_s = tk // CMP_STRIDE
    chunks = k[:, :n_s * CMP_STRIDE].reshape(b, n_s, CMP_STRIDE * d)
    w1r = w1.reshape(2, CMP_STRIDE * d, -1)
    bias = pos_emb.reshape(-1) @ w1
    hid = (chunks @ w1r[0])[:, :-1] + (chunks @ w1r[1])[:, 1:] + bias
    return jax.nn.silu(hid) @ w2


def nsa_chunk(q, qpos, kc, vc, ks, vs):
    n_cmp, n_sel = kc.shape[1], ks.shape[1]
    s = jnp.einsum('bqhd,bnd->bqhn', q, kc) * ATTN_SCALE
    cmp_last = jnp.arange(n_cmp) * CMP_STRIDE + (L_CMP - 1)
    p = masked_softmax(s, (cmp_last[None, :] <= qpos[:, None])[None, :, None, :])
    o_cmp = jnp.einsum('bqhn,bnd->bqhd', p.astype(q.dtype), vc)
    j = jnp.arange(n_cmp)[:, None]
    i = jnp.arange(n_sel)[None, :]
    cover = ((j * CMP_STRIDE < (i + 1) * L_SEL) & (j * CMP_STRIDE + L_CMP > i * L_SEL)).astype(jnp.float32)
    imp = jnp.einsum('bqn,ns->bqs', p.sum(axis=2), cover)
    blk = jnp.arange(n_sel)[None, :]
    cur = (qpos // L_SEL)[:, None]
    imp = jnp.where(((blk == 0) | (blk == cur) | (blk == cur - 1))[None], jnp.inf, imp)
    imp = jnp.where((blk * L_SEL > qpos[:, None])[None], -jnp.inf, imp)
    _, idx = lax.top_k(imp, min(N_SEL, n_sel))
    bi = jnp.arange(q.shape[0])[:, None, None]
    kg, vg = ks[bi, idx], vs[bi, idx]
    kpos = idx[..., None] * L_SEL + jnp.arange(L_SEL)
    mask = kpos <= qpos[None, :, None, None]
    sc = jnp.einsum('bqhd,bqjsd->bqhjs', q, kg) * ATTN_SCALE
    b, c, h, k, l = sc.shape
    p2 = masked_softmax(sc.reshape(b, c, h, k * l), mask.reshape(b, c, 1, k * l))
    o_sel = jnp.einsum('bqhjs,bqjsd->bqhd', p2.reshape(sc.shape).astype(q.dtype), vg)
    return o_cmp, o_sel


def nsa_compressed_selected(q, qpos, rows, cmp_pos, cmp_w1, cmp_w2):
    b, tk = rows.shape[:2]
    kc = nsa_compress(rows[:, :, 0], cmp_pos[0], cmp_w1[0], cmp_w2[0])
    vc = nsa_compress(rows[:, :, 1], cmp_pos[1], cmp_w1[1], cmp_w2[1])
    n_sel = -(-tk // L_SEL)
    sel = jnp.pad(rows[:, :, 2:], ((0, 0), (0, n_sel * L_SEL - tk), (0, 0), (0, 0)))
    sel = sel.reshape(b, n_sel, L_SEL, 2, HEAD_DIM)
    ks, vs = sel[:, :, :, 0], sel[:, :, :, 1]

    def chunk(qc, pc):
        return jnp.stack(nsa_chunk(qc, pc, kc, vc, ks, vs), axis=2)
    return map_query_chunks(chunk, q, qpos, Q_CHUNK)


def window_attend(q, qpos, k, v, kpos):
    s = jnp.einsum('bqhd,bkd->bqhk', q, k) * ATTN_SCALE
    rel = qpos[:, None] - kpos[None, :]
    mask = (rel >= 0) & (rel <= WINDOW) & (kpos >= 0)[None, :]
    p = masked_softmax(s, mask[None, :, None, :])
    return jnp.einsum('bqhk,bkd->bqhd', p.astype(q.dtype), v)


def window_banded(q, k, v):
    b, t = q.shape[:2]
    nb = t // WIN_QBLOCK
    kp = jnp.pad(k, ((0, 0), (WINDOW, 0), (0, 0)))
    vp = jnp.pad(v, ((0, 0), (WINDOW, 0), (0, 0)))
    idx = jnp.arange(nb)[:, None] * WIN_QBLOCK + jnp.arange(WINDOW + WIN_QBLOCK)
    qb = q.reshape(b, nb, WIN_QBLOCK, *q.shape[2:])
    qpos = jnp.arange(t).reshape(nb, WIN_QBLOCK)
    out = jax.vmap(window_attend, in_axes=(1, 0, 1, 1, 0), out_axes=1)(qb, qpos, kp[:, idx], vp[:, idx], idx - WINDOW)
    return out.reshape(q.shape)


def moba_chunk(q, qpos, kp, vp, kmean):
    b, c, h, _ = q.shape
    nb = kmean.shape[1]
    s = jnp.einsum('bqhd,bnhd->bqhn', q, kmean).astype(jnp.float32)
    cur = qpos // MOBA_BLOCK
    past = jnp.arange(nb)[None, :] < cur[:, None]
    s = jnp.where(past[None, :, None, :], s, -jnp.inf)
    _, top = lax.top_k(s, min(MOBA_TOPK, nb))
    cur_b = cur[None, :, None, None]
    own = jnp.broadcast_to(cur_b, (b, c, h, 1))
    idx = jnp.concatenate([top, own], axis=-1)
    ok = jnp.concatenate([top < cur_b, jnp.ones((b, c, h, 1), bool)], axis=-1)
    bi = jnp.arange(b)[:, None, None, None]
    hi = jnp.arange(h)[None, None, :, None]
    kg = kp[bi, idx, :, hi]
    vg = vp[bi, idx, :, hi]
    kpos = idx[..., None] * MOBA_BLOCK + jnp.arange(MOBA_BLOCK)
    mask = ok[..., None] & (kpos <= qpos[None, :, None, None, None])
    sc = jnp.einsum('bqhd,bqhjsd->bqhjs', q, kg) * ATTN_SCALE
    p = masked_softmax(sc.reshape(b, c, h, -1), mask.reshape(b, c, h, -1))
    return jnp.einsum('bqhjs,bqhjsd->bqhd', p.reshape(sc.shape).astype(q.dtype), vg)


def moba_attend(q, qpos, k, v):
    b, tk, h, d = k.shape
    nb = -(-tk // MOBA_BLOCK)
    pad = ((0, 0), (0, nb * MOBA_BLOCK - tk), (0, 0), (0, 0))
    kp = jnp.pad(k, pad).reshape(b, nb, MOBA_BLOCK, h, d)
    vp = jnp.pad(v, pad).reshape(b, nb, MOBA_BLOCK, h, d)
    kmean = jnp.mean(kp.astype(jnp.float32), axis=2).astype(k.dtype)
    return map_query_chunks(lambda qc, pc: moba_chunk(qc, pc, kp, vp, kmean), q, qpos, Q_CHUNK)


def decoder_layer(x, pos0, conv_buf, pool_buf, win_buf, nsa_past, moba_past, lw, win_len):
    b, t, _ = x.shape
    z = x @ lw['w_in']
    u_conv, q_nsa, kv_nsa, g_nsa, qkv_moba, u_pool = jnp.split(z, IN_SPLITS, axis=-1)
    qpos = pos0 + jnp.arange(t)

    y_conv, conv_new = conv_module(u_conv, conv_buf, lw['conv_dw'], lw['conv_dw_b'],
                                   lw['conv_ln_g'], lw['conv_ln_b'], lw['conv_pw'])

    qn = q_nsa.reshape(b, t, H_NSA, HEAD_DIM)
    kv = kv_nsa.reshape(b, t, 6, HEAD_DIM)
    nsa_rows = kv[:, :, :4]
    win_rows = kv[:, :, 4:]
    nsa_all = nsa_rows if nsa_past is None else jnp.concatenate([nsa_past, nsa_rows], axis=1)
    o_cs = nsa_compressed_selected(qn, qpos, nsa_all, lw['nsa_cmp_pos'], lw['nsa_cmp_w1'], lw['nsa_cmp_w2'])
    if win_buf is None:
        o_win = window_banded(qn, win_rows[:, :, 0], win_rows[:, :, 1])
        win_new = jnp.pad(win_rows, ((0, 0), (win_len, 0), (0, 0), (0, 0)))[:, -win_len:]
    else:
        win_all = jnp.concatenate([win_buf, win_rows], axis=1)
        kpos = pos0 - win_len + jnp.arange(win_len + t)
        o_win = window_attend(qn, qpos, win_all[:, :, 0], win_all[:, :, 1], kpos)
        win_new = win_all[:, -win_len:]
    gates = jax.nn.sigmoid(g_nsa.reshape(b, t, 3, H_NSA, 1))
    o_nsa = gates[:, :, 0] * o_cs[:, :, 0] + gates[:, :, 1] * o_cs[:, :, 1] + gates[:, :, 2] * o_win

    m = qkv_moba.reshape(b, t, 3, H_MOBA, HEAD_DIM)
    moba_rows = m[:, :, 1:]
    moba_all = moba_rows if moba_past is None else jnp.concatenate([moba_past, moba_rows], axis=1)
    o_moba = moba_attend(m[:, :, 0], qpos, moba_all[:, :, 0], moba_all[:, :, 1])

    y_pool, pool_new = pool_mixer(u_pool, pool_buf, qpos, lw['pool_w'], lw['pool_scale'])

    mix = jnp.concatenate([y_conv, o_nsa.reshape(b, t, -1), o_moba.reshape(b, t, -1), y_pool], axis=-1) @ lw['w_out']
    x = layer_norm(ALPHA * x + mix, lw['ln1_g'], lw['ln1_b'])
    ff = jnp.square(jax.nn.relu(x @ lw['w_up'])) @ lw['w_down']
    x = layer_norm(ALPHA * x + ff, lw['ln2_g'], lw['ln2_b'])
    return x, (moba_rows, nsa_rows, win_new, conv_new, pool_new)


def setup_inputs(seed: int = 0) -> dict:
    key = jax.random.key(seed)
    ks = iter(jax.random.split(key, 40))
    n_pages = PAST_LEN // PAGE_SIZE
    n_pool = (DEC_BATCH * n_pages * 5) // 4
    win_len = min(WINDOW, PAST_LEN)

    def nrm(shape, scale):
        return jax.random.normal(next(ks), shape, jnp.float32) * scale

    perm = jax.random.permutation(next(ks), n_pool)
    page_table = perm[:DEC_BATCH * n_pages].reshape(DEC_BATCH, n_pages).astype(jnp.int32)
    return {
        'x_prompt': nrm((BATCH, SEQ, D_MODEL), 1.0),
        'x_sample': nrm((DEC_BATCH, DEC_SEQ, D_MODEL), 1.0),
        'cache_moba_kv': nrm((DEPTH, n_pool, PAGE_SIZE, 2, H_MOBA, HEAD_DIM), 1.0),
        'cache_nsa_kv': nrm((DEPTH, n_pool, PAGE_SIZE, 4, HEAD_DIM), 1.0),
        'state_nsa_win': nrm((DEPTH, DEC_BATCH, win_len, 2, HEAD_DIM), 1.0),
        'state_conv': nrm((DEPTH, DEC_BATCH, CONV_W - 1, C_CONV), 0.5),
        'state_pool': nrm((DEPTH, DEC_BATCH, POOL_MAX - 1, C_POOL), 1.0),
        'page_table': page_table,
        'w_in': nrm((DEPTH, D_MODEL, N_IN), D_MODEL ** -0.5),
        'conv_dw': nrm((DEPTH, CONV_W, C_CONV), CONV_W ** -0.5),
        'conv_dw_b': nrm((DEPTH, C_CONV), 0.01),
        'conv_ln_g': 1.0 + nrm((DEPTH, C_CONV), 0.05),
        'conv_ln_b': nrm((DEPTH, C_CONV), 0.01),
        'conv_pw': nrm((DEPTH, C_CONV, C_CONV), C_CONV ** -0.5),
        'nsa_cmp_pos': nrm((DEPTH, 2, L_CMP, HEAD_DIM), 0.5),
        'nsa_cmp_w1': nrm((DEPTH, 2, L_CMP * HEAD_DIM, HEAD_DIM), (L_CMP * HEAD_DIM) ** -0.5),
        'nsa_cmp_w2': nrm((DEPTH, 2, HEAD_DIM, HEAD_DIM), HEAD_DIM ** -0.5),
        'pool_w': nrm((DEPTH, POOL_GROUPS, POOL_GW, POOL_GW), POOL_GW ** -0.5),
        'pool_scale': 1.0 + nrm((DEPTH, C_POOL), 0.1),
        'w_out': nrm((DEPTH, D_MIX, D_MODEL), BETA * D_MIX ** -0.5),
        'ln1_g': 1.0 + nrm((DEPTH, D_MODEL), 0.05),
        'ln1_b': nrm((DEPTH, D_MODEL), 0.01),
        'w_up': nrm((DEPTH, D_MODEL, D_FF), BETA * D_MODEL ** -0.5),
        'w_down': nrm((DEPTH, D_FF, D_MODEL), BETA * D_FF ** -0.5),
        'ln2_g': 1.0 + nrm((DEPTH, D_MODEL), 0.05),
        'ln2_b': nrm((DEPTH, D_MODEL), 0.01),
    }


def reference(x_prompt, x_sample, cache_moba_kv, cache_nsa_kv, state_nsa_win, state_conv, state_pool, page_table,
              w_in, conv_dw, conv_dw_b, conv_ln_g, conv_ln_b, conv_pw, nsa_cmp_pos, nsa_cmp_w1, nsa_cmp_w2,
              pool_w, pool_scale, w_out, ln1_g, ln1_b, w_up, w_down, ln2_g, ln2_b):
    past_len = page_table.shape[1] * cache_nsa_kv.shape[2]
    win_len = state_nsa_win.shape[2]
    bp, bs = x_prompt.shape[0], x_sample.shape[0]
    yp, ys = x_prompt, x_sample
    st_p, st_s = [], []
    for l in range(DEPTH):
        lw = {'w_in': w_in[l], 'conv_dw': conv_dw[l], 'conv_dw_b': conv_dw_b[l], 'conv_ln_g': conv_ln_g[l],
              'conv_ln_b': conv_ln_b[l], 'conv_pw': conv_pw[l], 'nsa_cmp_pos': nsa_cmp_pos[l],
              'nsa_cmp_w1': nsa_cmp_w1[l], 'nsa_cmp_w2': nsa_cmp_w2[l], 'pool_w': pool_w[l],
              'pool_scale': pool_scale[l], 'w_out': w_out[l], 'ln1_g': ln1_g[l], 'ln1_b': ln1_b[l],
              'w_up': w_up[l], 'w_down': w_down[l], 'ln2_g': ln2_g[l], 'ln2_b': ln2_b[l]}
        yp, st = decoder_layer(yp, 0, jnp.zeros((bp, CONV_W - 1, C_CONV), yp.dtype),
                               jnp.zeros((bp, POOL_MAX - 1, C_POOL), yp.dtype), None, None, None, lw, win_len)
        st_p.append(st)
        nsa_past = cache_nsa_kv[l, page_table].reshape(bs, past_len, 4, HEAD_DIM)
        moba_past = cache_moba_kv[l, page_table].reshape(bs, past_len, 2, H_MOBA, HEAD_DIM)
        ys, st = decoder_layer(ys, past_len, state_conv[l], state_pool[l], state_nsa_win[l],
                               nsa_past, moba_past, lw, win_len)
        st_s.append(st)

    def stk(sts, i):
        return jnp.stack([s[i] for s in sts], axis=0)

    return (yp, ys, stk(st_p, 0), stk(st_s, 0), stk(st_p, 1), stk(st_s, 1), stk(st_p, 2), stk(st_s, 2),
            stk(st_p, 3), stk(st_s, 3), stk(st_p, 4), stk(st_s, 4))
```

```python
import functools

import numpy as np
import jax
import jax.numpy as jnp
from jax import lax
from jax.experimental import pallas as pl
from jax.experimental.pallas import tpu as pltpu

F32 = jnp.float32
BF16 = jnp.bfloat16
HIGHEST = lax.Precision.HIGHEST

D_MODEL = 1024
HEAD_DIM = 64
GROUP_W = 256
N_HEADS = 4
CONV_W = 31
L_CMP = 32
CMP_STRIDE = 16
L_SEL = 64
N_SEL = 16
WINDOW = 512
MOBA_BLOCK = 256
MOBA_TOPK = 3
POOL_WINDOWS = (2, 4, 8, 16)
POOL_GW = 64
POOL_MAX = 16
D_FF = 4096
DEPTH = 2
ALPHA = (2 * DEPTH) ** 0.25
LN_EPS = 1e-5
ATTN_SCALE = HEAD_DIM ** -0.5
NEG = -1e30

OFF_UCONV, OFF_KVMOBA, OFF_QNSA, OFF_QMOBA = 0, 512, 1024, 1280
OFF_UPOOL, OFF_KVNSA, OFF_KVWIN, OFF_GATE = 1536, 1792, 2048, 2176
N_IN_PAD = 2304
N_IN = 2188
LANES = 128
VMEM_LIMIT = 56 * 1024 * 1024


def _in_perm():
    return np.concatenate([
        np.arange(0, 512), np.arange(1420, 1932), np.arange(512, 768), np.arange(1164, 1420),
        np.arange(1932, 2188), np.arange(768, 1024), np.arange(1024, 1152), np.arange(1152, 1164)])


def _ln(x, g, b):
    mu = jnp.mean(x, axis=-1, keepdims=True)
    xc = x - mu
    var = jnp.mean(xc * xc, axis=-1, keepdims=True)
    return xc * lax.rsqrt(var + LN_EPS) * g + b


def _sigmoid(x):
    return 1.0 / (1.0 + jnp.exp(-x))


def _silu(x):
    return x * _sigmoid(x)


def _dot(a, b, **kw):
    return jnp.dot(a, b, preferred_element_type=F32, **kw)


def _dot_nt(a, b, **kw):
    return lax.dot_general(a, b, (((1,), (1,)), ((), ())), preferred_element_type=F32, **kw)


def _masked_softmax(s, mask):
    s = jnp.where(mask, s, -jnp.inf)
    m = jnp.max(s, axis=-1, keepdims=True)
    m = jnp.where(m == -jnp.inf, 0.0, m)
    e = jnp.exp(s - m)
    d = jnp.sum(e, axis=-1, keepdims=True)
    return e * (1.0 / jnp.where(d > 0, d, 1.0))


def _cparams(sem, vmem=None):
    return pltpu.CompilerParams(dimension_semantics=sem, vmem_limit_bytes=vmem or VMEM_LIMIT)


def _in_proj_kernel(x_ref, w_ref, o_ref):
    o_ref[...] = _dot(x_ref[...].astype(BF16), w_ref[...])


def _in_proj(x, w):
    m, k = x.shape
    n = w.shape[1]
    tm = min(m, 512)
    return pl.pallas_call(
        _in_proj_kernel,
        grid=(m // tm,),
        in_specs=[pl.BlockSpec((tm, k), lambda i: (i, 0)), pl.BlockSpec((k, n), lambda i: (0, 0))],
        out_specs=pl.BlockSpec((tm, n), lambda i: (i, 0)),
        out_shape=jax.ShapeDtypeStruct((m, n), F32),
        compiler_params=_cparams(("parallel",)),
        name="in_proj",
    )(x, w)


def _out_proj_kernel(a_ref, b_ref, c_ref, d_ref, x_ref, w_ref, g_ref, bt_ref, o_ref):
    mix = jnp.concatenate([a_ref[...], b_ref[...], c_ref[...], d_ref[...]], axis=-1).astype(BF16)
    y = _dot(mix, w_ref[...])
    o_ref[...] = _ln(ALPHA * x_ref[...] + y, g_ref[...], bt_ref[...])


def _out_proj(pieces, x, w, g, b):
    m = x.shape[0]
    tm = min(m, 512)
    pspec = pl.BlockSpec((tm, GROUP_W), lambda i: (i, 0))
    return pl.pallas_call(
        _out_proj_kernel,
        grid=(m // tm,),
        in_specs=[pspec, pspec, pspec, pspec,
                  pl.BlockSpec((tm, D_MODEL), lambda i: (i, 0)),
                  pl.BlockSpec((D_MODEL, D_MODEL), lambda i: (0, 0)),
                  pl.BlockSpec((1, D_MODEL), lambda i: (0, 0)),
                  pl.BlockSpec((1, D_MODEL), lambda i: (0, 0))],
        out_specs=pl.BlockSpec((tm, D_MODEL), lambda i: (i, 0)),
        out_shape=jax.ShapeDtypeStruct((m, D_MODEL), F32),
        compiler_params=_cparams(("parallel",)),
        name="out_proj_ln",
    )(*pieces, x, w, g, b)


def _ffn_kernel(x_ref, wu_ref, wd_ref, g_ref, b_ref, o_ref, acc_ref):
    j = pl.program_id(1)

    @pl.when(j == 0)
    def _():
        acc_ref[...] = jnp.zeros_like(acc_ref)

    h = jnp.maximum(_dot(x_ref[...].astype(BF16), wu_ref[...]), 0.0)
    acc_ref[...] += _dot((h * h).astype(BF16), wd_ref[...])

    @pl.when(j == pl.num_programs(1) - 1)
    def _():
        o_ref[...] = _ln(ALPHA * x_ref[...] + acc_ref[...], g_ref[...], b_ref[...])


def _ffn(x, wu, wd, g, b):
    m = x.shape[0]
    tm = min(m, 1024)
    tf = 1024
    return pl.pallas_call(
        _ffn_kernel,
        grid=(m // tm, D_FF // tf),
        in_specs=[pl.BlockSpec((tm, D_MODEL), lambda i, j: (i, 0)),
                  pl.BlockSpec((D_MODEL, tf), lambda i, j: (0, j)),
                  pl.BlockSpec((tf, D_MODEL), lambda i, j: (j, 0)),
                  pl.BlockSpec((1, D_MODEL), lambda i, j: (0, 0)),
                  pl.BlockSpec((1, D_MODEL), lambda i, j: (0, 0))],
        out_specs=pl.BlockSpec((tm, D_MODEL), lambda i, j: (i, 0)),
        out_shape=jax.ShapeDtypeStruct((m, D_MODEL), F32),
        scratch_shapes=[pltpu.VMEM((tm, D_MODEL), F32)],
        compiler_params=_cparams(("parallel", "arbitrary")),
        name="ffn_ln",
    )(x, wu, wd, g, b)


_CONV_HALO = 32
_POOL_HALO = 16


def _pool_groups(rows_ref, buf_ref, tt, cnt_fn):
    n = _POOL_HALO + tt
    lane = lax.broadcasted_iota(jnp.int32, (1, GROUP_W), 1)
    cur = rows_ref[pl.ds(_POOL_HALO, tt), :]
    out = jnp.zeros((tt, GROUP_W), F32)
    shift = 1
    for g, w in enumerate(POOL_WINDOWS):
        src = rows_ref if g == 0 else buf_ref.at[(g - 1) % 2]
        dst = buf_ref.at[g % 2]
        lo = 2 * shift - 1
        dst[pl.ds(lo, n - lo), :] = src[pl.ds(lo, n - lo), :] + src[pl.ds(lo - shift, n - lo), :]
        shift *= 2
        s = dst[pl.ds(_POOL_HALO, tt), :]
        mean = s / cnt_fn(w)
        out = jnp.where((lane >= g * POOL_GW) & (lane < (g + 1) * POOL_GW), mean, out)
    return out - cur


def _conv_pool_prompt_kernel(u_ref, uh_ref, p_ref, ph_ref, dw_ref, dwb_ref, lg_ref, lb_ref, pw_ref,
                             plw_ref, pls_ref, yc_ref, yp_ref, cnew_ref, zc_ref, rows_ref, buf_ref, *, tt):
    i = pl.program_id(1)
    first = i == 0
    u = u_ref[...]
    zg = u[:, :GROUP_W] * _sigmoid(u[:, GROUP_W:])
    uh = uh_ref[...]
    zh = uh[:, :GROUP_W] * _sigmoid(uh[:, GROUP_W:])
    zc_ref[pl.ds(0, _CONV_HALO), :] = jnp.where(first, 0.0, zh)
    zc_ref[pl.ds(_CONV_HALO, tt), :] = zg
    y = jnp.zeros((tt, GROUP_W), F32) + dwb_ref[...]
    for k in range(CONV_W):
        y = y + dw_ref[k:k + 1, :] * zc_ref[pl.ds(_CONV_HALO - (CONV_W - 1) + k, tt), :]
    y = _silu(_ln(y, lg_ref[...], lb_ref[...]))
    yc_ref[...] = _dot(y.astype(BF16), pw_ref[...])

    @pl.when(i == pl.num_programs(1) - 1)
    def _():
        cnew_ref[...] = zc_ref[pl.ds(tt, _CONV_HALO), :]

    rows_ref[pl.ds(0, _POOL_HALO), :] = jnp.where(first, 0.0, ph_ref[...])
    rows_ref[pl.ds(_POOL_HALO, tt), :] = p_ref[...]
    qpos1 = i * tt + lax.broadcasted_iota(jnp.int32, (tt, 1), 0) + 1
    d = _pool_groups(rows_ref, buf_ref, tt, lambda w: jnp.minimum(w, qpos1).astype(F32))
    yp_ref[...] = _dot(d.astype(BF16), plw_ref[...]) * pls_ref[...]


def _conv_pool_prompt(z3, dw, dwb, lg, lb, pw, plw, pls):
    b, t, _ = z3.shape
    tt = 512
    nt = t // tt
    kern = functools.partial(_conv_pool_prompt_kernel, tt=tt)
    cst = lambda shape: pl.BlockSpec(shape, lambda bi, i: (0,) * len(shape))
    return pl.pallas_call(
        kern,
        grid=(b, nt),
        in_specs=[
            pl.BlockSpec((None, tt, 512), lambda bi, i: (bi, i, OFF_UCONV // 512)),
            pl.BlockSpec((None, _CONV_HALO, 512),
                         lambda bi, i: (bi, jnp.maximum(i * (tt // _CONV_HALO) - 1, 0), OFF_UCONV // 512)),
            pl.BlockSpec((None, tt, GROUP_W), lambda bi, i: (bi, i, OFF_UPOOL // GROUP_W)),
            pl.BlockSpec((None, _POOL_HALO, GROUP_W),
                         lambda bi, i: (bi, jnp.maximum(i * (tt // _POOL_HALO) - 1, 0), OFF_UPOOL // GROUP_W)),
            cst((CONV_W, GROUP_W)), cst((1, GROUP_W)), cst((1, GROUP_W)), cst((1, GROUP_W)),
            cst((GROUP_W, GROUP_W)), cst((GROUP_W, GROUP_W)), cst((1, GROUP_W)),
        ],
        out_specs=[
            pl.BlockSpec((None, tt, GROUP_W), lambda bi, i: (bi, i, 0)),
            pl.BlockSpec((None, tt, GROUP_W), lambda bi, i: (bi, i, 0)),
            pl.BlockSpec((None, _CONV_HALO, GROUP_W), lambda bi, i: (bi, 0, 0)),
        ],
        out_shape=[jax.ShapeDtypeStruct((b, t, GROUP_W), F32), jax.ShapeDtypeStruct((b, t, GROUP_W), F32),
                   jax.ShapeDtypeStruct((b, _CONV_HALO, GROUP_W), F32)],
        scratch_shapes=[pltpu.VMEM((_CONV_HALO + tt, GROUP_W), F32),
                        pltpu.VMEM((_POOL_HALO + tt, GROUP_W), F32),
                        pltpu.VMEM((2, _POOL_HALO + tt, GROUP_W), F32)],
        compiler_params=_cparams(("parallel", "arbitrary")),
        name="conv_pool_prompt",
    )(z3, z3, z3, z3, dw, dwb, lg, lb, pw, plw, pls)


def _compress_tail(c, bias_k, bias_v, w2k, w2v):
    n = c.shape[0]

    def one(c0, c1, bias, w2):
        hid = c0 + pltpu.roll(c1, n - 1, 0) + bias
        return _dot(_silu(hid).astype(BF16), w2)

    kc = one(c[:, 0:64], c[:, 64:128], bias_k, w2k)
    vc = one(c[:, 128:192], c[:, 192:256], bias_v, w2v)
    return kc, vc


def _cmp_bias(pos_ref, w1_ref):
    bk = _dot(pos_ref[0].astype(BF16), w1_ref[0])[0:1]
    bv = _dot(pos_ref[1].astype(BF16), w1_ref[1])[0:1]
    return bk, bv


def _compress_prompt_kernel(ck_ref, cv_ref, pos_ref, w1_ref, w2_ref, kc_ref, vc_ref):
    bk, bv = _cmp_bias(pos_ref, w1_ref)
    half = CMP_STRIDE * HEAD_DIM
    ck = ck_ref[...].astype(BF16)
    cv = cv_ref[...].astype(BF16)
    c = jnp.concatenate([_dot(ck, w1_ref[0, 0:half, :]), _dot(ck, w1_ref[0, half:2 * half, :]),
                         _dot(cv, w1_ref[1, 0:half, :]), _dot(cv, w1_ref[1, half:2 * half, :])], axis=-1)
    kc, vc = _compress_tail(c, bk, bv, w2_ref[0], w2_ref[1])
    kc_ref[...] = kc
    vc_ref[...] = vc


def _compress_prompt(chunks_k, chunks_v, pos, w1, w2):
    b, nc, kk = chunks_k.shape
    cst = lambda shape: pl.BlockSpec(shape, lambda bi: (0,) * len(shape))
    return pl.pallas_call(
        _compress_prompt_kernel,
        grid=(b,),
        in_specs=[pl.BlockSpec((None, nc, kk), lambda bi: (bi, 0, 0)),
                  pl.BlockSpec((None, nc, kk), lambda bi: (bi, 0, 0)),
                  cst(pos.shape), cst(w1.shape), cst(w2.shape)],
        out_specs=[pl.BlockSpec((None, nc, HEAD_DIM), lambda bi: (bi, 0, 0)),
                   pl.BlockSpec((None, nc, HEAD_DIM), lambda bi: (bi, 0, 0))],
        out_shape=[jax.ShapeDtypeStruct((b, nc, HEAD_DIM), F32)] * 2,
        compiler_params=_cparams(("parallel",)),
        name="compress_prompt",
    )(chunks_k, chunks_v, pos, w1, w2)


def _rank_desc(v, n):
    lane = lax.broadcasted_iota(jnp.int32, v.shape, 1)
    rank = jnp.zeros(v.shape, F32)
    for j in range(n):
        col = v[:, j:j + 1]
        rank = rank + jnp.where((col > v) | ((col == v) & (lane > j)), 1.0, 0.0)
    return rank


def _cover_matrix(n_cmp_pad, n_cmp, n_sel_pad):
    j = np.arange(n_cmp_pad)[:, None]
    i = np.arange(n_sel_pad)[None, :]
    cov = (j * CMP_STRIDE < (i + 1) * L_SEL) & (j * CMP_STRIDE + L_CMP > i * L_SEL) & (j < n_cmp)
    return jnp.asarray(cov.astype(np.float32))


def _nsa_prompt_kernel(q_ref, g_ref, kv_ref, win_ref, kc_ref, vc_ref, cover_ref, e_ref, o_ref,
                       m_ref, l_ref, acc_ref, *, c, tk, n_cmp):
    i = pl.program_id(1)
    qs = i * c
    h4 = N_HEADS * c
    q = q_ref[...]
    q4 = jnp.concatenate([q[:, h * HEAD_DIM:(h + 1) * HEAD_DIM] for h in range(N_HEADS)], axis=0).astype(BF16)
    qpos = qs + lax.broadcasted_iota(jnp.int32, (c, 1), 0)
    qpos4 = jnp.concatenate([qpos] * N_HEADS, axis=0)

    nc = kc_ref.shape[0]
    s = _dot_nt(q4, kc_ref[...].astype(BF16)) * ATTN_SCALE
    jj = lax.broadcasted_iota(jnp.int32, (1, nc), 1)
    p = _masked_softmax(s, (jj * CMP_STRIDE + (L_CMP - 1) <= qpos4) & (jj < n_cmp))
    o_cmp = _dot(p.astype(BF16), vc_ref[...].astype(BF16))
    psum = p[0:c] + p[c:2 * c] + p[2 * c:3 * c] + p[3 * c:4 * c]
    imp = _dot(psum, cover_ref[...], precision=HIGHEST)
    ns = imp.shape[1]
    blk = lax.broadcasted_iota(jnp.int32, (1, ns), 1)
    cur = qpos // L_SEL
    imp = jnp.where((blk == 0) | (blk == cur) | (blk == cur - 1), jnp.inf, imp)
    imp = jnp.where(blk * L_SEL > qpos, -jnp.inf, imp)
    sel = jnp.where(_rank_desc(imp, ns) < N_SEL, 1.0, 0.0).astype(BF16)

    m_ref[...] = jnp.full(m_ref.shape, NEG, F32)
    l_ref[...] = jnp.zeros(l_ref.shape, F32)
    acc_ref[...] = jnp.zeros(acc_ref.shape, F32)
    n_tiles = (qs + c + tk - 1) // tk

    def body(j, carry):
        k0 = pl.multiple_of(j * tk, tk)
        kt = kv_ref[pl.ds(k0, tk), 2 * HEAD_DIM:3 * HEAD_DIM].astype(BF16)
        vt = kv_ref[pl.ds(k0, tk), 3 * HEAD_DIM:4 * HEAD_DIM].astype(BF16)
        sc = _dot_nt(q4, kt) * ATTN_SCALE
        chosen = _dot(sel, e_ref[:, pl.ds(k0, tk)])
        kpos = k0 + lax.broadcasted_iota(jnp.int32, (1, tk), 1)
        valid = (chosen > 0.5) & (kpos <= qpos)
        valid4 = jnp.concatenate([valid] * N_HEADS, axis=0)
        sc = jnp.where(valid4, sc, NEG)
        m_old = m_ref[...]
        m_new = jnp.maximum(m_old, jnp.max(sc, axis=-1, keepdims=True))
        pe = jnp.where(valid4, jnp.exp(sc - m_new), 0.0)
        a = jnp.exp(m_old - m_new)
        l_ref[...] = a * l_ref[...] + jnp.sum(pe, axis=-1, keepdims=True)
        acc_ref[...] = a * acc_ref[...] + _dot(pe.astype(BF16), vt)
        m_ref[...] = m_new
        return carry

    lax.fori_loop(0, n_tiles, body, 0)
    o_sel = acc_ref[...] * (1.0 / l_ref[...])

    wl = WINDOW + c
    ks = pl.multiple_of(jnp.maximum(qs - WINDOW, 0), c)
    kw = win_ref[pl.ds(ks, wl), 0:HEAD_DIM].astype(BF16)
    vw = win_ref[pl.ds(ks, wl), HEAD_DIM:2 * HEAD_DIM].astype(BF16)
    sw = _dot_nt(q4, kw) * ATTN_SCALE
    rel = qpos4 - (ks + lax.broadcasted_iota(jnp.int32, (1, wl), 1))
    pw = _masked_softmax(sw, (rel >= 0) & (rel <= WINDOW))
    o_win = _dot(pw.astype(BF16), vw)

    g = _sigmoid(g_ref[...])

    def gate(br):
        return jnp.concatenate([g[:, br * N_HEADS + h:br * N_HEADS + h + 1] for h in range(N_HEADS)], axis=0)

    o = gate(0) * o_cmp + gate(1) * o_sel + gate(2) * o_win
    for h in range(N_HEADS):
        o_ref[:, h * HEAD_DIM:(h + 1) * HEAD_DIM] = o[h * c:(h + 1) * c]


def _nsa_prompt(z3, kc, vc):
    b, t, _ = z3.shape
    c = 128
    tk = 512
    nc = kc.shape[1]
    n_cmp = t // CMP_STRIDE - 1
    ns = t // L_SEL
    cover = _cover_matrix(nc, n_cmp, ns)
    e = jnp.asarray((np.arange(ns)[:, None] == (np.arange(t)[None, :] // L_SEL)).astype(np.float32)).astype(BF16)
    kern = functools.partial(_nsa_prompt_kernel, c=c, tk=tk, n_cmp=n_cmp)
    return pl.pallas_call(
        kern,
        grid=(b, t // c),
        in_specs=[
            pl.BlockSpec((None, c, GROUP_W), lambda bi, i: (bi, i, OFF_QNSA // GROUP_W)),
            pl.BlockSpec((None, c, LANES), lambda bi, i: (bi, i, OFF_GATE // LANES)),
            pl.BlockSpec((None, t, GROUP_W), lambda bi, i: (bi, 0, OFF_KVNSA // GROUP_W)),
            pl.BlockSpec((None, t, LANES), lambda bi, i: (bi, 0, OFF_KVWIN // LANES)),
            pl.BlockSpec((None, nc, HEAD_DIM), lambda bi, i: (bi, 0, 0)),
            pl.BlockSpec((None, nc, HEAD_DIM), lambda bi, i: (bi, 0, 0)),
            pl.BlockSpec((nc, ns), lambda bi, i: (0, 0)),
            pl.BlockSpec((ns, t), lambda bi, i: (0, 0)),
        ],
        out_specs=pl.BlockSpec((None, c, GROUP_W), lambda bi, i: (bi, i, 0)),
        out_shape=jax.ShapeDtypeStruct((b, t, GROUP_W), F32),
        scratch_shapes=[pltpu.VMEM((N_HEADS * c, 1), F32), pltpu.VMEM((N_HEADS * c, 1), F32),
                        pltpu.VMEM((N_HEADS * c, HEAD_DIM), F32)],
        compiler_params=_cparams(("parallel", "arbitrary")),
        name="nsa_prompt",
    )(z3, z3, z3, z3, kc, vc, cover, e)


def _moba_prompt_kernel(q_ref, kv_ref, o_ref, kmean_ref, m_ref, l_ref, acc_ref, *, nb):
    i = pl.program_id(1)
    c = MOBA_BLOCK

    @pl.when(i == 0)
    def _():
        for n in range(nb):
            kmean_ref[n:n + 1, :] = jnp.mean(kv_ref[n * c:(n + 1) * c, 0:GROUP_W], axis=0, keepdims=True)

    q = q_ref[...]
    blk = lax.broadcasted_iota(jnp.int32, (1, nb), 1)
    past = blk < i
    row = lax.broadcasted_iota(jnp.int32, (c, c), 0)
    col = lax.broadcasted_iota(jnp.int32, (c, c), 1)
    tri = col <= row
    for h in range(N_HEADS):
        lo, hi = h * HEAD_DIM, (h + 1) * HEAD_DIM
        qh = q[:, lo:hi]
        gate = _dot_nt(qh, kmean_ref[:, lo:hi], precision=HIGHEST)
        gate = jnp.where(past, gate, -jnp.inf)
        sel = jnp.where((_rank_desc(gate, nb) < MOBA_TOPK) & past, 1.0, 0.0)
        qb = qh.astype(BF16)
        m_ref[...] = jnp.full(m_ref.shape, NEG, F32)
        l_ref[...] = jnp.zeros(l_ref.shape, F32)
        acc_ref[...] = jnp.zeros(acc_ref.shape, F32)

        def body(j, carry):
            k0 = pl.multiple_of(j * c, c)
            kt = kv_ref[pl.ds(k0, c), lo:hi].astype(BF16)
            vt = kv_ref[pl.ds(k0, c), GROUP_W + lo:GROUP_W + hi].astype(BF16)
            sc = _dot_nt(qb, kt) * ATTN_SCALE
            chosen = jnp.sum(jnp.where(blk == j, sel, 0.0), axis=-1, keepdims=True) > 0.5
            own = j == i
            valid = (tri & own) | (chosen & jnp.logical_not(own))
            sc = jnp.where(valid, sc, NEG)
            m_old = m_ref[...]
            m_new = jnp.maximum(m_old, jnp.max(sc, axis=-1, keepdims=True))
            pe = jnp.where(valid, jnp.exp(sc - m_new), 0.0)
            a = jnp.exp(m_old - m_new)
            l_ref[...] = a * l_ref[...] + jnp.sum(pe, axis=-1, keepdims=True)
            acc_ref[...] = a * acc_ref[...] + _dot(pe.astype(BF16), vt)
            m_ref[...] = m_new
            return carry

        lax.fori_loop(0, i + 1, body, 0)
        o_ref[:, lo:hi] = acc_ref[...] * (1.0 / l_ref[...])


def _moba_prompt(z3):
    b, t, _ = z3.shape
    nb = t // MOBA_BLOCK
    kern = functools.partial(_moba_prompt_kernel, nb=nb)
    return pl.pallas_call(
        kern,
        grid=(b, nb),
        in_specs=[pl.BlockSpec((None, MOBA_BLOCK, GROUP_W), lambda bi, i: (bi, i, OFF_QMOBA // GROUP_W)),
                  pl.BlockSpec((None, t, 2 * GROUP_W), lambda bi, i: (bi, 0, OFF_KVMOBA // (2 * GROUP_W)))],
        out_specs=pl.BlockSpec((None, MOBA_BLOCK, GROUP_W), lambda bi, i: (bi, i, 0)),
        out_shape=jax.ShapeDtypeStruct((b, t, GROUP_W), F32),
        scratch_shapes=[pltpu.VMEM((nb, GROUP_W), F32), pltpu.VMEM((MOBA_BLOCK, 1), F32),
                        pltpu.VMEM((MOBA_BLOCK, 1), F32), pltpu.VMEM((MOBA_BLOCK, HEAD_DIM), F32)],
        compiler_params=_cparams(("parallel", "arbitrary")),
        name="moba_prompt",
    )(z3, z3)


def _conv_pool_sample_kernel(z_ref, cst_ref, pst_ref, dw_ref, dwb_ref, lg_ref, lb_ref, pw_ref, plw_ref, pls_ref,
                             yc_ref, yp_ref, cnew_ref, pnew_ref):
    u = z_ref[:, OFF_UCONV:OFF_UCONV + 2 * GROUP_W]
    zg = u[:, :GROUP_W] * _sigmoid(u[:, GROUP_W:])
    nst = CONV_W - 1
    y = dwb_ref[...] + dw_ref[nst:nst + 1, :] * zg
    for k in range(nst):
        y = y + dw_ref[k:k + 1, :] * cst_ref[k]
    y = _silu(_ln(y, lg_ref[...], lb_ref[...]))
    yc_ref[...] = _dot(y.astype(BF16), pw_ref[...])
    for k in range(nst - 1):
        cnew_ref[k] = cst_ref[k + 1]
    cnew_ref[nst - 1] = zg

    p = z_ref[:, OFF_UPOOL:OFF_UPOOL + GROUP_W]
    npst = POOL_MAX - 1
    lane = lax.broadcasted_iota(jnp.int32, (1, GROUP_W), 1)
    run = p
    mean = jnp.zeros_like(p)
    k = 1
    for g, w in enumerate(POOL_WINDOWS):
        while k < w:
            run = run + pst_ref[npst - k]
            k += 1
        mean = jnp.where((lane >= g * POOL_GW) & (lane < (g + 1) * POOL_GW), run / float(w), mean)
    d = mean - p
    yp_ref[...] = _dot(d.astype(BF16), plw_ref[...]) * pls_ref[...]
    for k in range(npst - 1):
        pnew_ref[k] = pst_ref[k + 1]
    pnew_ref[npst - 1] = p


def _conv_pool_sample(z, cst, pst, dw, dwb, lg, lb, pw, plw, pls):
    s = z.shape[0]
    return pl.pallas_call(
        _conv_pool_sample_kernel,
        out_shape=[jax.ShapeDtypeStruct((s, GROUP_W), F32), jax.ShapeDtypeStruct((s, GROUP_W), F32),
                   jax.ShapeDtypeStruct(cst.shape, F32), jax.ShapeDtypeStruct(pst.shape, F32)],
        compiler_params=pltpu.CompilerParams(vmem_limit_bytes=VMEM_LIMIT),
        name="conv_pool_sample",
    )(z, cst, pst, dw, dwb, lg, lb, pw, plw, pls)


_CMP_PAGES = 16


def _compress_pages_kernel(pt_ref, cache_ref, w_ref, c_ref, buf_ref, rows_ref, sem_ref, *, layer, n_steps, page):
    b = pl.program_id(0)
    s = pl.program_id(1)
    g = _CMP_PAGES
    step = b * n_steps + s
    total = pl.num_programs(0) * n_steps

    def copies(stp, slot):
        bb = stp // n_steps
        ss = stp % n_steps
        return [pltpu.make_async_copy(cache_ref.at[layer, pt_ref[bb, ss * g + k], pl.ds(0, 2)],
                                      buf_ref.at[slot, k], sem_ref.at[slot]) for k in range(g)]

    @pl.when(step == 0)
    def _():
        for cp in copies(step, 0):
            cp.start()

    slot = step % 2

    @pl.when(step + 1 < total)
    def _():
        for cp in copies(step + 1, 1 - slot):
            cp.start()

    for cp in copies(step, slot):
        cp.wait()

    for k in range(g):
        x = buf_ref[slot, k].reshape(2 * HEAD_DIM, page)
        rows_ref[pl.ds(k * page, page), :] = x.T
    nchunk = g * page // CMP_STRIDE
    acc = jnp.zeros((nchunk, 4 * HEAD_DIM), F32)
    for r in range(CMP_STRIDE):
        xr = rows_ref[pl.ds(r, nchunk, stride=CMP_STRIDE), :].astype(BF16)
        acc = acc + _dot(xr, w_ref[r])
    c_ref[...] = acc


def _compress_pages(page_table, cache_t, wstack, layer):
    s, npg = page_table.shape
    page = cache_t.shape[-1]
    g = _CMP_PAGES
    n_steps = npg // g
    nchunk = g * page // CMP_STRIDE
    kern = functools.partial(_compress_pages_kernel, layer=layer, n_steps=n_steps, page=page)
    return pl.pallas_call(
        kern,
        grid_spec=pltpu.PrefetchScalarGridSpec(
            num_scalar_prefetch=1,
            grid=(s, n_steps),
            in_specs=[pl.BlockSpec(memory_space=pl.ANY),
                      pl.BlockSpec(wstack.shape, lambda b, i, pt: (0, 0, 0))],
            out_specs=pl.BlockSpec((None, nchunk, 4 * HEAD_DIM), lambda b, i, pt: (b, i, 0)),
            scratch_shapes=[pltpu.VMEM((2, g, 2, HEAD_DIM, page), F32),
                            pltpu.VMEM((g * page, 2 * HEAD_DIM), F32),
                            pltpu.SemaphoreType.DMA((2,))]),
        out_shape=jax.ShapeDtypeStruct((s, n_steps * nchunk, 4 * HEAD_DIM), F32),
        compiler_params=_cparams(("arbitrary", "arbitrary")),
        name="compress_pages",
    )(page_table, cache_t, wstack)


def _stack_heads(q):
    rows = [q[:, h * HEAD_DIM:(h + 1) * HEAD_DIM] for h in range(N_HEADS)]
    return jnp.concatenate(rows + [jnp.zeros((8 - N_HEADS, HEAD_DIM), q.dtype)], axis=0)


def _nsa_select_kernel(c_ref, z_ref, pos_ref, w1_ref, w2_ref, cover_ref, ocmp_ref, idx_ref, *, n_cmp, n_sel, qpos):
    bk, bv = _cmp_bias(pos_ref, w1_ref)
    kc, vc = _compress_tail(c_ref[...], bk, bv, w2_ref[0], w2_ref[1])
    q = z_ref[:, OFF_QNSA:OFF_QNSA + GROUP_W]
    q4 = _stack_heads(q)
    nc = kc.shape[0]
    s = _dot_nt(q4.astype(BF16), kc.astype(BF16)) * ATTN_SCALE
    jj = lax.broadcasted_iota(jnp.int32, (1, nc), 1)
    p = _masked_softmax(s, (jj * CMP_STRIDE + (L_CMP - 1) <= qpos) & (jj < n_cmp))
    ocmp_ref[...] = _dot(p.astype(BF16), vc.astype(BF16))[0:N_HEADS]
    psum = jnp.sum(p[0:N_HEADS], axis=0, keepdims=True)
    imp = _dot(jnp.concatenate([psum] * 8, axis=0), cover_ref[...], precision=HIGHEST)[0:1]
    nsp = imp.shape[1]
    blk = lax.broadcasted_iota(jnp.int32, (1, nsp), 1)
    cur = qpos // L_SEL
    imp = jnp.where((blk == 0) | (blk == cur) | (blk == cur - 1), jnp.inf, imp)
    imp = jnp.where(blk * L_SEL > qpos, -jnp.inf, imp)
    work = jnp.where(blk < n_sel, jnp.maximum(imp, -3e38), -jnp.inf)
    out_lane = lax.broadcasted_iota(jnp.int32, (1, LANES), 1)
    idx = jnp.zeros((1, LANES), jnp.int32)
    for t in range(N_SEL):
        mx = jnp.max(work, axis=-1, keepdims=True)
        pick = jnp.min(jnp.where(work == mx, blk, nsp), axis=-1, keepdims=True)
        idx = jnp.where(out_lane == t, pick, idx)
        work = jnp.where(blk == pick, -jnp.inf, work)
    idx_ref[...] = idx


def _nsa_select(c, z3, pos, w1, w2, qpos):
    s, nc, _ = c.shape
    n_cmp = nc - 1
    n_sel = qpos // L_SEL + 1
    nsp = -(-n_sel // LANES) * LANES
    cover = _cover_matrix(nc, n_cmp, nsp)
    kern = functools.partial(_nsa_select_kernel, n_cmp=n_cmp, n_sel=n_sel, qpos=qpos)
    cst = lambda shape: pl.BlockSpec(shape, lambda b: (0,) * len(shape))
    return pl.pallas_call(
        kern,
        grid=(s,),
        in_specs=[pl.BlockSpec((None, nc, 4 * HEAD_DIM), lambda b: (b, 0, 0)),
                  pl.BlockSpec((None, 1, N_IN_PAD), lambda b: (b, 0, 0)),
                  cst(pos.shape), cst(w1.shape), cst(w2.shape), cst(cover.shape)],
        out_specs=[pl.BlockSpec((None, N_HEADS, HEAD_DIM), lambda b: (b, 0, 0)),
                   pl.BlockSpec((None, 1, LANES), lambda b: (b, 0, 0))],
        out_shape=[jax.ShapeDtypeStruct((s, N_HEADS, HEAD_DIM), F32), jax.ShapeDtypeStruct((s, 1, LANES), jnp.int32)],
        compiler_params=_cparams(("parallel",)),
        name="nsa_select",
    )(c, z3, pos, w1, w2, cover)


def _col_from_row(row):
    n = row.shape[1]
    eye = lax.broadcasted_iota(jnp.int32, (n, n), 0) == lax.broadcasted_iota(jnp.int32, (n, n), 1)
    return jnp.sum(jnp.where(eye, row, 0.0), axis=-1, keepdims=True)


def _nsa_sample_kernel(pt_ref, idx_ref, cache_ref, z_ref, ocmp_ref, win_ref, o_ref, wnew_ref, buf_ref, sem_ref,
                       *, layer, page, qpos):
    b = pl.program_id(0)
    npg = pt_ref.shape[1]
    per_page = page // L_SEL

    def copy(t):
        pg = jnp.minimum(idx_ref[b, t] // per_page, npg - 1)
        return pltpu.make_async_copy(cache_ref.at[layer, pt_ref[b, pg], pl.ds(2, 2)], buf_ref.at[t], sem_ref.at[0])

    for t in range(N_SEL):
        copy(t).start()

    z = z_ref[...]
    q = z[:, OFF_QNSA:OFF_QNSA + GROUP_W]
    q4b = _stack_heads(q).astype(BF16)
    kvn = z[:, OFF_KVNSA:OFF_KVNSA + GROUP_W]
    ks_new, vs_new = kvn[:, 2 * HEAD_DIM:3 * HEAD_DIM], kvn[:, 3 * HEAD_DIM:4 * HEAD_DIM]
    kvw = z[:, OFF_KVWIN:OFF_KVWIN + 2 * HEAD_DIM]
    kw_new, vw_new = kvw[:, 0:HEAD_DIM], kvw[:, HEAD_DIM:2 * HEAD_DIM]
    q4r = q4b.astype(F32)

    def new_score(k_new):
        kb = k_new.astype(BF16).astype(F32)
        return jnp.sum(q4r * kb, axis=-1, keepdims=True) * ATTN_SCALE

    wk = win_ref[0]
    wv = win_ref[1]
    sw = _dot(q4b, wk.astype(BF16)) * ATTN_SCALE
    sw_new = new_score(kw_new)
    mw = jnp.maximum(jnp.max(sw, axis=-1, keepdims=True), sw_new)
    ew = jnp.exp(sw - mw)
    ew_new = jnp.exp(sw_new - mw)
    dw = jnp.sum(ew, axis=-1, keepdims=True) + ew_new
    pw = ew * (1.0 / dw)
    pw_new = ew_new * (1.0 / dw)
    o_win = _dot_nt(pw.astype(BF16), wv.astype(BF16)) \
        + pw_new.astype(BF16).astype(F32) * vw_new.astype(BF16).astype(F32)
    wl = wk.shape[1]
    lane = lax.broadcasted_iota(jnp.int32, (1, wl), 1)
    wnew_ref[0] = jnp.where(lane == wl - 1, _col_from_row(kw_new), pltpu.roll(wk, wl - 1, 1))
    wnew_ref[1] = jnp.where(lane == wl - 1, _col_from_row(vw_new), pltpu.roll(wv, wl - 1, 1))

    for t in range(N_SEL):
        copy(t).wait()

    n_past = (qpos // L_SEL)
    plane = lax.broadcasted_iota(jnp.int32, (1, page), 1)
    scores = []
    new_taken = jnp.int32(0)
    for t in range(N_SEL):
        bid = idx_ref[b, t]
        in_past = bid < n_past
        half = bid % per_page
        valid = (plane >= half * L_SEL) & (plane < (half + 1) * L_SEL) & in_past
        sc = _dot(q4b, buf_ref[t, 0].astype(BF16)) * ATTN_SCALE
        scores.append(jnp.where(valid, sc, -jnp.inf))
        new_taken = new_taken + (bid == n_past).astype(jnp.int32)
    has_new = new_taken > 0
    ss_new = jnp.where(has_new, new_score(ks_new), -jnp.inf)
    ms = ss_new
    for sc in scores:
        ms = jnp.maximum(ms, jnp.max(sc, axis=-1, keepdims=True))
    ms = jnp.where(ms == -jnp.inf, 0.0, ms)
    es_new = jnp.exp(ss_new - ms)
    ds = es_new
    es = []
    for sc in scores:
        e = jnp.exp(sc - ms)
        es.append(e)
        ds = ds + jnp.sum(e, axis=-1, keepdims=True)
    inv = 1.0 / jnp.where(ds > 0, ds, 1.0)
    o_sel = (es_new * inv).astype(BF16).astype(F32) * vs_new.astype(BF16).astype(F32)
    for t in range(N_SEL):
        o_sel = o_sel + _dot_nt((es[t] * inv).astype(BF16), buf_ref[t, 1].astype(BF16))

    g = _sigmoid(z[:, OFF_GATE:OFF_GATE + LANES])
    o_cmp = ocmp_ref[...]
    for h in range(N_HEADS):
        oh = (g[:, h:h + 1] * o_cmp[h:h + 1] + g[:, N_HEADS + h:N_HEADS + h + 1] * o_sel[h:h + 1]
              + g[:, 2 * N_HEADS + h:2 * N_HEADS + h + 1] * o_win[h:h + 1])
        o_ref[:, h * HEAD_DIM:(h + 1) * HEAD_DIM] = oh


def _nsa_sample(page_table, idx, cache_t, z3, ocmp, win_t, layer, qpos):
    s = z3.shape[0]
    page = cache_t.shape[-1]
    wl = win_t.shape[-1]
    kern = functools.partial(_nsa_sample_kernel, layer=layer, page=page, qpos=qpos)
    return pl.pallas_call(
        kern,
        grid_spec=pltpu.PrefetchScalarGridSpec(
            num_scalar_prefetch=2,
            grid=(s,),
            in_specs=[pl.BlockSpec(memory_space=pl.ANY),
                      pl.BlockSpec((None, 1, N_IN_PAD), lambda b, pt, ix: (b, 0, 0)),
                      pl.BlockSpec((None, N_HEADS, HEAD_DIM), lambda b, pt, ix: (b, 0, 0)),
                      pl.BlockSpec((None, None, 2, HEAD_DIM, wl), lambda b, pt, ix: (layer, b, 0, 0, 0))],
            out_specs=[pl.BlockSpec((None, 1, GROUP_W), lambda b, pt, ix: (b, 0, 0)),
                       pl.BlockSpec((None, 2, HEAD_DIM, wl), lambda b, pt, ix: (b, 0, 0, 0))],
            scratch_shapes=[pltpu.VMEM((N_SEL, 2, HEAD_DIM, page), F32), pltpu.SemaphoreType.DMA((1,))]),
        out_shape=[jax.ShapeDtypeStruct((s, 1, GROUP_W), F32), jax.ShapeDtypeStruct((s, 2, HEAD_DIM, wl), F32)],
        compiler_params=_cparams(("arbitrary",)),
        name="nsa_sample",
    )(page_table, idx, cache_t, z3, ocmp, win_t)


_MOBA_PAGES = 16


def _moba_gate_kernel(pt_ref, cache_ref, qcol_ref, idx_ref, buf_ref, part_ref, sem_ref,
                      *, layer, n_steps, page, nb, cur):
    b = pl.program_id(0)
    s = pl.program_id(1)
    g = _MOBA_PAGES
    step = b * n_steps + s
    total = pl.num_programs(0) * n_steps
    ppb = MOBA_BLOCK // page
    bps = g // ppb

    def copies(stp, slot):
        bb = stp // n_steps
        ss = stp % n_steps
        return [pltpu.make_async_copy(cache_ref.at[layer, pt_ref[bb, ss * g + k], 0],
                                      buf_ref.at[slot, k], sem_ref.at[slot]) for k in range(g)]

    @pl.when(step == 0)
    def _():
        for cp in copies(step, 0):
            cp.start()

    slot = step % 2

    @pl.when(step + 1 < total)
    def _():
        for cp in copies(step + 1, 1 - slot):
            cp.start()

    for cp in copies(step, slot):
        cp.wait()

    qcol = qcol_ref[...]
    for j in range(bps):
        for h in range(N_HEADS):
            acc = jnp.zeros((HEAD_DIM, page), F32)
            for k in range(ppb):
                acc = acc + buf_ref[slot, j * ppb + k, h] * qcol[h * HEAD_DIM:(h + 1) * HEAD_DIM, :]
            part_ref[pl.ds(h * nb + s * bps + j, 1), :] = jnp.sum(acc, axis=0, keepdims=True)

    @pl.when(s == n_steps - 1)
    def _():
        ones = jnp.ones((8, page), F32)
        gate = _dot_nt(ones, part_ref[...], precision=HIGHEST)[0:1] * (1.0 / MOBA_BLOCK)
        lane = lax.broadcasted_iota(jnp.int32, gate.shape, 1)
        out_lane = lax.broadcasted_iota(jnp.int32, (1, LANES), 1)
        idx = jnp.zeros((1, LANES), jnp.int32)
        for h in range(N_HEADS):
            n_of = lane - h * nb
            inh = (n_of >= 0) & (n_of < nb) & (n_of < cur)
            work = jnp.where(inh, jnp.maximum(gate, -3e38), -jnp.inf)
            for t in range(MOBA_TOPK):
                mx = jnp.max(work, axis=-1, keepdims=True)
                pick = jnp.min(jnp.where((work == mx) & inh, n_of, 4 * nb), axis=-1, keepdims=True)
                idx = jnp.where(out_lane == h * MOBA_TOPK + t, pick, idx)
                work = jnp.where(n_of == pick, -jnp.inf, work)
        idx_ref[...] = idx


def _moba_gate(page_table, cache_t, qcol, layer, cur):
    s, npg = page_table.shape
    page = cache_t.shape[-1]
    g = _MOBA_PAGES
    n_steps = npg // g
    nb = npg * page // MOBA_BLOCK
    kern = functools.partial(_moba_gate_kernel, layer=layer, n_steps=n_steps, page=page, nb=nb, cur=cur)
    return pl.pallas_call(
        kern,
        grid_spec=pltpu.PrefetchScalarGridSpec(
            num_scalar_prefetch=1,
            grid=(s, n_steps),
            in_specs=[pl.BlockSpec(memory_space=pl.ANY),
                      pl.BlockSpec((None, N_HEADS * HEAD_DIM, page), lambda b, i, pt: (b, 0, 0))],
            out_specs=pl.BlockSpec((None, 1, LANES), lambda b, i, pt: (b, 0, 0)),
            scratch_shapes=[pltpu.VMEM((2, g, N_HEADS, HEAD_DIM, page), F32),
                            pltpu.VMEM((N_HEADS * nb, page), F32),
                            pltpu.SemaphoreType.DMA((2,))]),
        out_shape=jax.ShapeDtypeStruct((s, 1, LANES), jnp.int32),
        compiler_params=_cparams(("arbitrary", "arbitrary")),
        name="moba_gate",
    )(page_table, cache_t, qcol)


def _moba_sample_kernel(pt_ref, idx_ref, cache_ref, z_ref, o_ref, buf_ref, sem_ref, *, layer, page, cur):
    b = pl.program_id(0)
    npg = pt_ref.shape[1]
    ppb = MOBA_BLOCK // page
    n_src = MOBA_TOPK * ppb

    def copy(h, t, k, kv):
        blk = jnp.minimum(idx_ref[b, h * MOBA_TOPK + t], npg // ppb - 1)
        return pltpu.make_async_copy(cache_ref.at[layer, pt_ref[b, blk * ppb + k], kv, h],
                                     buf_ref.at[kv, h, t * ppb + k], sem_ref.at[0])

    every = [(h, t, k, kv) for h in range(N_HEADS) for t in range(MOBA_TOPK) for k in range(ppb) for kv in range(2)]
    for a in every:
        copy(*a).start()
    z = z_ref[...]
    q = z[:, OFF_QMOBA:OFF_QMOBA + GROUP_W]
    k_new = z[:, OFF_KVMOBA:OFF_KVMOBA + GROUP_W]
    v_new = z[:, OFF_KVMOBA + GROUP_W:OFF_KVMOBA + 2 * GROUP_W]
    for a in every:
        copy(*a).wait()

    for h in range(N_HEADS):
        lo, hi = h * HEAD_DIM, (h + 1) * HEAD_DIM
        qh = jnp.concatenate([q[:, lo:hi]] * 8, axis=0).astype(BF16)
        qr = q[:, lo:hi].astype(BF16).astype(F32)
        s_new = jnp.sum(qr * k_new[:, lo:hi].astype(BF16).astype(F32), axis=-1, keepdims=True) * ATTN_SCALE
        scores = []
        m = s_new
        for t in range(MOBA_TOPK):
            ok = idx_ref[b, h * MOBA_TOPK + t] < cur
            for k in range(ppb):
                sc = _dot(qh, buf_ref[0, h, t * ppb + k].astype(BF16))[0:1] * ATTN_SCALE
                sc = jnp.where(ok, sc, -jnp.inf)
                scores.append(sc)
                m = jnp.maximum(m, jnp.max(sc, axis=-1, keepdims=True))
        e_new = jnp.exp(s_new - m)
        d = e_new
        es = []
        for sc in scores:
            e = jnp.exp(sc - m)
            es.append(e)
            d = d + jnp.sum(e, axis=-1, keepdims=True)
        inv = 1.0 / d
        o = (e_new * inv).astype(BF16).astype(F32) * v_new[:, lo:hi].astype(BF16).astype(F32)
        for n in range(n_src):
            pn = jnp.concatenate([es[n] * inv] * 8, axis=0).astype(BF16)
            o = o + _dot_nt(pn, buf_ref[1, h, n].astype(BF16))[0:1]
        o_ref[:, lo:hi] = o


def _moba_sample(page_table, idx, cache_t, z3, layer, cur):
    s = z3.shape[0]
    page = cache_t.shape[-1]
    ppb = MOBA_BLOCK // page
    kern = functools.partial(_moba_sample_kernel, layer=layer, page=page, cur=cur)
    return pl.pallas_call(
        kern,
        grid_spec=pltpu.PrefetchScalarGridSpec(
            num_scalar_prefetch=2,
            grid=(s,),
            in_specs=[pl.BlockSpec(memory_space=pl.ANY),
                      pl.BlockSpec((None, 1, N_IN_PAD), lambda b, pt, ix: (b, 0, 0))],
            out_specs=pl.BlockSpec((None, 1, GROUP_W), lambda b, pt, ix: (b, 0, 0)),
            scratch_shapes=[pltpu.VMEM((2, N_HEADS, MOBA_TOPK * ppb, HEAD_DIM, page), F32),
                            pltpu.SemaphoreType.DMA((1,))]),
        out_shape=jax.ShapeDtypeStruct((s, 1, GROUP_W), F32),
        compiler_params=_cparams(("arbitrary",)),
        name="moba_sample",
    )(page_table, idx, cache_t, z3)


def _cmp_weight_stack(w1):
    w = w1.reshape(2, 2, CMP_STRIDE, HEAD_DIM, HEAD_DIM)
    zero = jnp.zeros((CMP_STRIDE, HEAD_DIM, 2 * HEAD_DIM), F32)
    top = jnp.concatenate([w[0, 0], w[0, 1], zero], axis=-1)
    bot = jnp.concatenate([zero, w[1, 0], w[1, 1]], axis=-1)
    return jnp.concatenate([top, bot], axis=1).astype(BF16)


def kernel(x_prompt, x_sample, cache_moba_kv, cache_nsa_kv, state_nsa_win, state_conv, state_pool, page_table,
           w_in, conv_dw, conv_dw_b, conv_ln_g, conv_ln_b, conv_pw, nsa_cmp_pos, nsa_cmp_w1, nsa_cmp_w2,
           pool_w, pool_scale, w_out, ln1_g, ln1_b, w_up, w_down, ln2_g, ln2_b):
    bp, t, _ = x_prompt.shape
    bs = x_sample.shape[0]
    depth = w_in.shape[0]
    page = cache_nsa_kv.shape[2]
    past_len = page_table.shape[1] * page
    win_len = state_nsa_win.shape[2]
    assert x_sample.shape[1] == 1 and win_len == WINDOW and t % 512 == 0 and t >= WINDOW + 128
    assert page % L_SEL == 0 and MOBA_BLOCK % page == 0 and past_len % MOBA_BLOCK == 0
    assert past_len // MOBA_BLOCK >= MOBA_TOPK and past_len // L_SEL + 1 >= N_SEL and past_len >= POOL_MAX
    assert page_table.shape[1] % _CMP_PAGES == 0 and page_table.shape[1] % _MOBA_PAGES == 0

    nsa_t = jnp.transpose(cache_nsa_kv, (0, 1, 3, 4, 2))
    moba_t = jnp.transpose(cache_moba_kv, (0, 1, 3, 4, 5, 2))
    win_t = jnp.transpose(state_nsa_win, (0, 1, 3, 4, 2))
    conv_t = jnp.transpose(state_conv, (0, 2, 1, 3))
    pool_t = jnp.transpose(state_pool, (0, 2, 1, 3))

    perm = _in_perm()
    row = lambda v: v.reshape(1, -1)
    yp = x_prompt.reshape(bp * t, D_MODEL)
    ys = x_sample.reshape(bs, D_MODEL)
    st_p, st_s = [], []
    for l in range(depth):
        w_in_l = jnp.pad(w_in[l][:, perm], ((0, 0), (0, N_IN_PAD - N_IN))).astype(BF16)
        pw = conv_pw[l].astype(BF16)
        plw = jax.scipy.linalg.block_diag(*[pool_w[l, g] for g in range(len(POOL_WINDOWS))]).astype(BF16)
        w1 = nsa_cmp_w1[l].astype(BF16)
        w2 = nsa_cmp_w2[l].astype(BF16)
        pos = jnp.broadcast_to(nsa_cmp_pos[l].reshape(2, 1, L_CMP * HEAD_DIM), (2, 8, L_CMP * HEAD_DIM))
        w_out_l = w_out[l].astype(BF16)
        w_up_l = w_up[l].astype(BF16)
        w_down_l = w_down[l].astype(BF16)
        small = (conv_dw[l], row(conv_dw_b[l]), row(conv_ln_g[l]), row(conv_ln_b[l]), pw, plw, row(pool_scale[l]))

        z = _in_proj(yp, w_in_l)
        z3 = z.reshape(bp, t, N_IN_PAD)
        y_conv, y_pool, conv_new32 = _conv_pool_prompt(z3, *small)
        nchunk = t // CMP_STRIDE
        chunks_k = z3[:, :, OFF_KVNSA:OFF_KVNSA + HEAD_DIM].reshape(bp, nchunk, CMP_STRIDE * HEAD_DIM)
        chunks_v = z3[:, :, OFF_KVNSA + HEAD_DIM:OFF_KVNSA + 2 * HEAD_DIM].reshape(bp, nchunk, CMP_STRIDE * HEAD_DIM)
        kc, vc = _compress_prompt(chunks_k, chunks_v, pos, w1, w2)
        o_nsa = _nsa_prompt(z3, kc, vc)
        o_moba = _moba_prompt(z3)
        flat = lambda a: a.reshape(bp * t, GROUP_W)
        x1 = _out_proj((flat(y_conv), flat(o_nsa), flat(o_moba), flat(y_pool)), yp, w_out_l, row(ln1_g[l]), row(ln1_b[l]))
        yp = _ffn(x1, w_up_l, w_down_l, row(ln2_g[l]), row(ln2_b[l]))
        moba_rows = z3[:, :, OFF_KVMOBA:OFF_KVMOBA + 2 * GROUP_W].reshape(bp, t, 2, N_HEADS, HEAD_DIM)
        nsa_rows = z3[:, :, OFF_KVNSA:OFF_KVNSA + GROUP_W].reshape(bp, t, 4, HEAD_DIM)
        win_new = z3[:, t - win_len:, OFF_KVWIN:OFF_KVWIN + 2 * HEAD_DIM].reshape(bp, win_len, 2, HEAD_DIM)
        conv_new = conv_new32[:, _CONV_HALO - (CONV_W - 1):]
        pool_new = z3[:, t - (POOL_MAX - 1):, OFF_UPOOL:OFF_UPOOL + GROUP_W]
        st_p.append((moba_rows, nsa_rows, win_new, conv_new, pool_new))

        zs = _in_proj(ys, w_in_l)
        zs3 = zs.reshape(bs, 1, N_IN_PAD)
        ys_conv, ys_pool, conv_new_t, pool_new_t = _conv_pool_sample(zs, conv_t[l], pool_t[l], *small)
        c = _compress_pages(page_table, nsa_t, _cmp_weight_stack(nsa_cmp_w1[l]), l)
        o_cmp, sel_idx = _nsa_select(c, zs3, pos, w1, w2, past_len)
        os_nsa, win_new_t = _nsa_sample(page_table, sel_idx.reshape(bs, LANES), nsa_t, zs3, o_cmp, win_t, l, past_len)
        q_m = zs[:, OFF_QMOBA:OFF_QMOBA + GROUP_W]
        qcol = jnp.broadcast_to(q_m[:, :, None], (bs, GROUP_W, page))
        top_idx = _moba_gate(page_table, moba_t, qcol, l, past_len // MOBA_BLOCK)
        os_moba = _moba_sample(page_table, top_idx.reshape(bs, LANES), moba_t, zs3, l, past_len // MOBA_BLOCK)
        x1s = _out_proj((ys_conv, os_nsa.reshape(bs, GROUP_W), os_moba.reshape(bs, GROUP_W), ys_pool), ys, w_out_l,
                        row(ln1_g[l]), row(ln1_b[l]))
        ys = _ffn(x1s, w_up_l, w_down_l, row(ln2_g[l]), row(ln2_b[l]))
        st_s.append((zs[:, OFF_KVMOBA:OFF_KVMOBA + 2 * GROUP_W].reshape(bs, 1, 2, N_HEADS, HEAD_DIM),
                     zs[:, OFF_KVNSA:OFF_KVNSA + GROUP_W].reshape(bs, 1, 4, HEAD_DIM),
                     jnp.transpose(win_new_t, (0, 3, 1, 2)),
                     jnp.transpose(conv_new_t, (1, 0, 2)),
                     jnp.transpose(pool_new_t, (1, 0, 2))))

    stk = lambda sts, i: jnp.stack([s[i] for s in sts], axis=0)
    return (yp.reshape(bp, t, D_MODEL), ys.reshape(bs, 1, D_MODEL),
            stk(st_p, 0), stk(st_s, 0), stk(st_p, 1), stk(st_s, 1), stk(st_p, 2), stk(st_s, 2),
            stk(st_p, 3), stk(st_s, 3), stk(st_p, 4), stk(st_s, 4))
```

```python
import functools

import numpy as np
import jax
import jax.numpy as jnp
from jax import lax
from jax.experimental import pallas as pl
from jax.experimental.pallas import tpu as pltpu

F32 = jnp.float32
BF16 = jnp.bfloat16
HIGHEST = lax.Precision.HIGHEST

D_MODEL = 1024
HEAD_DIM = 64
GROUP_W = 256
N_HEADS = 4
CONV_W = 31
L_CMP = 32
CMP_STRIDE = 16
L_SEL = 64
N_SEL = 16
WINDOW = 512
MOBA_BLOCK = 256
MOBA_TOPK = 3
POOL_WINDOWS = (2, 4, 8, 16)
POOL_GW = 64
POOL_MAX = 16
D_FF = 4096
DEPTH = 2
ALPHA = (2 * DEPTH) ** 0.25
LN_EPS = 1e-5
ATTN_SCALE = HEAD_DIM ** -0.5
NEG = -1e30

OFF_UCONV, OFF_KVMOBA, OFF_QNSA, OFF_QMOBA = 0, 512, 1024, 1280
OFF_UPOOL, OFF_KVNSA, OFF_KVWIN, OFF_GATE = 1536, 1792, 2048, 2176
N_IN_PAD = 2304
N_IN = 2188
LANES = 128
VMEM_LIMIT = 56 * 1024 * 1024


def _in_perm():
    return np.concatenate([
        np.arange(0, 512), np.arange(1420, 1932), np.arange(512, 768), np.arange(1164, 1420),
        np.arange(1932, 2188), np.arange(768, 1024), np.arange(1024, 1152), np.arange(1152, 1164)])


def _ln(x, g, b):
    mu = jnp.mean(x, axis=-1, keepdims=True)
    xc = x - mu
    var = jnp.mean(xc * xc, axis=-1, keepdims=True)
    return xc * lax.rsqrt(var + LN_EPS) * g + b


def _sigmoid(x):
    return 1.0 / (1.0 + jnp.exp(-x))


def _silu(x):
    return x * _sigmoid(x)


def _dot(a, b, **kw):
    return jnp.dot(a, b, preferred_element_type=F32, **kw)


def _dot_nt(a, b, **kw):
    return lax.dot_general(a, b, (((1,), (1,)), ((), ())), preferred_element_type=F32, **kw)


def _masked_softmax(s, mask):
    r = s.shape[0]
    s = jnp.where(mask, s, -jnp.inf)
    m = jnp.max(_lane_partial(s, jnp.maximum), axis=-1, keepdims=True)
    m = jnp.broadcast_to(jnp.where(m == -jnp.inf, 0.0, m), (r, LANES))
    e = _exp_shifted(s, m)
    d = jnp.sum(_lane_partial(e, jnp.add), axis=-1, keepdims=True)
    inv = jnp.broadcast_to(1.0 / jnp.where(d > 0, d, 1.0), (r, LANES))
    return jnp.concatenate([e[:, g * LANES:(g + 1) * LANES] * inv for g in range(s.shape[1] // LANES)], axis=-1)


def _lane_partial(s, op):
    out = s[:, 0:LANES]
    for g in range(1, s.shape[1] // LANES):
        out = op(out, s[:, g * LANES:(g + 1) * LANES])
    return out


def _exp_shifted(s, m):
    return jnp.concatenate([jnp.exp(s[:, g * LANES:(g + 1) * LANES] - m) for g in range(s.shape[1] // LANES)], axis=-1)


def _cparams(sem, vmem=None):
    return pltpu.CompilerParams(dimension_semantics=sem, vmem_limit_bytes=vmem or VMEM_LIMIT)


def _in_proj_kernel(x_ref, w_ref, o_ref):
    o_ref[...] = _dot(x_ref[...].astype(BF16), w_ref[...])


def _in_proj(x, w):
    m, k = x.shape
    n = w.shape[1]
    tm = min(m, 512)
    return pl.pallas_call(
        _in_proj_kernel,
        grid=(m // tm,),
        in_specs=[pl.BlockSpec((tm, k), lambda i: (i, 0)), pl.BlockSpec((k, n), lambda i: (0, 0))],
        out_specs=pl.BlockSpec((tm, n), lambda i: (i, 0)),
        out_shape=jax.ShapeDtypeStruct((m, n), F32),
        compiler_params=_cparams(("parallel",)),
        name="in_proj",
    )(x, w)


def _out_proj_kernel(a_ref, b_ref, c_ref, d_ref, x_ref, w_ref, g_ref, bt_ref, o_ref):
    mix = jnp.concatenate([a_ref[...], b_ref[...], c_ref[...], d_ref[...]], axis=-1).astype(BF16)
    y = _dot(mix, w_ref[...])
    o_ref[...] = _ln(ALPHA * x_ref[...] + y, g_ref[...], bt_ref[...])


def _out_proj(pieces, x, w, g, b):
    m = x.shape[0]
    tm = min(m, 512)
    pspec = pl.BlockSpec((tm, GROUP_W), lambda i: (i, 0))
    return pl.pallas_call(
        _out_proj_kernel,
        grid=(m // tm,),
        in_specs=[pspec, pspec, pspec, pspec,
                  pl.BlockSpec((tm, D_MODEL), lambda i: (i, 0)),
                  pl.BlockSpec((D_MODEL, D_MODEL), lambda i: (0, 0)),
                  pl.BlockSpec((1, D_MODEL), lambda i: (0, 0)),
                  pl.BlockSpec((1, D_MODEL), lambda i: (0, 0))],
        out_specs=pl.BlockSpec((tm, D_MODEL), lambda i: (i, 0)),
        out_shape=jax.ShapeDtypeStruct((m, D_MODEL), F32),
        compiler_params=_cparams(("parallel",)),
        name="out_proj_ln",
    )(*pieces, x, w, g, b)


def _ffn_kernel(x_ref, wu_ref, wd_ref, g_ref, b_ref, o_ref, acc_ref):
    j = pl.program_id(1)

    @pl.when(j == 0)
    def _():
        acc_ref[...] = jnp.zeros_like(acc_ref)

    h = jnp.maximum(_dot(x_ref[...].astype(BF16), wu_ref[...]), 0.0)
    acc_ref[...] += _dot((h * h).astype(BF16), wd_ref[...])

    @pl.when(j == pl.num_programs(1) - 1)
    def _():
        o_ref[...] = _ln(ALPHA * x_ref[...] + acc_ref[...], g_ref[...], b_ref[...])


def _ffn(x, wu, wd, g, b):
    m = x.shape[0]
    tm = min(m, 1024)
    tf = 1024
    return pl.pallas_call(
        _ffn_kernel,
        grid=(m // tm, D_FF // tf),
        in_specs=[pl.BlockSpec((tm, D_MODEL), lambda i, j: (i, 0)),
                  pl.BlockSpec((D_MODEL, tf), lambda i, j: (0, j)),
                  pl.BlockSpec((tf, D_MODEL), lambda i, j: (j, 0)),
                  pl.BlockSpec((1, D_MODEL), lambda i, j: (0, 0)),
                  pl.BlockSpec((1, D_MODEL), lambda i, j: (0, 0))],
        out_specs=pl.BlockSpec((tm, D_MODEL), lambda i, j: (i, 0)),
        out_shape=jax.ShapeDtypeStruct((m, D_MODEL), F32),
        scratch_shapes=[pltpu.VMEM((tm, D_MODEL), F32)],
        compiler_params=_cparams(("parallel", "arbitrary")),
        name="ffn_ln",
    )(x, wu, wd, g, b)


_CONV_HALO = 32
_POOL_HALO = 16


def _pool_groups(rows_ref, buf_ref, tt, cnt_fn):
    n = _POOL_HALO + tt
    lane = lax.broadcasted_iota(jnp.int32, (1, GROUP_W), 1)
    cur = rows_ref[pl.ds(_POOL_HALO, tt), :]
    out = jnp.zeros((tt, GROUP_W), F32)
    shift = 1
    for g, w in enumerate(POOL_WINDOWS):
        src = rows_ref if g == 0 else buf_ref.at[(g - 1) % 2]
        dst = buf_ref.at[g % 2]
        lo = 2 * shift - 1
        dst[pl.ds(lo, n - lo), :] = src[pl.ds(lo, n - lo), :] + src[pl.ds(lo - shift, n - lo), :]
        shift *= 2
        s = dst[pl.ds(_POOL_HALO, tt), :]
        mean = s / cnt_fn(w)
        out = jnp.where((lane >= g * POOL_GW) & (lane < (g + 1) * POOL_GW), mean, out)
    return out - cur


def _conv_pool_prompt_kernel(u_ref, uh_ref, p_ref, ph_ref, dw_ref, dwb_ref, lg_ref, lb_ref, pw_ref,
                             plw_ref, pls_ref, yc_ref, yp_ref, cnew_ref, zc_ref, rows_ref, buf_ref, *, tt):
    i = pl.program_id(1)
    first = i == 0
    u = u_ref[...]
    zg = u[:, :GROUP_W] * _sigmoid(u[:, GROUP_W:])
    uh = uh_ref[...]
    zh = uh[:, :GROUP_W] * _sigmoid(uh[:, GROUP_W:])
    zc_ref[pl.ds(0, _CONV_HALO), :] = jnp.where(first, 0.0, zh)
    zc_ref[pl.ds(_CONV_HALO, tt), :] = zg
    y = jnp.zeros((tt, GROUP_W), F32) + dwb_ref[...]
    for k in range(CONV_W):
        y = y + dw_ref[k:k + 1, :] * zc_ref[pl.ds(_CONV_HALO - (CONV_W - 1) + k, tt), :]
    y = _silu(_ln(y, lg_ref[...], lb_ref[...]))
    yc_ref[...] = _dot(y.astype(BF16), pw_ref[...])

    @pl.when(i == pl.num_programs(1) - 1)
    def _():
        cnew_ref[...] = zc_ref[pl.ds(tt, _CONV_HALO), :]

    rows_ref[pl.ds(0, _POOL_HALO), :] = jnp.where(first, 0.0, ph_ref[...])
    rows_ref[pl.ds(_POOL_HALO, tt), :] = p_ref[...]
    qpos1 = i * tt + lax.broadcasted_iota(jnp.int32, (tt, 1), 0) + 1
    d = _pool_groups(rows_ref, buf_ref, tt, lambda w: jnp.minimum(w, qpos1).astype(F32))
    yp_ref[...] = _dot(d.astype(BF16), plw_ref[...]) * pls_ref[...]


def _conv_pool_prompt(z3, dw, dwb, lg, lb, pw, plw, pls):
    b, t, _ = z3.shape
    tt = 512
    nt = t // tt
    kern = functools.partial(_conv_pool_prompt_kernel, tt=tt)
    cst = lambda shape: pl.BlockSpec(shape, lambda bi, i: (0,) * len(shape))
    return pl.pallas_call(
        kern,
        grid=(b, nt),
        in_specs=[
            pl.BlockSpec((None, tt, 512), lambda bi, i: (bi, i, OFF_UCONV // 512)),
            pl.BlockSpec((None, _CONV_HALO, 512),
                         lambda bi, i: (bi, jnp.maximum(i * (tt // _CONV_HALO) - 1, 0), OFF_UCONV // 512)),
            pl.BlockSpec((None, tt, GROUP_W), lambda bi, i: (bi, i, OFF_UPOOL // GROUP_W)),
            pl.BlockSpec((None, _POOL_HALO, GROUP_W),
                         lambda bi, i: (bi, jnp.maximum(i * (tt // _POOL_HALO) - 1, 0), OFF_UPOOL // GROUP_W)),
            cst((CONV_W, GROUP_W)), cst((1, GROUP_W)), cst((1, GROUP_W)), cst((1, GROUP_W)),
            cst((GROUP_W, GROUP_W)), cst((GROUP_W, GROUP_W)), cst((1, GROUP_W)),
        ],
        out_specs=[
            pl.BlockSpec((None, tt, GROUP_W), lambda bi, i: (bi, i, 0)),
            pl.BlockSpec((None, tt, GROUP_W), lambda bi, i: (bi, i, 0)),
            pl.BlockSpec((None, _CONV_HALO, GROUP_W), lambda bi, i: (bi, 0, 0)),
        ],
        out_shape=[jax.ShapeDtypeStruct((b, t, GROUP_W), F32), jax.ShapeDtypeStruct((b, t, GROUP_W), F32),
                   jax.ShapeDtypeStruct((b, _CONV_HALO, GROUP_W), F32)],
        scratch_shapes=[pltpu.VMEM((_CONV_HALO + tt, GROUP_W), F32),
                        pltpu.VMEM((_POOL_HALO + tt, GROUP_W), F32),
                        pltpu.VMEM((2, _POOL_HALO + tt, GROUP_W), F32)],
        compiler_params=_cparams(("parallel", "arbitrary")),
        name="conv_pool_prompt",
    )(z3, z3, z3, z3, dw, dwb, lg, lb, pw, plw, pls)


def _compress_tail(c, bias_k, bias_v, w2k, w2v):
    n = c.shape[0]

    def one(c0, c1, bias, w2):
        hid = c0 + pltpu.roll(c1, n - 1, 0) + bias
        return _dot(_silu(hid).astype(BF16), w2)

    kc = one(c[:, 0:64], c[:, 64:128], bias_k, w2k)
    vc = one(c[:, 128:192], c[:, 192:256], bias_v, w2v)
    return kc, vc


def _cmp_bias(pos_ref, w1_ref):
    bk = _dot(pos_ref[0].astype(BF16), w1_ref[0])[0:1]
    bv = _dot(pos_ref[1].astype(BF16), w1_ref[1])[0:1]
    return bk, bv


def _compress_prompt_kernel(ck_ref, cv_ref, pos_ref, w1_ref, w2_ref, kc_ref, vc_ref):
    bk, bv = _cmp_bias(pos_ref, w1_ref)
    half = CMP_STRIDE * HEAD_DIM
    ck = ck_ref[...].astype(BF16)
    cv = cv_ref[...].astype(BF16)
    c = jnp.concatenate([_dot(ck, w1_ref[0, 0:half, :]), _dot(ck, w1_ref[0, half:2 * half, :]),
                         _dot(cv, w1_ref[1, 0:half, :]), _dot(cv, w1_ref[1, half:2 * half, :])], axis=-1)
    kc, vc = _compress_tail(c, bk, bv, w2_ref[0], w2_ref[1])
    kc_ref[...] = kc
    vc_ref[...] = vc


def _compress_prompt(chunks_k, chunks_v, pos, w1, w2):
    b, nc, kk = chunks_k.shape
    cst = lambda shape: pl.BlockSpec(shape, lambda bi: (0,) * len(shape))
    return pl.pallas_call(
        _compress_prompt_kernel,
        grid=(b,),
        in_specs=[pl.BlockSpec((None, nc, kk), lambda bi: (bi, 0, 0)),
                  pl.BlockSpec((None, nc, kk), lambda bi: (bi, 0, 0)),
                  cst(pos.shape), cst(w1.shape), cst(w2.shape)],
        out_specs=[pl.BlockSpec((None, nc, HEAD_DIM), lambda bi: (bi, 0, 0)),
                   pl.BlockSpec((None, nc, HEAD_DIM), lambda bi: (bi, 0, 0))],
        out_shape=[jax.ShapeDtypeStruct((b, nc, HEAD_DIM), F32)] * 2,
        compiler_params=_cparams(("parallel",)),
        name="compress_prompt",
    )(chunks_k, chunks_v, pos, w1, w2)


def _rank_desc(v, n):
    lane = lax.broadcasted_iota(jnp.int32, v.shape, 1)
    rank = jnp.zeros(v.shape, F32)
    for j in range(n):
        col = v[:, j:j + 1]
        rank = rank + jnp.where((col > v) | ((col == v) & (lane > j)), 1.0, 0.0)
    return rank


def _cover_matrix(n_cmp_pad, n_cmp, n_sel_pad):
    j = np.arange(n_cmp_pad)[:, None]
    i = np.arange(n_sel_pad)[None, :]
    cov = (j * CMP_STRIDE < (i + 1) * L_SEL) & (j * CMP_STRIDE + L_CMP > i * L_SEL) & (j < n_cmp)
    return jnp.asarray(cov.astype(np.float32))


def _nsa_prompt_kernel(q_ref, g_ref, kv_ref, win_ref, kc_ref, vc_ref, covert_ref, o_ref,
                       kt_ref, v_ref, kwt_ref, vw_ref, qa_ref, so_ref, m_ref, l_ref, acc_ref, *, c, tk, n_cmp, ns):
    i = pl.program_id(1)
    qs = i * c
    t = kv_ref.shape[0]
    nsp = LANES - HEAD_DIM

    @pl.when(i == 0)
    def _():
        rows = lax.broadcasted_iota(jnp.int32, (nsp, t), 0)
        cols = lax.broadcasted_iota(jnp.int32, (nsp, t), 1)
        kt_ref[HEAD_DIM:LANES, :] = jnp.where(rows == cols // L_SEL, 1.0, 0.0).astype(BF16)
        rt = 256
        for n in range(t // rt):
            x = kv_ref[n * rt:(n + 1) * rt, :]
            kt_ref[0:HEAD_DIM, n * rt:(n + 1) * rt] = x.T[2 * HEAD_DIM:3 * HEAD_DIM].astype(BF16)
            v_ref[n * rt:(n + 1) * rt, :] = x[:, 3 * HEAD_DIM:4 * HEAD_DIM].astype(BF16)
            w = win_ref[n * rt:(n + 1) * rt, :]
            kwt_ref[:, n * rt:(n + 1) * rt] = w.T[0:HEAD_DIM].astype(BF16)
            vw_ref[n * rt:(n + 1) * rt, :] = w[:, HEAD_DIM:2 * HEAD_DIM].astype(BF16)

    q = q_ref[...] * ATTN_SCALE
    q4 = jnp.concatenate([q[:, h * HEAD_DIM:(h + 1) * HEAD_DIM] for h in range(N_HEADS)], axis=0).astype(BF16)
    qpos = qs + lax.broadcasted_iota(jnp.int32, (c, 1), 0)
    qpos4 = jnp.concatenate([qpos] * N_HEADS, axis=0)

    nc = kc_ref.shape[0]
    s = _dot_nt(q4, kc_ref[...].astype(BF16))
    jj = lax.broadcasted_iota(jnp.int32, (1, nc), 1)
    p = _masked_softmax(s, (jj * CMP_STRIDE + (L_CMP - 1) <= qpos4) & (jj < n_cmp))
    o_cmp = _dot(p.astype(BF16), vc_ref[...].astype(BF16))
    psum = p[0:c] + p[c:2 * c] + p[2 * c:3 * c] + p[3 * c:4 * c]

    imp = _dot_nt(covert_ref[...], psum, precision=HIGHEST)
    blk = lax.broadcasted_iota(jnp.int32, (nsp, 1), 0)
    qpos_l = qs + lax.broadcasted_iota(jnp.int32, (1, c), 1)
    cur = qpos_l // L_SEL
    imp = jnp.where((blk == 0) | (blk == cur) | (blk == cur - 1), jnp.inf, imp)
    imp = jnp.where(blk * L_SEL > qpos_l, -jnp.inf, imp)
    rank = jnp.zeros((nsp, c), F32)
    for b in range(ns):
        rb = imp[b:b + 1, :]
        rank = rank + jnp.where((rb > imp) | ((rb == imp) & (blk > b)), 1.0, 0.0)
    sel = (rank < N_SEL) & (blk * L_SEL <= qpos_l)
    bias_t = jnp.where(sel, 0.0, NEG)
    bias = jnp.concatenate([bias_t, jnp.zeros((LANES - nsp, c), F32)], axis=0).T[:, 0:nsp]
    qa = jnp.concatenate([q4, jnp.concatenate([bias] * N_HEADS, axis=0).astype(BF16)], axis=-1)

    jt = qs // tk
    k_own = pl.multiple_of(jt * tk, tk)
    qa_ref[...] = qa
    kpos = k_own + lax.broadcasted_iota(jnp.int32, (1, tk), 1)
    sc = jnp.where(kpos <= qpos4, _dot(qa, kt_ref[:, pl.ds(k_own, tk)]), NEG)
    so_ref[...] = sc
    m_ref[...] = _lane_partial(sc, jnp.maximum)

    def max_body(j, carry):
        k0 = pl.multiple_of(j * tk, tk)
        sc = _dot(qa_ref[...], kt_ref[:, pl.ds(k0, tk)])
        m_ref[...] = jnp.maximum(m_ref[...], _lane_partial(sc, jnp.maximum))
        return carry

    lax.fori_loop(0, jt, max_body, 0)
    m = jnp.broadcast_to(jnp.max(m_ref[...], axis=-1, keepdims=True), m_ref.shape)
    m_ref[...] = m
    pe = _exp_shifted(so_ref[...], m)
    l_ref[...] = _lane_partial(pe, jnp.add)
    acc_ref[...] = _dot(pe.astype(BF16), v_ref[pl.ds(k_own, tk), :])

    def sum_body(j, carry):
        k0 = pl.multiple_of(j * tk, tk)
        pe = _exp_shifted(_dot(qa_ref[...], kt_ref[:, pl.ds(k0, tk)]), m_ref[...])
        l_ref[...] += _lane_partial(pe, jnp.add)
        acc_ref[...] += _dot(pe.astype(BF16), v_ref[pl.ds(k0, tk), :])
        return carry

    lax.fori_loop(0, jt, sum_body, 0)
    o_sel = acc_ref[...] * (1.0 / jnp.sum(l_ref[...], axis=-1, keepdims=True))

    wl = WINDOW + c
    ks = pl.multiple_of(jnp.maximum(qs - WINDOW, 0), c)
    rel = qpos4 - (ks + lax.broadcasted_iota(jnp.int32, (1, wl), 1))
    sw = jnp.where((rel >= 0) & (rel <= WINDOW), _dot(q4, kwt_ref[:, pl.ds(ks, wl)]), NEG)
    mw = jnp.broadcast_to(jnp.max(_lane_partial(sw, jnp.maximum), axis=-1, keepdims=True), (N_HEADS * c, LANES))
    ew = _exp_shifted(sw, mw)
    lw = jnp.sum(_lane_partial(ew, jnp.add), axis=-1, keepdims=True)
    o_win = _dot(ew.astype(BF16), vw_ref[pl.ds(ks, wl), :]) * (1.0 / lw)

    g = _sigmoid(g_ref[...])

    def gate(br):
        return jnp.concatenate([g[:, br * N_HEADS + h:br * N_HEADS + h + 1] for h in range(N_HEADS)], axis=0)

    o = gate(0) * o_cmp + gate(1) * o_sel + gate(2) * o_win
    for h in range(N_HEADS):
        o_ref[:, h * HEAD_DIM:(h + 1) * HEAD_DIM] = o[h * c:(h + 1) * c]


def _nsa_prompt(z3, kc, vc):
    b, t, _ = z3.shape
    c = 128
    tk = 512
    nc = kc.shape[1]
    n_cmp = t // CMP_STRIDE - 1
    ns = t // L_SEL
    nsp = LANES - HEAD_DIM
    assert ns <= nsp and c == LANES
    covert = _cover_matrix(nc, n_cmp, nsp).T
    kern = functools.partial(_nsa_prompt_kernel, c=c, tk=tk, n_cmp=n_cmp, ns=ns)
    return pl.pallas_call(
        kern,
        grid=(b, t // c),
        in_specs=[
            pl.BlockSpec((None, c, GROUP_W), lambda bi, i: (bi, i, OFF_QNSA // GROUP_W)),
            pl.BlockSpec((None, c, LANES), lambda bi, i: (bi, i, OFF_GATE // LANES)),
            pl.BlockSpec((None, t, GROUP_W), lambda bi, i: (bi, 0, OFF_KVNSA // GROUP_W)),
            pl.BlockSpec((None, t, LANES), lambda bi, i: (bi, 0, OFF_KVWIN // LANES)),
            pl.BlockSpec((None, nc, HEAD_DIM), lambda bi, i: (bi, 0, 0)),
            pl.BlockSpec((None, nc, HEAD_DIM), lambda bi, i: (bi, 0, 0)),
            pl.BlockSpec((nsp, nc), lambda bi, i: (0, 0)),
        ],
        out_specs=pl.BlockSpec((None, c, GROUP_W), lambda bi, i: (bi, i, 0)),
        out_shape=jax.ShapeDtypeStruct((b, t, GROUP_W), F32),
        scratch_shapes=[pltpu.VMEM((LANES, t), BF16), pltpu.VMEM((t, HEAD_DIM), BF16),
                        pltpu.VMEM((HEAD_DIM, t), BF16), pltpu.VMEM((t, HEAD_DIM), BF16),
                        pltpu.VMEM((N_HEADS * c, LANES), BF16), pltpu.VMEM((N_HEADS * c, tk), F32),
                        pltpu.VMEM((N_HEADS * c, LANES), F32), pltpu.VMEM((N_HEADS * c, LANES), F32),
                        pltpu.VMEM((N_HEADS * c, HEAD_DIM), F32)],
        compiler_params=_cparams(("parallel", "arbitrary")),
        name="nsa_prompt",
    )(z3, z3, z3, z3, kc, vc, covert)


def _moba_prompt_kernel(q_ref, kv_ref, o_ref, kmean_ref, kt_ref, v_ref, qa_ref, so_ref, m_ref, l_ref, acc_ref,
                        *, nb):
    i = pl.program_id(1)
    c = MOBA_BLOCK
    t = kv_ref.shape[0]

    @pl.when(i == 0)
    def _():
        rows = lax.broadcasted_iota(jnp.int32, (LANES - HEAD_DIM, t), 0)
        cols = lax.broadcasted_iota(jnp.int32, (LANES - HEAD_DIM, t), 1)
        onehot = jnp.where(rows == cols // c, 1.0, 0.0).astype(BF16)
        for h in range(N_HEADS):
            kt_ref[h, HEAD_DIM:LANES, :] = onehot
        kmean_ref[...] = jnp.zeros(kmean_ref.shape, F32)
        for n in range(nb):
            x = kv_ref[n * c:(n + 1) * c, :]
            kmean_ref[n:n + 1, :] = jnp.mean(x[:, 0:GROUP_W], axis=0, keepdims=True)
            xt = x[:, 0:GROUP_W].T
            for h in range(N_HEADS):
                lo, hi = h * HEAD_DIM, (h + 1) * HEAD_DIM
                kt_ref[h, 0:HEAD_DIM, n * c:(n + 1) * c] = xt[lo:hi].astype(BF16)
                v_ref[h, n * c:(n + 1) * c, :] = x[:, GROUP_W + lo:GROUP_W + hi].astype(BF16)

    q = q_ref[...]
    nbp = LANES - HEAD_DIM
    nbr = -(-nb // 8) * 8
    blk = lax.broadcasted_iota(jnp.int32, (nbr, 1), 0)
    fblk = blk.astype(F32)
    past = blk < i
    row = lax.broadcasted_iota(jnp.int32, (c, c), 0)
    col = lax.broadcasted_iota(jnp.int32, (c, c), 1)
    tri = col <= row
    own0 = pl.multiple_of(i * c, c)
    for h in range(N_HEADS):
        lo, hi = h * HEAD_DIM, (h + 1) * HEAD_DIM
        qh = q[:, lo:hi]
        work = jnp.where(past, _dot_nt(kmean_ref[0:nbr, lo:hi], qh, precision=HIGHEST), -jnp.inf)
        sel = jnp.zeros((nbr, c), jnp.bool_)
        for _ in range(MOBA_TOPK):
            mx = jnp.max(work, axis=0, keepdims=True)
            pick = jnp.min(jnp.where(work == mx, fblk, float(nbp)), axis=0, keepdims=True)
            hit = fblk == pick
            sel = sel | (hit & (mx > -jnp.inf))
            work = jnp.where(hit, -jnp.inf, work)
        bias_t = jnp.where(sel, 0.0, NEG)
        bias = jnp.concatenate([bias_t, jnp.full((LANES - nbr, c), NEG, F32)], axis=0).T[:, 0:nbp]
        qs = (qh * ATTN_SCALE).astype(BF16)
        qa_ref[h] = jnp.concatenate([qs, bias.astype(BF16)], axis=-1)
        sc = jnp.where(tri, _dot(qs, kt_ref[h, 0:HEAD_DIM, pl.ds(own0, c)]), NEG)
        so_ref[h] = sc
        m_ref[h] = _lane_partial(sc, jnp.maximum)

    def max_body(j, carry):
        k0 = pl.multiple_of(j * c, c)
        for h in range(N_HEADS):
            sc = _dot(qa_ref[h], kt_ref[h, :, pl.ds(k0, c)])
            m_ref[h] = jnp.maximum(m_ref[h], _lane_partial(sc, jnp.maximum))
        return carry

    lax.fori_loop(0, i, max_body, 0)
    for h in range(N_HEADS):
        m = jnp.broadcast_to(jnp.max(m_ref[h], axis=-1, keepdims=True), (c, LANES))
        m_ref[h] = m
        pe = _exp_shifted(so_ref[h], m)
        l_ref[h] = _lane_partial(pe, jnp.add)
        acc_ref[h] = _dot(pe.astype(BF16), v_ref[h, pl.ds(own0, c), :])

    def sum_body(j, carry):
        k0 = pl.multiple_of(j * c, c)
        for h in range(N_HEADS):
            pe = _exp_shifted(_dot(qa_ref[h], kt_ref[h, :, pl.ds(k0, c)]), m_ref[h])
            l_ref[h] += _lane_partial(pe, jnp.add)
            acc_ref[h] += _dot(pe.astype(BF16), v_ref[h, pl.ds(k0, c), :])
        return carry

    lax.fori_loop(0, i, sum_body, 0)
    for h in range(N_HEADS):
        l = jnp.sum(l_ref[h], axis=-1, keepdims=True)
        o_ref[:, h * HEAD_DIM:(h + 1) * HEAD_DIM] = acc_ref[h] * (1.0 / l)


def _moba_prompt(z3):
    b, t, _ = z3.shape
    nb = t // MOBA_BLOCK
    assert nb <= LANES - HEAD_DIM
    kern = functools.partial(_moba_prompt_kernel, nb=nb)
    return pl.pallas_call(
        kern,
        grid=(b, nb),
        in_specs=[pl.BlockSpec((None, MOBA_BLOCK, GROUP_W), lambda bi, i: (bi, i, OFF_QMOBA // GROUP_W)),
                  pl.BlockSpec((None, t, 2 * GROUP_W), lambda bi, i: (bi, 0, OFF_KVMOBA // (2 * GROUP_W)))],
        out_specs=pl.BlockSpec((None, MOBA_BLOCK, GROUP_W), lambda bi, i: (bi, i, 0)),
        out_shape=jax.ShapeDtypeStruct((b, t, GROUP_W), F32),
        scratch_shapes=[pltpu.VMEM((LANES - HEAD_DIM, GROUP_W), F32),
                        pltpu.VMEM((N_HEADS, LANES, t), BF16),
                        pltpu.VMEM((N_HEADS, t, HEAD_DIM), BF16),
                        pltpu.VMEM((N_HEADS, MOBA_BLOCK, LANES), BF16),
                        pltpu.VMEM((N_HEADS, MOBA_BLOCK, MOBA_BLOCK), F32),
                        pltpu.VMEM((N_HEADS, MOBA_BLOCK, LANES), F32),
                        pltpu.VMEM((N_HEADS, MOBA_BLOCK, LANES), F32),
                        pltpu.VMEM((N_HEADS, MOBA_BLOCK, HEAD_DIM), F32)],
        compiler_params=_cparams(("parallel", "arbitrary")),
        name="moba_prompt",
    )(z3, z3)


def _conv_pool_sample_kernel(z_ref, cst_ref, pst_ref, dw_ref, dwb_ref, lg_ref, lb_ref, pw_ref, plw_ref, pls_ref,
                             yc_ref, yp_ref, cnew_ref, pnew_ref):
    u = z_ref[:, OFF_UCONV:OFF_UCONV + 2 * GROUP_W]
    zg = u[:, :GROUP_W] * _sigmoid(u[:, GROUP_W:])
    nst = CONV_W - 1
    y = dwb_ref[...] + dw_ref[nst:nst + 1, :] * zg
    for k in range(nst):
        y = y + dw_ref[k:k + 1, :] * cst_ref[k]
    y = _silu(_ln(y, lg_ref[...], lb_ref[...]))
    yc_ref[...] = _dot(y.astype(BF16), pw_ref[...])
    for k in range(nst - 1):
        cnew_ref[k] = cst_ref[k + 1]
    cnew_ref[nst - 1] = zg

    p = z_ref[:, OFF_UPOOL:OFF_UPOOL + GROUP_W]
    npst = POOL_MAX - 1
    lane = lax.broadcasted_iota(jnp.int32, (1, GROUP_W), 1)
    run = p
    mean = jnp.zeros_like(p)
    k = 1
    for g, w in enumerate(POOL_WINDOWS):
        while k < w:
            run = run + pst_ref[npst - k]
            k += 1
        mean = jnp.where((lane >= g * POOL_GW) & (lane < (g + 1) * POOL_GW), run / float(w), mean)
    d = mean - p
    yp_ref[...] = _dot(d.astype(BF16), plw_ref[...]) * pls_ref[...]
    for k in range(npst - 1):
        pnew_ref[k] = pst_ref[k + 1]
    pnew_ref[npst - 1] = p


def _conv_pool_sample(z, cst, pst, dw, dwb, lg, lb, pw, plw, pls):
    s = z.shape[0]
    return pl.pallas_call(
        _conv_pool_sample_kernel,
        out_shape=[jax.ShapeDtypeStruct((s, GROUP_W), F32), jax.ShapeDtypeStruct((s, GROUP_W), F32),
                   jax.ShapeDtypeStruct(cst.shape, F32), jax.ShapeDtypeStruct(pst.shape, F32)],
        compiler_params=pltpu.CompilerParams(vmem_limit_bytes=VMEM_LIMIT),
        name="conv_pool_sample",
    )(z, cst, pst, dw, dwb, lg, lb, pw, plw, pls)


_CMP_PAGES = 16


def _compress_pages_kernel(pt_ref, cache_ref, w_ref, c_ref, buf_ref, rows_ref, sem_ref, *, layer, n_steps, page):
    b = pl.program_id(0)
    s = pl.program_id(1)
    g = _CMP_PAGES
    step = b * n_steps + s
    total = pl.num_programs(0) * n_steps

    def copies(stp, slot):
        bb = stp // n_steps
        ss = stp % n_steps
        return [pltpu.make_async_copy(cache_ref.at[layer, pt_ref[bb, ss * g + k], pl.ds(0, 2)],
                                      buf_ref.at[slot, k], sem_ref.at[slot]) for k in range(g)]

    @pl.when(step == 0)
    def _():
        for cp in copies(step, 0):
            cp.start()

    slot = step % 2

    @pl.when(step + 1 < total)
    def _():
        for cp in copies(step + 1, 1 - slot):
            cp.start()

    for cp in copies(step, slot):
        cp.wait()

    for k in range(g):
        x = buf_ref[slot, k].reshape(2 * HEAD_DIM, page)
        rows_ref[pl.ds(k * page, page), :] = x.T
    nchunk = g * page // CMP_STRIDE
    acc = jnp.zeros((nchunk, 4 * HEAD_DIM), F32)
    for r in range(CMP_STRIDE):
        xr = rows_ref[pl.ds(r, nchunk, stride=CMP_STRIDE), :].astype(BF16)
        acc = acc + _dot(xr, w_ref[r])
    c_ref[...] = acc


def _compress_pages(page_table, cache_t, wstack, layer):
    s, npg = page_table.shape
    page = cache_t.shape[-1]
    g = _CMP_PAGES
    n_steps = npg // g
    nchunk = g * page // CMP_STRIDE
    kern = functools.partial(_compress_pages_kernel, layer=layer, n_steps=n_steps, page=page)
    return pl.pallas_call(
        kern,
        grid_spec=pltpu.PrefetchScalarGridSpec(
            num_scalar_prefetch=1,
            grid=(s, n_steps),
            in_specs=[pl.BlockSpec(memory_space=pl.ANY),
                      pl.BlockSpec(wstack.shape, lambda b, i, pt: (0, 0, 0))],
            out_specs=pl.BlockSpec((None, nchunk, 4 * HEAD_DIM), lambda b, i, pt: (b, i, 0)),
            scratch_shapes=[pltpu.VMEM((2, g, 2, HEAD_DIM, page), F32),
                            pltpu.VMEM((g * page, 2 * HEAD_DIM), F32),
                            pltpu.SemaphoreType.DMA((2,))]),
        out_shape=jax.ShapeDtypeStruct((s, n_steps * nchunk, 4 * HEAD_DIM), F32),
        compiler_params=_cparams(("arbitrary", "arbitrary")),
        name="compress_pages",
    )(page_table, cache_t, wstack)


def _stack_heads(q):
    rows = [q[:, h * HEAD_DIM:(h + 1) * HEAD_DIM] for h in range(N_HEADS)]
    return jnp.concatenate(rows + [jnp.zeros((8 - N_HEADS, HEAD_DIM), q.dtype)], axis=0)


def _nsa_select_kernel(c_ref, z_ref, pos_ref, w1_ref, w2_ref, cover_ref, ocmp_ref, idx_ref, *, n_cmp, n_sel, qpos):
    bk, bv = _cmp_bias(pos_ref, w1_ref)
    kc, vc = _compress_tail(c_ref[...], bk, bv, w2_ref[0], w2_ref[1])
    q = z_ref[:, OFF_QNSA:OFF_QNSA + GROUP_W]
    q4 = _stack_heads(q)
    nc = kc.shape[0]
    s = _dot_nt(q4.astype(BF16), kc.astype(BF16)) * ATTN_SCALE
    jj = lax.broadcasted_iota(jnp.int32, (1, nc), 1)
    p = _masked_softmax(s, (jj * CMP_STRIDE + (L_CMP - 1) <= qpos) & (jj < n_cmp))
    ocmp_ref[...] = _dot(p.astype(BF16), vc.astype(BF16))[0:N_HEADS]
    psum = jnp.sum(p[0:N_HEADS], axis=0, keepdims=True)
    imp = _dot(jnp.concatenate([psum] * 8, axis=0), cover_ref[...], precision=HIGHEST)[0:1]
    nsp = imp.shape[1]
    blk = lax.broadcasted_iota(jnp.int32, (1, nsp), 1)
    cur = qpos // L_SEL
    imp = jnp.where((blk == 0) | (blk == cur) | (blk == cur - 1), jnp.inf, imp)
    imp = jnp.where(blk * L_SEL > qpos, -jnp.inf, imp)
    work = jnp.where(blk < n_sel, jnp.maximum(imp, -3e38), -jnp.inf)
    out_lane = lax.broadcasted_iota(jnp.int32, (1, LANES), 1)
    idx = jnp.zeros((1, LANES), jnp.int32)
    for t in range(N_SEL):
        mx = jnp.max(work, axis=-1, keepdims=True)
        pick = jnp.min(jnp.where(work == mx, blk, nsp), axis=-1, keepdims=True)
        idx = jnp.where(out_lane == t, pick, idx)
        work = jnp.where(blk == pick, -jnp.inf, work)
    idx_ref[...] = idx


def _nsa_select(c, z3, pos, w1, w2, qpos):
    s, nc, _ = c.shape
    n_cmp = nc - 1
    n_sel = qpos // L_SEL + 1
    nsp = -(-n_sel // LANES) * LANES
    cover = _cover_matrix(nc, n_cmp, nsp)
    kern = functools.partial(_nsa_select_kernel, n_cmp=n_cmp, n_sel=n_sel, qpos=qpos)
    cst = lambda shape: pl.BlockSpec(shape, lambda b: (0,) * len(shape))
    return pl.pallas_call(
        kern,
        grid=(s,),
        in_specs=[pl.BlockSpec((None, nc, 4 * HEAD_DIM), lambda b: (b, 0, 0)),
                  pl.BlockSpec((None, 1, N_IN_PAD), lambda b: (b, 0, 0)),
                  cst(pos.shape), cst(w1.shape), cst(w2.shape), cst(cover.shape)],
        out_specs=[pl.BlockSpec((None, N_HEADS, HEAD_DIM), lambda b: (b, 0, 0)),
                   pl.BlockSpec((None, 1, LANES), lambda b: (b, 0, 0))],
        out_shape=[jax.ShapeDtypeStruct((s, N_HEADS, HEAD_DIM), F32), jax.ShapeDtypeStruct((s, 1, LANES), jnp.int32)],
        compiler_params=_cparams(("parallel",)),
        name="nsa_select",
    )(c, z3, pos, w1, w2, cover)


def _col_from_row(row):
    n = row.shape[1]
    eye = lax.broadcasted_iota(jnp.int32, (n, n), 0) == lax.broadcasted_iota(jnp.int32, (n, n), 1)
    return jnp.sum(jnp.where(eye, row, 0.0), axis=-1, keepdims=True)


def _nsa_sample_kernel(pt_ref, idx_ref, cache_ref, z_ref, ocmp_ref, win_ref, o_ref, wnew_ref, buf_ref, sem_ref,
                       *, layer, page, qpos):
    b = pl.program_id(0)
    npg = pt_ref.shape[1]
    per_page = page // L_SEL

    def copy(t):
        pg = jnp.minimum(idx_ref[b, t] // per_page, npg - 1)
        return pltpu.make_async_copy(cache_ref.at[layer, pt_ref[b, pg], pl.ds(2, 2)], buf_ref.at[t], sem_ref.at[0])

    for t in range(N_SEL):
        copy(t).start()

    z = z_ref[...]
    q = z[:, OFF_QNSA:OFF_QNSA + GROUP_W]
    q4b = _stack_heads(q).astype(BF16)
    kvn = z[:, OFF_KVNSA:OFF_KVNSA + GROUP_W]
    ks_new, vs_new = kvn[:, 2 * HEAD_DIM:3 * HEAD_DIM], kvn[:, 3 * HEAD_DIM:4 * HEAD_DIM]
    kvw = z[:, OFF_KVWIN:OFF_KVWIN + 2 * HEAD_DIM]
    kw_new, vw_new = kvw[:, 0:HEAD_DIM], kvw[:, HEAD_DIM:2 * HEAD_DIM]
    q4r = q4b.astype(F32)

    def new_score(k_new):
        kb = k_new.astype(BF16).astype(F32)
        return jnp.sum(q4r * kb, axis=-1, keepdims=True) * ATTN_SCALE

    wk = win_ref[0]
    wv = win_ref[1]
    sw = _dot(q4b, wk.astype(BF16)) * ATTN_SCALE
    sw_new = new_score(kw_new)
    mw = jnp.maximum(jnp.max(sw, axis=-1, keepdims=True), sw_new)
    ew = jnp.exp(sw - mw)
    ew_new = jnp.exp(sw_new - mw)
    dw = jnp.sum(ew, axis=-1, keepdims=True) + ew_new
    pw = ew * (1.0 / dw)
    pw_new = ew_new * (1.0 / dw)
    o_win = _dot_nt(pw.astype(BF16), wv.astype(BF16)) \
        + pw_new.astype(BF16).astype(F32) * vw_new.astype(BF16).astype(F32)
    wl = wk.shape[1]
    lane = lax.broadcasted_iota(jnp.int32, (1, wl), 1)
    wnew_ref[0] = jnp.where(lane == wl - 1, _col_from_row(kw_new), pltpu.roll(wk, wl - 1, 1))
    wnew_ref[1] = jnp.where(lane == wl - 1, _col_from_row(vw_new), pltpu.roll(wv, wl - 1, 1))

    for t in range(N_SEL):
        copy(t).wait()

    n_past = (qpos // L_SEL)
    plane = lax.broadcasted_iota(jnp.int32, (1, page), 1)
    scores = []
    new_taken = jnp.int32(0)
    for t in range(N_SEL):
        bid = idx_ref[b, t]
        in_past = bid < n_past
        half = bid % per_page
        valid = (plane >= half * L_SEL) & (plane < (half + 1) * L_SEL) & in_past
        sc = _dot(q4b, buf_ref[t, 0].astype(BF16)) * ATTN_SCALE
        scores.append(jnp.where(valid, sc, -jnp.inf))
        new_taken = new_taken + (bid == n_past).astype(jnp.int32)
    has_new = new_taken > 0
    ss_new = jnp.where(has_new, new_score(ks_new), -jnp.inf)
    ms = ss_new
    for sc in scores:
        ms = jnp.maximum(ms, jnp.max(sc, axis=-1, keepdims=True))
    ms = jnp.where(ms == -jnp.inf, 0.0, ms)
    es_new = jnp.exp(ss_new - ms)
    ds = es_new
    es = []
    for sc in scores:
        e = jnp.exp(sc - ms)
        es.append(e)
        ds = ds + jnp.sum(e, axis=-1, keepdims=True)
    inv = 1.0 / jnp.where(ds > 0, ds, 1.0)
    o_sel = (es_new * inv).astype(BF16).astype(F32) * vs_new.astype(BF16).astype(F32)
    for t in range(N_SEL):
        o_sel = o_sel + _dot_nt((es[t] * inv).astype(BF16), buf_ref[t, 1].astype(BF16))

    g = _sigmoid(z[:, OFF_GATE:OFF_GATE + LANES])
    o_cmp = ocmp_ref[...]
    for h in range(N_HEADS):
        oh = (g[:, h:h + 1] * o_cmp[h:h + 1] + g[:, N_HEADS + h:N_HEADS + h + 1] * o_sel[h:h + 1]
              + g[:, 2 * N_HEADS + h:2 * N_HEADS + h + 1] * o_win[h:h + 1])
        o_ref[:, h * HEAD_DIM:(h + 1) * HEAD_DIM] = oh


def _nsa_sample(page_table, idx, cache_t, z3, ocmp, win_t, layer, qpos):
    s = z3.shape[0]
    page = cache_t.shape[-1]
    wl = win_t.shape[-1]
    kern = functools.partial(_nsa_sample_kernel, layer=layer, page=page, qpos=qpos)
    return pl.pallas_call(
        kern,
        grid_spec=pltpu.PrefetchScalarGridSpec(
            num_scalar_prefetch=2,
            grid=(s,),
            in_specs=[pl.BlockSpec(memory_space=pl.ANY),
                      pl.BlockSpec((None, 1, N_IN_PAD), lambda b, pt, ix: (b, 0, 0)),
                      pl.BlockSpec((None, N_HEADS, HEAD_DIM), lambda b, pt, ix: (b, 0, 0)),
                      pl.BlockSpec((None, None, 2, HEAD_DIM, wl), lambda b, pt, ix: (layer, b, 0, 0, 0))],
            out_specs=[pl.BlockSpec((None, 1, GROUP_W), lambda b, pt, ix: (b, 0, 0)),
                       pl.BlockSpec((None, 2, HEAD_DIM, wl), lambda b, pt, ix: (b, 0, 0, 0))],
            scratch_shapes=[pltpu.VMEM((N_SEL, 2, HEAD_DIM, page), F32), pltpu.SemaphoreType.DMA((1,))]),
        out_shape=[jax.ShapeDtypeStruct((s, 1, GROUP_W), F32), jax.ShapeDtypeStruct((s, 2, HEAD_DIM, wl), F32)],
        compiler_params=_cparams(("arbitrary",)),
        name="nsa_sample",
    )(page_table, idx, cache_t, z3, ocmp, win_t)


_MOBA_PAGES = 16


def _moba_gate_kernel(pt_ref, cache_ref, qcol_ref, idx_ref, buf_ref, part_ref, sem_ref,
                      *, layer, n_steps, page, nb, cur):
    b = pl.program_id(0)
    s = pl.program_id(1)
    g = _MOBA_PAGES
    step = b * n_steps + s
    total = pl.num_programs(0) * n_steps
    ppb = MOBA_BLOCK // page
    bps = g // ppb

    def copies(stp, slot):
        bb = stp // n_steps
        ss = stp % n_steps
        return [pltpu.make_async_copy(cache_ref.at[layer, pt_ref[bb, ss * g + k], 0],
                                      buf_ref.at[slot, k], sem_ref.at[slot]) for k in range(g)]

    @pl.when(step == 0)
    def _():
        for cp in copies(step, 0):
            cp.start()

    slot = step % 2

    @pl.when(step + 1 < total)
    def _():
        for cp in copies(step + 1, 1 - slot):
            cp.start()

    for cp in copies(step, slot):
        cp.wait()

    qcol = qcol_ref[...]
    for j in range(bps):
        for h in range(N_HEADS):
            acc = jnp.zeros((HEAD_DIM, page), F32)
            for k in range(ppb):
                acc = acc + buf_ref[slot, j * ppb + k, h] * qcol[h * HEAD_DIM:(h + 1) * HEAD_DIM, :]
            part_ref[pl.ds(h * nb + s * bps + j, 1), :] = jnp.sum(acc, axis=0, keepdims=True)

    @pl.when(s == n_steps - 1)
    def _():
        ones = jnp.ones((8, page), F32)
        gate = _dot_nt(ones, part_ref[...], precision=HIGHEST)[0:1] * (1.0 / MOBA_BLOCK)
        lane = lax.broadcasted_iota(jnp.int32, gate.shape, 1)
        out_lane = lax.broadcasted_iota(jnp.int32, (1, LANES), 1)
        idx = jnp.zeros((1, LANES), jnp.int32)
        for h in range(N_HEADS):
            n_of = lane - h * nb
            inh = (n_of >= 0) & (n_of < nb) & (n_of < cur)
            work = jnp.where(inh, jnp.maximum(gate, -3e38), -jnp.inf)
            for t in range(MOBA_TOPK):
                mx = jnp.max(work, axis=-1, keepdims=True)
                pick = jnp.min(jnp.where((work == mx) & inh, n_of, 4 * nb), axis=-1, keepdims=True)
                idx = jnp.where(out_lane == h * MOBA_TOPK + t, pick, idx)
                work = jnp.where(n_of == pick, -jnp.inf, work)
        idx_ref[...] = idx


def _moba_gate(page_table, cache_t, qcol, layer, cur):
    s, npg = page_table.shape
    page = cache_t.shape[-1]
    g = _MOBA_PAGES
    n_steps = npg // g
    nb = npg * page // MOBA_BLOCK
    kern = functools.partial(_moba_gate_kernel, layer=layer, n_steps=n_steps, page=page, nb=nb, cur=cur)
    return pl.pallas_call(
        kern,
        grid_spec=pltpu.PrefetchScalarGridSpec(
            num_scalar_prefetch=1,
            grid=(s, n_steps),
            in_specs=[pl.BlockSpec(memory_space=pl.ANY),
                      pl.BlockSpec((None, N_HEADS * HEAD_DIM, page), lambda b, i, pt: (b, 0, 0))],
            out_specs=pl.BlockSpec((None, 1, LANES), lambda b, i, pt: (b, 0, 0)),
            scratch_shapes=[pltpu.VMEM((2, g, N_HEADS, HEAD_DIM, page), F32),
                            pltpu.VMEM((N_HEADS * nb, page), F32),
                            pltpu.SemaphoreType.DMA((2,))]),
        out_shape=jax.ShapeDtypeStruct((s, 1, LANES), jnp.int32),
        compiler_params=_cparams(("arbitrary", "arbitrary")),
        name="moba_gate",
    )(page_table, cache_t, qcol)


def _moba_sample_kernel(pt_ref, idx_ref, cache_ref, z_ref, o_ref, buf_ref, sem_ref, *, layer, page, cur):
    b = pl.program_id(0)
    npg = pt_ref.shape[1]
    ppb = MOBA_BLOCK // page
    n_src = MOBA_TOPK * ppb

    def copy(h, t, k, kv):
        blk = jnp.minimum(idx_ref[b, h * MOBA_TOPK + t], npg // ppb - 1)
        return pltpu.make_async_copy(cache_ref.at[layer, pt_ref[b, blk * ppb + k], kv, h],
                                     buf_ref.at[kv, h, t * ppb + k], sem_ref.at[0])

    every = [(h, t, k, kv) for h in range(N_HEADS) for t in range(MOBA_TOPK) for k in range(ppb) for kv in range(2)]
    for a in every:
        copy(*a).start()
    z = z_ref[...]
    q = z[:, OFF_QMOBA:OFF_QMOBA + GROUP_W]
    k_new = z[:, OFF_KVMOBA:OFF_KVMOBA + GROUP_W]
    v_new = z[:, OFF_KVMOBA + GROUP_W:OFF_KVMOBA + 2 * GROUP_W]
    for a in every:
        copy(*a).wait()

    for h in range(N_HEADS):
        lo, hi = h * HEAD_DIM, (h + 1) * HEAD_DIM
        qh = jnp.concatenate([q[:, lo:hi]] * 8, axis=0).astype(BF16)
        qr = q[:, lo:hi].astype(BF16).astype(F32)
        s_new = jnp.sum(qr * k_new[:, lo:hi].astype(BF16).astype(F32), axis=-1, keepdims=True) * ATTN_SCALE
        scores = []
        m = s_new
        for t in range(MOBA_TOPK):
            ok = idx_ref[b, h * MOBA_TOPK + t] < cur
            for k in range(ppb):
                sc = _dot(qh, buf_ref[0, h, t * ppb + k].astype(BF16))[0:1] * ATTN_SCALE
                sc = jnp.where(ok, sc, -jnp.inf)
                scores.append(sc)
                m = jnp.maximum(m, jnp.max(sc, axis=-1, keepdims=True))
        e_new = jnp.exp(s_new - m)
        d = e_new
        es = []
        for sc in scores:
            e = jnp.exp(sc - m)
            es.append(e)
            d = d + jnp.sum(e, axis=-1, keepdims=True)
        inv = 1.0 / d
        o = (e_new * inv).astype(BF16).astype(F32) * v_new[:, lo:hi].astype(BF16).astype(F32)
        for n in range(n_src):
            pn = jnp.concatenate([es[n] * inv] * 8, axis=0).astype(BF16)
            o = o + _dot_nt(pn, buf_ref[1, h, n].astype(BF16))[0:1]
        o_ref[:, lo:hi] = o


def _moba_sample(page_table, idx, cache_t, z3, layer, cur):
    s = z3.shape[0]
    page = cache_t.shape[-1]
    ppb = MOBA_BLOCK // page
    kern = functools.partial(_moba_sample_kernel, layer=layer, page=page, cur=cur)
    return pl.pallas_call(
        kern,
        grid_spec=pltpu.PrefetchScalarGridSpec(
            num_scalar_prefetch=2,
            grid=(s,),
            in_specs=[pl.BlockSpec(memory_space=pl.ANY),
                      pl.BlockSpec((None, 1, N_IN_PAD), lambda b, pt, ix: (b, 0, 0))],
            out_specs=pl.BlockSpec((None, 1, GROUP_W), lambda b, pt, ix: (b, 0, 0)),
            scratch_shapes=[pltpu.VMEM((2, N_HEADS, MOBA_TOPK * ppb, HEAD_DIM, page), F32),
                            pltpu.SemaphoreType.DMA((1,))]),
        out_shape=jax.ShapeDtypeStruct((s, 1, GROUP_W), F32),
        compiler_params=_cparams(("arbitrary",)),
        name="moba_sample",
    )(page_table, idx, cache_t, z3)


def _cmp_weight_stack(w1):
    w = w1.reshape(2, 2, CMP_STRIDE, HEAD_DIM, HEAD_DIM)
    zero = jnp.zeros((CMP_STRIDE, HEAD_DIM, 2 * HEAD_DIM), F32)
    top = jnp.concatenate([w[0, 0], w[0, 1], zero], axis=-1)
    bot = jnp.concatenate([zero, w[1, 0], w[1, 1]], axis=-1)
    return jnp.concatenate([top, bot], axis=1).astype(BF16)


def kernel(x_prompt, x_sample, cache_moba_kv, cache_nsa_kv, state_nsa_win, state_conv, state_pool, page_table,
           w_in, conv_dw, conv_dw_b, conv_ln_g, conv_ln_b, conv_pw, nsa_cmp_pos, nsa_cmp_w1, nsa_cmp_w2,
           pool_w, pool_scale, w_out, ln1_g, ln1_b, w_up, w_down, ln2_g, ln2_b):
    bp, t, _ = x_prompt.shape
    bs = x_sample.shape[0]
    depth = w_in.shape[0]
    page = cache_nsa_kv.shape[2]
    past_len = page_table.shape[1] * page
    win_len = state_nsa_win.shape[2]
    assert x_sample.shape[1] == 1 and win_len == WINDOW and t % 512 == 0 and t >= WINDOW + 128
    assert page % L_SEL == 0 and MOBA_BLOCK % page == 0 and past_len % MOBA_BLOCK == 0
    assert past_len // MOBA_BLOCK >= MOBA_TOPK and past_len // L_SEL + 1 >= N_SEL and past_len >= POOL_MAX
    assert page_table.shape[1] % _CMP_PAGES == 0 and page_table.shape[1] % _MOBA_PAGES == 0

    nsa_t = jnp.transpose(cache_nsa_kv, (0, 1, 3, 4, 2))
    moba_t = jnp.transpose(cache_moba_kv, (0, 1, 3, 4, 5, 2))
    win_t = jnp.transpose(state_nsa_win, (0, 1, 3, 4, 2))
    conv_t = jnp.transpose(state_conv, (0, 2, 1, 3))
    pool_t = jnp.transpose(state_pool, (0, 2, 1, 3))

    perm = _in_perm()
    row = lambda v: v.reshape(1, -1)
    yp = x_prompt.reshape(bp * t, D_MODEL)
    ys = x_sample.reshape(bs, D_MODEL)
    st_p, st_s = [], []
    for l in range(depth):
        w_in_l = jnp.pad(w_in[l][:, perm], ((0, 0), (0, N_IN_PAD - N_IN))).astype(BF16)
        pw = conv_pw[l].astype(BF16)
        plw = jax.scipy.linalg.block_diag(*[pool_w[l, g] for g in range(len(POOL_WINDOWS))]).astype(BF16)
        w1 = nsa_cmp_w1[l].astype(BF16)
        w2 = nsa_cmp_w2[l].astype(BF16)
        pos = jnp.broadcast_to(nsa_cmp_pos[l].reshape(2, 1, L_CMP * HEAD_DIM), (2, 8, L_CMP * HEAD_DIM))
        w_out_l = w_out[l].astype(BF16)
        w_up_l = w_up[l].astype(BF16)
        w_down_l = w_down[l].astype(BF16)
        small = (conv_dw[l], row(conv_dw_b[l]), row(conv_ln_g[l]), row(conv_ln_b[l]), pw, plw, row(pool_scale[l]))

        z = _in_proj(yp, w_in_l)
        z3 = z.reshape(bp, t, N_IN_PAD)
        y_conv, y_pool, conv_new32 = _conv_pool_prompt(z3, *small)
        nchunk = t // CMP_STRIDE
        chunks_k = z3[:, :, OFF_KVNSA:OFF_KVNSA + HEAD_DIM].reshape(bp, nchunk, CMP_STRIDE * HEAD_DIM)
        chunks_v = z3[:, :, OFF_KVNSA + HEAD_DIM:OFF_KVNSA + 2 * HEAD_DIM].reshape(bp, nchunk, CMP_STRIDE * HEAD_DIM)
        kc, vc = _compress_prompt(chunks_k, chunks_v, pos, w1, w2)
        o_nsa = _nsa_prompt(z3, kc, vc)
        o_moba = _moba_prompt(z3)
        flat = lambda a: a.reshape(bp * t, GROUP_W)
        x1 = _out_proj((flat(y_conv), flat(o_nsa), flat(o_moba), flat(y_pool)), yp, w_out_l, row(ln1_g[l]), row(ln1_b[l]))
        yp = _ffn(x1, w_up_l, w_down_l, row(ln2_g[l]), row(ln2_b[l]))
        moba_rows = z3[:, :, OFF_KVMOBA:OFF_KVMOBA + 2 * GROUP_W].reshape(bp, t, 2, N_HEADS, HEAD_DIM)
        nsa_rows = z3[:, :, OFF_KVNSA:OFF_KVNSA + GROUP_W].reshape(bp, t, 4, HEAD_DIM)
        win_new = z3[:, t - win_len:, OFF_KVWIN:OFF_KVWIN + 2 * HEAD_DIM].reshape(bp, win_len, 2, HEAD_DIM)
        conv_new = conv_new32[:, _CONV_HALO - (CONV_W - 1):]
        pool_new = z3[:, t - (POOL_MAX - 1):, OFF_UPOOL:OFF_UPOOL + GROUP_W]
        st_p.append((moba_rows, nsa_rows, win_new, conv_new, pool_new))

        zs = _in_proj(ys, w_in_l)
        zs3 = zs.reshape(bs, 1, N_IN_PAD)
        ys_conv, ys_pool, conv_new_t, pool_new_t = _conv_pool_sample(zs, conv_t[l], pool_t[l], *small)
        c = _compress_pages(page_table, nsa_t, _cmp_weight_stack(nsa_cmp_w1[l]), l)
        o_cmp, sel_idx = _nsa_select(c, zs3, pos, w1, w2, past_len)
        os_nsa, win_new_t = _nsa_sample(page_table, sel_idx.reshape(bs, LANES), nsa_t, zs3, o_cmp, win_t, l, past_len)
        q_m = zs[:, OFF_QMOBA:OFF_QMOBA + GROUP_W]
        qcol = jnp.broadcast_to(q_m[:, :, None], (bs, GROUP_W, page))
        top_idx = _moba_gate(page_table, moba_t, qcol, l, past_len // MOBA_BLOCK)
        os_moba = _moba_sample(page_table, top_idx.reshape(bs, LANES), moba_t, zs3, l, past_len // MOBA_BLOCK)
        x1s = _out_proj((ys_conv, os_nsa.reshape(bs, GROUP_W), os_moba.reshape(bs, GROUP_W), ys_pool), ys, w_out_l,
                        row(ln1_g[l]), row(ln1_b[l]))
        ys = _ffn(x1s, w_up_l, w_down_l, row(ln2_g[l]), row(ln2_b[l]))
        st_s.append((zs[:, OFF_KVMOBA:OFF_KVMOBA + 2 * GROUP_W].reshape(bs, 1, 2, N_HEADS, HEAD_DIM),
                     zs[:, OFF_KVNSA:OFF_KVNSA + GROUP_W].reshape(bs, 1, 4, HEAD_DIM),
                     jnp.transpose(win_new_t, (0, 3, 1, 2)),
                     jnp.transpose(conv_new_t, (1, 0, 2)),
                     jnp.transpose(pool_new_t, (1, 0, 2))))

    stk = lambda sts, i: jnp.stack([s[i] for s in sts], axis=0)
    return (yp.reshape(bp, t, D_MODEL), ys.reshape(bs, 1, D_MODEL),
            stk(st_p, 0), stk(st_s, 0), stk(st_p, 1), stk(st_s, 1), stk(st_p, 2), stk(st_s, 2),
            stk(st_p, 3), stk(st_s, 3), stk(st_p, 4), stk(st_s, 4))
```

```python
import functools

import numpy as np
import jax
import jax.numpy as jnp
from jax import lax
from jax.experimental import pallas as pl
from jax.experimental.pallas import tpu as pltpu

F32 = jnp.float32
BF16 = jnp.bfloat16
HIGHEST = lax.Precision.HIGHEST

D_MODEL = 1024
HEAD_DIM = 64
GROUP_W = 256
N_HEADS = 4
CONV_W = 31
L_CMP = 32
CMP_STRIDE = 16
L_SEL = 64
N_SEL = 16
WINDOW = 512
MOBA_BLOCK = 256
MOBA_TOPK = 3
POOL_WINDOWS = (2, 4, 8, 16)
POOL_GW = 64
POOL_MAX = 16
D_FF = 4096
DEPTH = 2
ALPHA = (2 * DEPTH) ** 0.25
LN_EPS = 1e-5
ATTN_SCALE = HEAD_DIM ** -0.5
NEG = -1e30

OFF_UCONV, OFF_KVMOBA, OFF_QNSA, OFF_QMOBA = 0, 512, 1024, 1280
OFF_UPOOL, OFF_KVNSA, OFF_KVWIN, OFF_GATE = 1536, 1792, 2048, 2176
N_IN_PAD = 2304
N_IN = 2188
LANES = 128
VMEM_LIMIT = 56 * 1024 * 1024


def _in_perm():
    return np.concatenate([
        np.arange(0, 512), np.arange(1420, 1932), np.arange(512, 768), np.arange(1164, 1420),
        np.arange(1932, 2188), np.arange(768, 1024), np.arange(1024, 1152), np.arange(1152, 1164)])


def _ln(x, g, b):
    mu = jnp.mean(x, axis=-1, keepdims=True)
    xc = x - mu
    var = jnp.mean(xc * xc, axis=-1, keepdims=True)
    return xc * lax.rsqrt(var + LN_EPS) * g + b


def _sigmoid(x):
    return 1.0 / (1.0 + jnp.exp(-x))


def _silu(x):
    return x * _sigmoid(x)


def _dot(a, b, **kw):
    return jnp.dot(a, b, preferred_element_type=F32, **kw)


def _dot_nt(a, b, **kw):
    return lax.dot_general(a, b, (((1,), (1,)), ((), ())), preferred_element_type=F32, **kw)


def _masked_softmax(s, mask):
    r = s.shape[0]
    s = jnp.where(mask, s, -jnp.inf)
    m = jnp.max(_lane_partial(s, jnp.maximum), axis=-1, keepdims=True)
    m = jnp.broadcast_to(jnp.where(m == -jnp.inf, 0.0, m), (r, LANES))
    e = _exp_shifted(s, m)
    d = jnp.sum(_lane_partial(e, jnp.add), axis=-1, keepdims=True)
    inv = jnp.broadcast_to(1.0 / jnp.where(d > 0, d, 1.0), (r, LANES))
    return jnp.concatenate([e[:, g * LANES:(g + 1) * LANES] * inv for g in range(s.shape[1] // LANES)], axis=-1)


def _lane_partial(s, op):
    out = s[:, 0:LANES]
    for g in range(1, s.shape[1] // LANES):
        out = op(out, s[:, g * LANES:(g + 1) * LANES])
    return out


def _exp_shifted(s, m):
    return jnp.concatenate([jnp.exp(s[:, g * LANES:(g + 1) * LANES] - m) for g in range(s.shape[1] // LANES)], axis=-1)


def _cparams(sem, vmem=None):
    return pltpu.CompilerParams(dimension_semantics=sem, vmem_limit_bytes=vmem or VMEM_LIMIT)


def _in_proj_kernel(x_ref, w_ref, o_ref):
    o_ref[...] = _dot(x_ref[...].astype(BF16), w_ref[...])


def _in_proj(x, w):
    m, k = x.shape
    n = w.shape[1]
    tm = min(m, 512)
    return pl.pallas_call(
        _in_proj_kernel,
        grid=(m // tm,),
        in_specs=[pl.BlockSpec((tm, k), lambda i: (i, 0)), pl.BlockSpec((k, n), lambda i: (0, 0))],
        out_specs=pl.BlockSpec((tm, n), lambda i: (i, 0)),
        out_shape=jax.ShapeDtypeStruct((m, n), F32),
        compiler_params=_cparams(("parallel",)),
        name="in_proj",
    )(x, w)


def _out_proj_kernel(a_ref, b_ref, c_ref, d_ref, x_ref, w_ref, g_ref, bt_ref, o_ref):
    mix = jnp.concatenate([a_ref[...], b_ref[...], c_ref[...], d_ref[...]], axis=-1).astype(BF16)
    y = _dot(mix, w_ref[...])
    o_ref[...] = _ln(ALPHA * x_ref[...] + y, g_ref[...], bt_ref[...])


def _out_proj(pieces, x, w, g, b):
    m = x.shape[0]
    tm = min(m, 512)
    pspec = pl.BlockSpec((tm, GROUP_W), lambda i: (i, 0))
    return pl.pallas_call(
        _out_proj_kernel,
        grid=(m // tm,),
        in_specs=[pspec, pspec, pspec, pspec,
                  pl.BlockSpec((tm, D_MODEL), lambda i: (i, 0)),
                  pl.BlockSpec((D_MODEL, D_MODEL), lambda i: (0, 0)),
                  pl.BlockSpec((1, D_MODEL), lambda i: (0, 0)),
                  pl.BlockSpec((1, D_MODEL), lambda i: (0, 0))],
        out_specs=pl.BlockSpec((tm, D_MODEL), lambda i: (i, 0)),
        out_shape=jax.ShapeDtypeStruct((m, D_MODEL), F32),
        compiler_params=_cparams(("parallel",)),
        name="out_proj_ln",
    )(*pieces, x, w, g, b)


def _ffn_kernel(x_ref, wu_ref, wd_ref, g_ref, b_ref, o_ref, acc_ref):
    j = pl.program_id(1)

    @pl.when(j == 0)
    def _():
        acc_ref[...] = jnp.zeros_like(acc_ref)

    h = jnp.maximum(_dot(x_ref[...].astype(BF16), wu_ref[...]), 0.0)
    acc_ref[...] += _dot((h * h).astype(BF16), wd_ref[...])

    @pl.when(j == pl.num_programs(1) - 1)
    def _():
        o_ref[...] = _ln(ALPHA * x_ref[...] + acc_ref[...], g_ref[...], b_ref[...])


def _ffn(x, wu, wd, g, b):
    m = x.shape[0]
    tm = min(m, 1024)
    tf = 1024
    return pl.pallas_call(
        _ffn_kernel,
        grid=(m // tm, D_FF // tf),
        in_specs=[pl.BlockSpec((tm, D_MODEL), lambda i, j: (i, 0)),
                  pl.BlockSpec((D_MODEL, tf), lambda i, j: (0, j)),
                  pl.BlockSpec((tf, D_MODEL), lambda i, j: (j, 0)),
                  pl.BlockSpec((1, D_MODEL), lambda i, j: (0, 0)),
                  pl.BlockSpec((1, D_MODEL), lambda i, j: (0, 0))],
        out_specs=pl.BlockSpec((tm, D_MODEL), lambda i, j: (i, 0)),
        out_shape=jax.ShapeDtypeStruct((m, D_MODEL), F32),
        scratch_shapes=[pltpu.VMEM((tm, D_MODEL), F32)],
        compiler_params=_cparams(("parallel", "arbitrary")),
        name="ffn_ln",
    )(x, wu, wd, g, b)


_CONV_HALO = 32
_POOL_HALO = 16


def _pool_groups(rows_ref, buf_ref, tt, cnt_fn):
    n = _POOL_HALO + tt
    lane = lax.broadcasted_iota(jnp.int32, (1, GROUP_W), 1)
    cur = rows_ref[pl.ds(_POOL_HALO, tt), :]
    out = jnp.zeros((tt, GROUP_W), F32)
    shift = 1
    for g, w in enumerate(POOL_WINDOWS):
        src = rows_ref if g == 0 else buf_ref.at[(g - 1) % 2]
        dst = buf_ref.at[g % 2]
        lo = 2 * shift - 1
        dst[pl.ds(lo, n - lo), :] = src[pl.ds(lo, n - lo), :] + src[pl.ds(lo - shift, n - lo), :]
        shift *= 2
        s = dst[pl.ds(_POOL_HALO, tt), :]
        mean = s / cnt_fn(w)
        out = jnp.where((lane >= g * POOL_GW) & (lane < (g + 1) * POOL_GW), mean, out)
    return out - cur


def _conv_pool_prompt_kernel(u_ref, uh_ref, p_ref, ph_ref, dw_ref, dwb_ref, lg_ref, lb_ref, pw_ref,
                             plw_ref, pls_ref, yc_ref, yp_ref, cnew_ref, zc_ref, rows_ref, buf_ref, *, tt):
    i = pl.program_id(1)
    first = i == 0
    u = u_ref[...]
    zg = u[:, :GROUP_W] * _sigmoid(u[:, GROUP_W:])
    uh = uh_ref[...]
    zh = uh[:, :GROUP_W] * _sigmoid(uh[:, GROUP_W:])
    zc_ref[pl.ds(0, _CONV_HALO), :] = jnp.where(first, 0.0, zh)
    zc_ref[pl.ds(_CONV_HALO, tt), :] = zg
    y = jnp.zeros((tt, GROUP_W), F32) + dwb_ref[...]
    for k in range(CONV_W):
        y = y + dw_ref[k:k + 1, :] * zc_ref[pl.ds(_CONV_HALO - (CONV_W - 1) + k, tt), :]
    y = _silu(_ln(y, lg_ref[...], lb_ref[...]))
    yc_ref[...] = _dot(y.astype(BF16), pw_ref[...])

    @pl.when(i == pl.num_programs(1) - 1)
    def _():
        cnew_ref[...] = zc_ref[pl.ds(tt, _CONV_HALO), :]

    rows_ref[pl.ds(0, _POOL_HALO), :] = jnp.where(first, 0.0, ph_ref[...])
    rows_ref[pl.ds(_POOL_HALO, tt), :] = p_ref[...]
    qpos1 = i * tt + lax.broadcasted_iota(jnp.int32, (tt, 1), 0) + 1
    d = _pool_groups(rows_ref, buf_ref, tt, lambda w: jnp.minimum(w, qpos1).astype(F32))
    yp_ref[...] = _dot(d.astype(BF16), plw_ref[...]) * pls_ref[...]


def _conv_pool_prompt(z3, dw, dwb, lg, lb, pw, plw, pls):
    b, t, _ = z3.shape
    tt = 512
    nt = t // tt
    kern = functools.partial(_conv_pool_prompt_kernel, tt=tt)
    cst = lambda shape: pl.BlockSpec(shape, lambda bi, i: (0,) * len(shape))
    return pl.pallas_call(
        kern,
        grid=(b, nt),
        in_specs=[
            pl.BlockSpec((None, tt, 512), lambda bi, i: (bi, i, OFF_UCONV // 512)),
            pl.BlockSpec((None, _CONV_HALO, 512),
                         lambda bi, i: (bi, jnp.maximum(i * (tt // _CONV_HALO) - 1, 0), OFF_UCONV // 512)),
            pl.BlockSpec((None, tt, GROUP_W), lambda bi, i: (bi, i, OFF_UPOOL // GROUP_W)),
            pl.BlockSpec((None, _POOL_HALO, GROUP_W),
                         lambda bi, i: (bi, jnp.maximum(i * (tt // _POOL_HALO) - 1, 0), OFF_UPOOL // GROUP_W)),
            cst((CONV_W, GROUP_W)), cst((1, GROUP_W)), cst((1, GROUP_W)), cst((1, GROUP_W)),
            cst((GROUP_W, GROUP_W)), cst((GROUP_W, GROUP_W)), cst((1, GROUP_W)),
        ],
        out_specs=[
            pl.BlockSpec((None, tt, GROUP_W), lambda bi, i: (bi, i, 0)),
            pl.BlockSpec((None, tt, GROUP_W), lambda bi, i: (bi, i, 0)),
            pl.BlockSpec((None, _CONV_HALO, GROUP_W), lambda bi, i: (bi, 0, 0)),
        ],
        out_shape=[jax.ShapeDtypeStruct((b, t, GROUP_W), F32), jax.ShapeDtypeStruct((b, t, GROUP_W), F32),
                   jax.ShapeDtypeStruct((b, _CONV_HALO, GROUP_W), F32)],
        scratch_shapes=[pltpu.VMEM((_CONV_HALO + tt, GROUP_W), F32),
                        pltpu.VMEM((_POOL_HALO + tt, GROUP_W), F32),
                        pltpu.VMEM((2, _POOL_HALO + tt, GROUP_W), F32)],
        compiler_params=_cparams(("parallel", "arbitrary")),
        name="conv_pool_prompt",
    )(z3, z3, z3, z3, dw, dwb, lg, lb, pw, plw, pls)


def _compress_tail(c, bias_k, bias_v, w2k, w2v):
    n = c.shape[0]

    def one(c0, c1, bias, w2):
        hid = c0 + pltpu.roll(c1, n - 1, 0) + bias
        return _dot(_silu(hid).astype(BF16), w2)

    kc = one(c[:, 0:64], c[:, 64:128], bias_k, w2k)
    vc = one(c[:, 128:192], c[:, 192:256], bias_v, w2v)
    return kc, vc


def _cmp_bias(pos_ref, w1_ref):
    bk = _dot(pos_ref[0].astype(BF16), w1_ref[0])[0:1]
    bv = _dot(pos_ref[1].astype(BF16), w1_ref[1])[0:1]
    return bk, bv


def _compress_prompt_kernel(ck_ref, cv_ref, pos_ref, w1_ref, w2_ref, kc_ref, vc_ref):
    bk, bv = _cmp_bias(pos_ref, w1_ref)
    half = CMP_STRIDE * HEAD_DIM
    ck = ck_ref[...].astype(BF16)
    cv = cv_ref[...].astype(BF16)
    c = jnp.concatenate([_dot(ck, w1_ref[0, 0:half, :]), _dot(ck, w1_ref[0, half:2 * half, :]),
                         _dot(cv, w1_ref[1, 0:half, :]), _dot(cv, w1_ref[1, half:2 * half, :])], axis=-1)
    kc, vc = _compress_tail(c, bk, bv, w2_ref[0], w2_ref[1])
    kc_ref[...] = kc
    vc_ref[...] = vc


def _compress_prompt(chunks_k, chunks_v, pos, w1, w2):
    b, nc, kk = chunks_k.shape
    cst = lambda shape: pl.BlockSpec(shape, lambda bi: (0,) * len(shape))
    return pl.pallas_call(
        _compress_prompt_kernel,
        grid=(b,),
        in_specs=[pl.BlockSpec((None, nc, kk), lambda bi: (bi, 0, 0)),
                  pl.BlockSpec((None, nc, kk), lambda bi: (bi, 0, 0)),
                  cst(pos.shape), cst(w1.shape), cst(w2.shape)],
        out_specs=[pl.BlockSpec((None, nc, HEAD_DIM), lambda bi: (bi, 0, 0)),
                   pl.BlockSpec((None, nc, HEAD_DIM), lambda bi: (bi, 0, 0))],
        out_shape=[jax.ShapeDtypeStruct((b, nc, HEAD_DIM), F32)] * 2,
        compiler_params=_cparams(("parallel",)),
        name="compress_prompt",
    )(chunks_k, chunks_v, pos, w1, w2)


def _rank_desc(v, n):
    lane = lax.broadcasted_iota(jnp.int32, v.shape, 1)
    rank = jnp.zeros(v.shape, F32)
    for j in range(n):
        col = v[:, j:j + 1]
        rank = rank + jnp.where((col > v) | ((col == v) & (lane > j)), 1.0, 0.0)
    return rank


def _cover_matrix(n_cmp_pad, n_cmp, n_sel_pad):
    j = np.arange(n_cmp_pad)[:, None]
    i = np.arange(n_sel_pad)[None, :]
    cov = (j * CMP_STRIDE < (i + 1) * L_SEL) & (j * CMP_STRIDE + L_CMP > i * L_SEL) & (j < n_cmp)
    return jnp.asarray(cov.astype(np.float32))


def _nsa_prompt_kernel(q_ref, g_ref, kv_ref, win_ref, kc_ref, vc_ref, covert_ref, o_ref,
                       kt_ref, v_ref, kwt_ref, vw_ref, qa_ref, so_ref, sp_ref, m_ref, l_ref, acc_ref, ow_ref,
                       *, c, tk, n_cmp, ns):
    i = pl.program_id(1)
    qs = i * c
    t = kv_ref.shape[0]
    nsp = LANES - HEAD_DIM

    @pl.when(i == 0)
    def _():
        rows = lax.broadcasted_iota(jnp.int32, (nsp, t), 0)
        cols = lax.broadcasted_iota(jnp.int32, (nsp, t), 1)
        kt_ref[HEAD_DIM:LANES, :] = jnp.where(rows == cols // L_SEL, 1.0, 0.0).astype(BF16)
        rt = 256
        for n in range(t // rt):
            x = kv_ref[n * rt:(n + 1) * rt, :]
            kt_ref[0:HEAD_DIM, n * rt:(n + 1) * rt] = x.T[2 * HEAD_DIM:3 * HEAD_DIM].astype(BF16)
            v_ref[n * rt:(n + 1) * rt, :] = x[:, 3 * HEAD_DIM:4 * HEAD_DIM].astype(BF16)
            w = win_ref[n * rt:(n + 1) * rt, :]
            kwt_ref[:, n * rt:(n + 1) * rt] = w.T[0:HEAD_DIM].astype(BF16)
            vw_ref[n * rt:(n + 1) * rt, :] = w[:, HEAD_DIM:2 * HEAD_DIM].astype(BF16)

    q = q_ref[...] * ATTN_SCALE
    q4 = jnp.concatenate([q[:, h * HEAD_DIM:(h + 1) * HEAD_DIM] for h in range(N_HEADS)], axis=0).astype(BF16)
    qpos = qs + lax.broadcasted_iota(jnp.int32, (c, 1), 0)
    qpos4 = jnp.concatenate([qpos] * N_HEADS, axis=0)

    wl = WINDOW + c
    ks = pl.multiple_of(jnp.maximum(qs - WINDOW, 0), c)
    rel = qpos4 - (ks + lax.broadcasted_iota(jnp.int32, (1, wl), 1))
    sw = jnp.where((rel >= 0) & (rel <= WINDOW), _dot(q4, kwt_ref[:, pl.ds(ks, wl)]), NEG)
    mw = jnp.broadcast_to(jnp.max(_lane_partial(sw, jnp.maximum), axis=-1, keepdims=True), (N_HEADS * c, LANES))
    ew = _exp_shifted(sw, mw)
    lw = jnp.sum(_lane_partial(ew, jnp.add), axis=-1, keepdims=True)
    ow_ref[...] = _dot(ew.astype(BF16), vw_ref[pl.ds(ks, wl), :]) * (1.0 / lw)

    nc = kc_ref.shape[0]
    s = _dot_nt(q4, kc_ref[...].astype(BF16))
    jj = lax.broadcasted_iota(jnp.int32, (1, nc), 1)
    p = _masked_softmax(s, (jj * CMP_STRIDE + (L_CMP - 1) <= qpos4) & (jj < n_cmp))
    o_cmp = _dot(p.astype(BF16), vc_ref[...].astype(BF16))
    psum = p[0:c] + p[c:2 * c] + p[2 * c:3 * c] + p[3 * c:4 * c]

    imp = _dot_nt(covert_ref[...], psum, precision=HIGHEST)
    blk = lax.broadcasted_iota(jnp.int32, (nsp, 1), 0)
    qpos_l = qs + lax.broadcasted_iota(jnp.int32, (1, c), 1)
    cur = qpos_l // L_SEL
    imp = jnp.where((blk == 0) | (blk == cur) | (blk == cur - 1), jnp.inf, imp)
    imp = jnp.where(blk * L_SEL > qpos_l, -jnp.inf, imp)
    rank = jnp.zeros((nsp, c), F32)
    for b in range(ns):
        rb = imp[b:b + 1, :]
        rank = rank + jnp.where((rb > imp) | ((rb == imp) & (blk > b)), 1.0, 0.0)
    sel = (rank < N_SEL) & (blk * L_SEL <= qpos_l)
    bias_t = jnp.where(sel, 0.0, NEG)
    bias = jnp.concatenate([bias_t, jnp.zeros((LANES - nsp, c), F32)], axis=0).T[:, 0:nsp]
    qa = jnp.concatenate([q4, jnp.concatenate([bias] * N_HEADS, axis=0).astype(BF16)], axis=-1)

    jt = qs // tk
    k_own = pl.multiple_of(jt * tk, tk)
    qa_ref[...] = qa
    kpos = k_own + lax.broadcasted_iota(jnp.int32, (1, tk), 1)
    sc = jnp.where(kpos <= qpos4, _dot(qa, kt_ref[:, pl.ds(k_own, tk)]), NEG)
    so_ref[...] = sc
    m_ref[...] = _lane_partial(sc, jnp.maximum)

    def max_body(j, carry):
        k0 = pl.multiple_of(j * tk, tk)
        sc = _dot(qa_ref[...], kt_ref[:, pl.ds(k0, tk)])
        sp_ref[j] = sc
        m_ref[...] = jnp.maximum(m_ref[...], _lane_partial(sc, jnp.maximum))
        return carry

    lax.fori_loop(0, jt, max_body, 0)
    m = jnp.broadcast_to(jnp.max(m_ref[...], axis=-1, keepdims=True), m_ref.shape)
    m_ref[...] = m
    pe = _exp_shifted(so_ref[...], m)
    l_ref[...] = _lane_partial(pe, jnp.add)
    acc_ref[...] = _dot(pe.astype(BF16), v_ref[pl.ds(k_own, tk), :])

    def sum_body(j, carry):
        k0 = pl.multiple_of(j * tk, tk)
        pe = _exp_shifted(sp_ref[j], m_ref[...])
        l_ref[...] += _lane_partial(pe, jnp.add)
        acc_ref[...] += _dot(pe.astype(BF16), v_ref[pl.ds(k0, tk), :])
        return carry

    lax.fori_loop(0, jt, sum_body, 0)
    o_sel = acc_ref[...] * (1.0 / jnp.sum(l_ref[...], axis=-1, keepdims=True))

    o_win = ow_ref[...]
    g = _sigmoid(g_ref[...])

    def gate(br):
        return jnp.concatenate([g[:, br * N_HEADS + h:br * N_HEADS + h + 1] for h in range(N_HEADS)], axis=0)

    o = gate(0) * o_cmp + gate(1) * o_sel + gate(2) * o_win
    for h in range(N_HEADS):
        o_ref[:, h * HEAD_DIM:(h + 1) * HEAD_DIM] = o[h * c:(h + 1) * c]


def _nsa_prompt(z3, kc, vc):
    b, t, _ = z3.shape
    c = 128
    tk = 512
    nc = kc.shape[1]
    n_cmp = t // CMP_STRIDE - 1
    ns = t // L_SEL
    nsp = LANES - HEAD_DIM
    assert ns <= nsp and c == LANES
    covert = _cover_matrix(nc, n_cmp, nsp).T
    kern = functools.partial(_nsa_prompt_kernel, c=c, tk=tk, n_cmp=n_cmp, ns=ns)
    return pl.pallas_call(
        kern,
        grid=(b, t // c),
        in_specs=[
            pl.BlockSpec((None, c, GROUP_W), lambda bi, i: (bi, i, OFF_QNSA // GROUP_W)),
            pl.BlockSpec((None, c, LANES), lambda bi, i: (bi, i, OFF_GATE // LANES)),
            pl.BlockSpec((None, t, GROUP_W), lambda bi, i: (bi, 0, OFF_KVNSA // GROUP_W)),
            pl.BlockSpec((None, t, LANES), lambda bi, i: (bi, 0, OFF_KVWIN // LANES)),
            pl.BlockSpec((None, nc, HEAD_DIM), lambda bi, i: (bi, 0, 0)),
            pl.BlockSpec((None, nc, HEAD_DIM), lambda bi, i: (bi, 0, 0)),
            pl.BlockSpec((nsp, nc), lambda bi, i: (0, 0)),
        ],
        out_specs=pl.BlockSpec((None, c, GROUP_W), lambda bi, i: (bi, i, 0)),
        out_shape=jax.ShapeDtypeStruct((b, t, GROUP_W), F32),
        scratch_shapes=[pltpu.VMEM((LANES, t), BF16), pltpu.VMEM((t, HEAD_DIM), BF16),
                        pltpu.VMEM((HEAD_DIM, t), BF16), pltpu.VMEM((t, HEAD_DIM), BF16),
                        pltpu.VMEM((N_HEADS * c, LANES), BF16), pltpu.VMEM((N_HEADS * c, tk), F32),
                        pltpu.VMEM((t // tk - 1, N_HEADS * c, tk), F32),
                        pltpu.VMEM((N_HEADS * c, LANES), F32), pltpu.VMEM((N_HEADS * c, LANES), F32),
                        pltpu.VMEM((N_HEADS * c, HEAD_DIM), F32), pltpu.VMEM((N_HEADS * c, HEAD_DIM), F32)],
        compiler_params=_cparams(("parallel", "arbitrary")),
        name="nsa_prompt",
    )(z3, z3, z3, z3, kc, vc, covert)


def _moba_prompt_kernel(q_ref, kv_ref, o_ref, kmean_ref, kt_ref, v_ref, qa_ref, so_ref, sp_ref, m_ref, l_ref, acc_ref,
                        *, nb):
    i = pl.program_id(1)
    c = MOBA_BLOCK
    t = kv_ref.shape[0]

    @pl.when(i == 0)
    def _():
        rows = lax.broadcasted_iota(jnp.int32, (LANES - HEAD_DIM, t), 0)
        cols = lax.broadcasted_iota(jnp.int32, (LANES - HEAD_DIM, t), 1)
        onehot = jnp.where(rows == cols // c, 1.0, 0.0).astype(BF16)
        for h in range(N_HEADS):
            kt_ref[h, HEAD_DIM:LANES, :] = onehot
        kmean_ref[...] = jnp.zeros(kmean_ref.shape, F32)
        for n in range(nb):
            x = kv_ref[n * c:(n + 1) * c, :]
            kmean_ref[n:n + 1, :] = jnp.mean(x[:, 0:GROUP_W], axis=0, keepdims=True)
            xt = x[:, 0:GROUP_W].T
            for h in range(N_HEADS):
                lo, hi = h * HEAD_DIM, (h + 1) * HEAD_DIM
                kt_ref[h, 0:HEAD_DIM, n * c:(n + 1) * c] = xt[lo:hi].astype(BF16)
                v_ref[h, n * c:(n + 1) * c, :] = x[:, GROUP_W + lo:GROUP_W + hi].astype(BF16)

    q = q_ref[...]
    nbp = LANES - HEAD_DIM
    nbr = -(-nb // 8) * 8
    blk = lax.broadcasted_iota(jnp.int32, (nbr, 1), 0)
    fblk = blk.astype(F32)
    past = blk < i
    row = lax.broadcasted_iota(jnp.int32, (c, c), 0)
    col = lax.broadcasted_iota(jnp.int32, (c, c), 1)
    tri = col <= row
    own0 = pl.multiple_of(i * c, c)
    for h in range(N_HEADS):
        lo, hi = h * HEAD_DIM, (h + 1) * HEAD_DIM
        qh = q[:, lo:hi]
        work = jnp.where(past, _dot_nt(kmean_ref[0:nbr, lo:hi], qh, precision=HIGHEST), -jnp.inf)
        sel = jnp.zeros((nbr, c), jnp.bool_)
        for _ in range(MOBA_TOPK):
            mx = jnp.max(work, axis=0, keepdims=True)
            pick = jnp.min(jnp.where(work == mx, fblk, float(nbp)), axis=0, keepdims=True)
            hit = fblk == pick
            sel = sel | (hit & (mx > -jnp.inf))
            work = jnp.where(hit, -jnp.inf, work)
        bias_t = jnp.where(sel, 0.0, NEG)
        bias = jnp.concatenate([bias_t, jnp.full((LANES - nbr, c), NEG, F32)], axis=0).T[:, 0:nbp]
        qs = (qh * ATTN_SCALE).astype(BF16)
        qa_ref[h] = jnp.concatenate([qs, bias.astype(BF16)], axis=-1)
        sc = jnp.where(tri, _dot(qs, kt_ref[h, 0:HEAD_DIM, pl.ds(own0, c)]), NEG)
        so_ref[h] = sc
        m_ref[h] = _lane_partial(sc, jnp.maximum)

    def max_body(j, carry):
        k0 = pl.multiple_of(j * c, c)
        for h in range(N_HEADS):
            sc = _dot(qa_ref[h], kt_ref[h, :, pl.ds(k0, c)])
            sp_ref[h, j] = sc
            m_ref[h] = jnp.maximum(m_ref[h], _lane_partial(sc, jnp.maximum))
        return carry

    lax.fori_loop(0, i, max_body, 0)
    for h in range(N_HEADS):
        m = jnp.broadcast_to(jnp.max(m_ref[h], axis=-1, keepdims=True), (c, LANES))
        m_ref[h] = m
        pe = _exp_shifted(so_ref[h], m)
        l_ref[h] = _lane_partial(pe, jnp.add)
        acc_ref[h] = _dot(pe.astype(BF16), v_ref[h, pl.ds(own0, c), :])

    def sum_body(j, carry):
        k0 = pl.multiple_of(j * c, c)
        for h in range(N_HEADS):
            pe = _exp_shifted(sp_ref[h, j], m_ref[h])
            l_ref[h] += _lane_partial(pe, jnp.add)
            acc_ref[h] += _dot(pe.astype(BF16), v_ref[h, pl.ds(k0, c), :])
        return carry

    lax.fori_loop(0, i, sum_body, 0)
    for h in range(N_HEADS):
        l = jnp.sum(l_ref[h], axis=-1, keepdims=True)
        o_ref[:, h * HEAD_DIM:(h + 1) * HEAD_DIM] = acc_ref[h] * (1.0 / l)


def _moba_prompt(z3):
    b, t, _ = z3.shape
    nb = t // MOBA_BLOCK
    assert nb <= LANES - HEAD_DIM
    kern = functools.partial(_moba_prompt_kernel, nb=nb)
    return pl.pallas_call(
        kern,
        grid=(b, nb),
        in_specs=[pl.BlockSpec((None, MOBA_BLOCK, GROUP_W), lambda bi, i: (bi, i, OFF_QMOBA // GROUP_W)),
                  pl.BlockSpec((None, t, 2 * GROUP_W), lambda bi, i: (bi, 0, OFF_KVMOBA // (2 * GROUP_W)))],
        out_specs=pl.BlockSpec((None, MOBA_BLOCK, GROUP_W), lambda bi, i: (bi, i, 0)),
        out_shape=jax.ShapeDtypeStruct((b, t, GROUP_W), F32),
        scratch_shapes=[pltpu.VMEM((LANES - HEAD_DIM, GROUP_W), F32),
                        pltpu.VMEM((N_HEADS, LANES, t), BF16),
                        pltpu.VMEM((N_HEADS, t, HEAD_DIM), BF16),
                        pltpu.VMEM((N_HEADS, MOBA_BLOCK, LANES), BF16),
                        pltpu.VMEM((N_HEADS, MOBA_BLOCK, MOBA_BLOCK), F32),
                        pltpu.VMEM((N_HEADS, nb - 1, MOBA_BLOCK, MOBA_BLOCK), F32),
                        pltpu.VMEM((N_HEADS, MOBA_BLOCK, LANES), F32),
                        pltpu.VMEM((N_HEADS, MOBA_BLOCK, LANES), F32),
                        pltpu.VMEM((N_HEADS, MOBA_BLOCK, HEAD_DIM), F32)],
        compiler_params=_cparams(("parallel", "arbitrary")),
        name="moba_prompt",
    )(z3, z3)


def _conv_pool_sample_kernel(z_ref, cst_ref, pst_ref, dw_ref, dwb_ref, lg_ref, lb_ref, pw_ref, plw_ref, pls_ref,
                             yc_ref, yp_ref, cnew_ref, pnew_ref):
    u = z_ref[:, OFF_UCONV:OFF_UCONV + 2 * GROUP_W]
    zg = u[:, :GROUP_W] * _sigmoid(u[:, GROUP_W:])
    nst = CONV_W - 1
    y = dwb_ref[...] + dw_ref[nst:nst + 1, :] * zg
    for k in range(nst):
        y = y + dw_ref[k:k + 1, :] * cst_ref[k]
    y = _silu(_ln(y, lg_ref[...], lb_ref[...]))
    yc_ref[...] = _dot(y.astype(BF16), pw_ref[...])
    for k in range(nst - 1):
        cnew_ref[k] = cst_ref[k + 1]
    cnew_ref[nst - 1] = zg

    p = z_ref[:, OFF_UPOOL:OFF_UPOOL + GROUP_W]
    npst = POOL_MAX - 1
    lane = lax.broadcasted_iota(jnp.int32, (1, GROUP_W), 1)
    run = p
    mean = jnp.zeros_like(p)
    k = 1
    for g, w in enumerate(POOL_WINDOWS):
        while k < w:
            run = run + pst_ref[npst - k]
            k += 1
        mean = jnp.where((lane >= g * POOL_GW) & (lane < (g + 1) * POOL_GW), run / float(w), mean)
    d = mean - p
    yp_ref[...] = _dot(d.astype(BF16), plw_ref[...]) * pls_ref[...]
    for k in range(npst - 1):
        pnew_ref[k] = pst_ref[k + 1]
    pnew_ref[npst - 1] = p


def _conv_pool_sample(z, cst, pst, dw, dwb, lg, lb, pw, plw, pls):
    s = z.shape[0]
    return pl.pallas_call(
        _conv_pool_sample_kernel,
        out_shape=[jax.ShapeDtypeStruct((s, GROUP_W), F32), jax.ShapeDtypeStruct((s, GROUP_W), F32),
                   jax.ShapeDtypeStruct(cst.shape, F32), jax.ShapeDtypeStruct(pst.shape, F32)],
        compiler_params=pltpu.CompilerParams(vmem_limit_bytes=VMEM_LIMIT),
        name="conv_pool_sample",
    )(z, cst, pst, dw, dwb, lg, lb, pw, plw, pls)


_CMP_PAGES = 16
_CMP_GROUPS = 1


def _compress_pages_kernel(pt_ref, cache_ref, w_ref, c_ref, buf_ref, rows_ref, sem_ref, *, layer, n_steps, page):
    b = pl.program_id(0)
    s = pl.program_id(1)
    g = _CMP_PAGES
    step = b * n_steps + s
    total = pl.num_programs(0) * n_steps

    def copies(stp, slot):
        bb = stp // n_steps
        ss = stp % n_steps
        return [pltpu.make_async_copy(cache_ref.at[layer, pt_ref[bb, ss * g + k], pl.ds(0, 2)],
                                      buf_ref.at[slot, k], sem_ref.at[slot]) for k in range(g)]

    @pl.when(step == 0)
    def _():
        for cp in copies(step, 0):
            cp.start()

    slot = step % 2

    @pl.when(step + 1 < total)
    def _():
        for cp in copies(step + 1, 1 - slot):
            cp.start()

    for cp in copies(step, slot):
        cp.wait()

    gp = g // _CMP_GROUPS
    nchunk = gp * page // CMP_STRIDE
    for grp in range(_CMP_GROUPS):
        rows = rows_ref.at[grp]
        for k in range(gp):
            x = buf_ref[slot, grp * gp + k].reshape(2 * HEAD_DIM, page)
            rows[pl.ds(k * page, page), :] = x.T
        acc = jnp.zeros((nchunk, 4 * HEAD_DIM), F32)
        for r in range(0, CMP_STRIDE, 2):
            xr = jnp.concatenate([rows[pl.ds(r, nchunk, stride=CMP_STRIDE), :],
                                  rows[pl.ds(r + 1, nchunk, stride=CMP_STRIDE), :]], axis=-1).astype(BF16)
            acc = acc + _dot(xr, w_ref[r // 2])
        c_ref[pl.ds(grp * nchunk, nchunk), :] = acc


def _compress_pages(page_table, cache_t, wstack, layer):
    s, npg = page_table.shape
    page = cache_t.shape[-1]
    g = _CMP_PAGES
    n_steps = npg // g
    nchunk = g * page // CMP_STRIDE
    kern = functools.partial(_compress_pages_kernel, layer=layer, n_steps=n_steps, page=page)
    return pl.pallas_call(
        kern,
        grid_spec=pltpu.PrefetchScalarGridSpec(
            num_scalar_prefetch=1,
            grid=(s, n_steps),
            in_specs=[pl.BlockSpec(memory_space=pl.ANY),
                      pl.BlockSpec(wstack.shape, lambda b, i, pt: (0, 0, 0))],
            out_specs=pl.BlockSpec((None, nchunk, 4 * HEAD_DIM), lambda b, i, pt: (b, i, 0)),
            scratch_shapes=[pltpu.VMEM((2, g, 2, HEAD_DIM, page), F32),
                            pltpu.VMEM((_CMP_GROUPS, g // _CMP_GROUPS * page, 2 * HEAD_DIM), F32),
                            pltpu.SemaphoreType.DMA((2,))]),
        out_shape=jax.ShapeDtypeStruct((s, n_steps * nchunk, 4 * HEAD_DIM), F32),
        compiler_params=_cparams(("arbitrary", "arbitrary")),
        name="compress_pages",
    )(page_table, cache_t, wstack)


def _stack_heads(q):
    rows = [q[:, h * HEAD_DIM:(h + 1) * HEAD_DIM] for h in range(N_HEADS)]
    return jnp.concatenate(rows + [jnp.zeros((8 - N_HEADS, HEAD_DIM), q.dtype)], axis=0)


def _nsa_select_kernel(c_ref, z_ref, pos_ref, w1_ref, w2_ref, cover_ref, ocmp_ref, idx_ref, psum_ref,
                       *, n_cmp, n_sel, qpos):
    bk, bv = _cmp_bias(pos_ref, w1_ref)
    kc, vc = _compress_tail(c_ref[...], bk, bv, w2_ref[0], w2_ref[1])
    q = z_ref[:, OFF_QNSA:OFF_QNSA + GROUP_W]
    q4 = _stack_heads(q)
    nc = kc.shape[0]
    s = _dot_nt(q4.astype(BF16), kc.astype(BF16)) * ATTN_SCALE
    jj = lax.broadcasted_iota(jnp.int32, (1, nc), 1)
    p = _masked_softmax(s, (jj * CMP_STRIDE + (L_CMP - 1) <= qpos) & (jj < n_cmp))
    ocmp_ref[...] = _dot(p.astype(BF16), vc.astype(BF16))[0:N_HEADS]
    b = pl.program_id(0)
    psum_ref[pl.ds(b, 1), :] = jnp.sum(p[0:N_HEADS], axis=0, keepdims=True)

    @pl.when(b == pl.num_programs(0) - 1)
    def _():
        imp = _dot(psum_ref[...], cover_ref[...], precision=HIGHEST)
        ns, nsp = imp.shape
        blk = lax.broadcasted_iota(jnp.int32, (1, nsp), 1)
        fblk = blk.astype(F32)
        cur = qpos // L_SEL
        imp = jnp.where((blk == 0) | (blk == cur) | (blk == cur - 1), jnp.inf, imp)
        imp = jnp.where(blk * L_SEL > qpos, -jnp.inf, imp)
        work = jnp.where(blk < n_sel, jnp.maximum(imp, -3e38), -jnp.inf)
        out_lane = lax.broadcasted_iota(jnp.int32, (1, LANES), 1)
        idx = jnp.zeros((ns, LANES), jnp.int32)
        for t in range(N_SEL):
            mx = jnp.max(work, axis=-1, keepdims=True)
            pick = jnp.min(jnp.where(work == mx, fblk, float(nsp)), axis=-1, keepdims=True)
            idx = jnp.where(out_lane == t, pick.astype(jnp.int32), idx)
            work = jnp.where(fblk == pick, -jnp.inf, work)
        idx_ref[...] = idx


def _nsa_select(c, z3, pos, w1, w2, qpos):
    s, nc, _ = c.shape
    n_cmp = nc - 1
    n_sel = qpos // L_SEL + 1
    nsp = -(-n_sel // LANES) * LANES
    cover = _cover_matrix(nc, n_cmp, nsp)
    kern = functools.partial(_nsa_select_kernel, n_cmp=n_cmp, n_sel=n_sel, qpos=qpos)
    cst = lambda shape: pl.BlockSpec(shape, lambda b: (0,) * len(shape))
    return pl.pallas_call(
        kern,
        grid=(s,),
        in_specs=[pl.BlockSpec((None, nc, 4 * HEAD_DIM), lambda b: (b, 0, 0)),
                  pl.BlockSpec((None, 1, N_IN_PAD), lambda b: (b, 0, 0)),
                  cst(pos.shape), cst(w1.shape), cst(w2.shape), cst(cover.shape)],
        out_specs=[pl.BlockSpec((None, N_HEADS, HEAD_DIM), lambda b: (b, 0, 0)),
                   pl.BlockSpec((s, LANES), lambda b: (0, 0))],
        out_shape=[jax.ShapeDtypeStruct((s, N_HEADS, HEAD_DIM), F32), jax.ShapeDtypeStruct((s, LANES), jnp.int32)],
        scratch_shapes=[pltpu.VMEM((s, nc), F32)],
        compiler_params=_cparams(("arbitrary",)),
        name="nsa_select",
    )(c, z3, pos, w1, w2, cover)


def _col_from_row(row):
    n = row.shape[1]
    eye = lax.broadcasted_iota(jnp.int32, (n, n), 0) == lax.broadcasted_iota(jnp.int32, (n, n), 1)
    return jnp.sum(jnp.where(eye, row, 0.0), axis=-1, keepdims=True)


def _nsa_sample_kernel(pt_ref, idx_ref, cache_ref, z_ref, ocmp_ref, win_ref, o_ref, wnew_ref, buf_ref, sem_ref,
                       *, layer, page, qpos):
    b = pl.program_id(0)
    npg = pt_ref.shape[1]
    per_page = page // L_SEL

    slot = b % 2

    def copy(bb, t, sl):
        pg = jnp.minimum(idx_ref[bb, t] // per_page, npg - 1)
        return pltpu.make_async_copy(cache_ref.at[layer, pt_ref[bb, pg], pl.ds(2, 2)], buf_ref.at[sl, t],
                                     sem_ref.at[sl])

    @pl.when(b == 0)
    def _():
        for t in range(N_SEL):
            copy(b, t, 0).start()

    @pl.when(b + 1 < pl.num_programs(0))
    def _():
        for t in range(N_SEL):
            copy(b + 1, t, 1 - slot).start()

    z = z_ref[...]
    q = z[:, OFF_QNSA:OFF_QNSA + GROUP_W]
    q4b = _stack_heads(q).astype(BF16)
    kvn = z[:, OFF_KVNSA:OFF_KVNSA + GROUP_W]
    ks_new, vs_new = kvn[:, 2 * HEAD_DIM:3 * HEAD_DIM], kvn[:, 3 * HEAD_DIM:4 * HEAD_DIM]
    kvw = z[:, OFF_KVWIN:OFF_KVWIN + 2 * HEAD_DIM]
    kw_new, vw_new = kvw[:, 0:HEAD_DIM], kvw[:, HEAD_DIM:2 * HEAD_DIM]
    q4r = q4b.astype(F32)

    def new_score(k_new):
        kb = k_new.astype(BF16).astype(F32)
        return jnp.sum(q4r * kb, axis=-1, keepdims=True) * ATTN_SCALE

    wk = win_ref[0]
    wv = win_ref[1]
    sw = _dot(q4b, wk.astype(BF16)) * ATTN_SCALE
    sw_new = new_score(kw_new)
    mw = jnp.maximum(jnp.max(sw, axis=-1, keepdims=True), sw_new)
    ew = jnp.exp(sw - mw)
    ew_new = jnp.exp(sw_new - mw)
    dw = jnp.sum(ew, axis=-1, keepdims=True) + ew_new
    pw = ew * (1.0 / dw)
    pw_new = ew_new * (1.0 / dw)
    o_win = _dot_nt(pw.astype(BF16), wv.astype(BF16)) \
        + pw_new.astype(BF16).astype(F32) * vw_new.astype(BF16).astype(F32)
    wl = wk.shape[1]
    lane = lax.broadcasted_iota(jnp.int32, (1, wl), 1)
    wnew_ref[0] = jnp.where(lane == wl - 1, _col_from_row(kw_new), pltpu.roll(wk, wl - 1, 1))
    wnew_ref[1] = jnp.where(lane == wl - 1, _col_from_row(vw_new), pltpu.roll(wv, wl - 1, 1))

    for t in range(N_SEL):
        copy(b, t, slot).wait()

    n_past = (qpos // L_SEL)
    plane = lax.broadcasted_iota(jnp.int32, (1, page), 1)
    scores = []
    new_taken = jnp.int32(0)
    for t in range(N_SEL):
        bid = idx_ref[b, t]
        in_past = bid < n_past
        half = bid % per_page
        valid = (plane >= half * L_SEL) & (plane < (half + 1) * L_SEL) & in_past
        sc = _dot(q4b, buf_ref[slot, t, 0].astype(BF16)) * ATTN_SCALE
        scores.append(jnp.where(valid, sc, -jnp.inf))
        new_taken = new_taken + (bid == n_past).astype(jnp.int32)
    has_new = new_taken > 0
    ss_new = jnp.where(has_new, new_score(ks_new), -jnp.inf)
    ms = ss_new
    for sc in scores:
        ms = jnp.maximum(ms, jnp.max(sc, axis=-1, keepdims=True))
    ms = jnp.where(ms == -jnp.inf, 0.0, ms)
    es_new = jnp.exp(ss_new - ms)
    ds = es_new
    es = []
    for sc in scores:
        e = jnp.exp(sc - ms)
        es.append(e)
        ds = ds + jnp.sum(e, axis=-1, keepdims=True)
    inv = 1.0 / jnp.where(ds > 0, ds, 1.0)
    o_sel = (es_new * inv).astype(BF16).astype(F32) * vs_new.astype(BF16).astype(F32)
    for t in range(N_SEL):
        o_sel = o_sel + _dot_nt((es[t] * inv).astype(BF16), buf_ref[slot, t, 1].astype(BF16))

    g = _sigmoid(z[:, OFF_GATE:OFF_GATE + LANES])
    o_cmp = ocmp_ref[...]
    for h in range(N_HEADS):
        oh = (g[:, h:h + 1] * o_cmp[h:h + 1] + g[:, N_HEADS + h:N_HEADS + h + 1] * o_sel[h:h + 1]
              + g[:, 2 * N_HEADS + h:2 * N_HEADS + h + 1] * o_win[h:h + 1])
        o_ref[:, h * HEAD_DIM:(h + 1) * HEAD_DIM] = oh


def _nsa_sample(page_table, idx, cache_t, z3, ocmp, win_t, layer, qpos):
    s = z3.shape[0]
    page = cache_t.shape[-1]
    wl = win_t.shape[-1]
    kern = functools.partial(_nsa_sample_kernel, layer=layer, page=page, qpos=qpos)
    return pl.pallas_call(
        kern,
        grid_spec=pltpu.PrefetchScalarGridSpec(
            num_scalar_prefetch=2,
            grid=(s,),
            in_specs=[pl.BlockSpec(memory_space=pl.ANY),
                      pl.BlockSpec((None, 1, N_IN_PAD), lambda b, pt, ix: (b, 0, 0)),
                      pl.BlockSpec((None, N_HEADS, HEAD_DIM), lambda b, pt, ix: (b, 0, 0)),
                      pl.BlockSpec((None, None, 2, HEAD_DIM, wl), lambda b, pt, ix: (layer, b, 0, 0, 0))],
            out_specs=[pl.BlockSpec((None, 1, GROUP_W), lambda b, pt, ix: (b, 0, 0)),
                       pl.BlockSpec((None, 2, HEAD_DIM, wl), lambda b, pt, ix: (b, 0, 0, 0))],
            scratch_shapes=[pltpu.VMEM((2, N_SEL, 2, HEAD_DIM, page), F32), pltpu.SemaphoreType.DMA((2,))]),
        out_shape=[jax.ShapeDtypeStruct((s, 1, GROUP_W), F32), jax.ShapeDtypeStruct((s, 2, HEAD_DIM, wl), F32)],
        compiler_params=_cparams(("arbitrary",)),
        name="nsa_sample",
    )(page_table, idx, cache_t, z3, ocmp, win_t)


_MOBA_PAGES = 16


def _moba_gate_kernel(pt_ref, cache_ref, qcol_ref, idx_ref, buf_ref, part_ref, sem_ref,
                      *, layer, n_steps, page, nb, cur):
    b = pl.program_id(0)
    s = pl.program_id(1)
    g = _MOBA_PAGES
    step = b * n_steps + s
    total = pl.num_programs(0) * n_steps
    ppb = MOBA_BLOCK // page
    bps = g // ppb

    def copies(stp, slot):
        bb = stp // n_steps
        ss = stp % n_steps
        return [pltpu.make_async_copy(cache_ref.at[layer, pt_ref[bb, ss * g + k], 0],
                                      buf_ref.at[slot, k], sem_ref.at[slot]) for k in range(g)]

    @pl.when(step == 0)
    def _():
        for cp in copies(step, 0):
            cp.start()

    slot = step % 2

    @pl.when(step + 1 < total)
    def _():
        for cp in copies(step + 1, 1 - slot):
            cp.start()

    for cp in copies(step, slot):
        cp.wait()

    for j in range(bps):
        for h in range(N_HEADS):
            qh = qcol_ref[h * HEAD_DIM:(h + 1) * HEAD_DIM, :]
            acc = buf_ref[slot, j * ppb, h] * qh
            for k in range(1, ppb):
                acc = acc + buf_ref[slot, j * ppb + k, h] * qh
            part_ref[h * nb + s * bps + j] = jnp.sum(acc.reshape(HEAD_DIM // 8, 8, page), axis=0)

    @pl.when(s == n_steps - 1)
    def _():
        ones = jnp.ones((8, page), F32)
        part = jnp.sum(part_ref[...], axis=1)
        gate = _dot_nt(ones, part, precision=HIGHEST)[0:1] * (1.0 / MOBA_BLOCK)
        lane = lax.broadcasted_iota(jnp.int32, gate.shape, 1)
        out_lane = lax.broadcasted_iota(jnp.int32, (1, LANES), 1)
        idx = jnp.zeros((1, LANES), jnp.int32)
        for h in range(N_HEADS):
            n_of = lane - h * nb
            f_of = n_of.astype(F32)
            inh = (n_of >= 0) & (n_of < nb) & (n_of < cur)
            work = jnp.where(inh, jnp.maximum(gate, -3e38), -jnp.inf)
            for t in range(MOBA_TOPK):
                mx = jnp.max(work, axis=-1, keepdims=True)
                pick = jnp.min(jnp.where((work == mx) & inh, f_of, float(4 * nb)), axis=-1, keepdims=True)
                idx = jnp.where(out_lane == h * MOBA_TOPK + t, pick.astype(jnp.int32), idx)
                work = jnp.where(f_of == pick, -jnp.inf, work)
        idx_ref[...] = idx


def _moba_gate(page_table, cache_t, qcol, layer, cur):
    s, npg = page_table.shape
    page = cache_t.shape[-1]
    g = _MOBA_PAGES
    n_steps = npg // g
    nb = npg * page // MOBA_BLOCK
    kern = functools.partial(_moba_gate_kernel, layer=layer, n_steps=n_steps, page=page, nb=nb, cur=cur)
    return pl.pallas_call(
        kern,
        grid_spec=pltpu.PrefetchScalarGridSpec(
            num_scalar_prefetch=1,
            grid=(s, n_steps),
            in_specs=[pl.BlockSpec(memory_space=pl.ANY),
                      pl.BlockSpec((None, N_HEADS * HEAD_DIM, page), lambda b, i, pt: (b, 0, 0))],
            out_specs=pl.BlockSpec((None, 1, LANES), lambda b, i, pt: (b, 0, 0)),
            scratch_shapes=[pltpu.VMEM((2, g, N_HEADS, HEAD_DIM, page), F32),
                            pltpu.VMEM((N_HEADS * nb, 8, page), F32),
                            pltpu.SemaphoreType.DMA((2,))]),
        out_shape=jax.ShapeDtypeStruct((s, 1, LANES), jnp.int32),
        compiler_params=_cparams(("arbitrary", "arbitrary")),
        name="moba_gate",
    )(page_table, cache_t, qcol)


def _moba_sample_kernel(pt_ref, idx_ref, cache_ref, z_ref, o_ref, buf_ref, sem_ref, *, layer, page, cur):
    b = pl.program_id(0)
    npg = pt_ref.shape[1]
    ppb = MOBA_BLOCK // page
    n_src = MOBA_TOPK * ppb

    slot = b % 2

    def copy(bb, sl, h, t, k, kv):
        blk = jnp.minimum(idx_ref[bb, h * MOBA_TOPK + t], npg // ppb - 1)
        return pltpu.make_async_copy(cache_ref.at[layer, pt_ref[bb, blk * ppb + k], kv, h],
                                     buf_ref.at[sl, kv, h, t * ppb + k], sem_ref.at[sl])

    every = [(h, t, k, kv) for h in range(N_HEADS) for t in range(MOBA_TOPK) for k in range(ppb) for kv in range(2)]

    @pl.when(b == 0)
    def _():
        for a in every:
            copy(b, 0, *a).start()

    @pl.when(b + 1 < pl.num_programs(0))
    def _():
        for a in every:
            copy(b + 1, 1 - slot, *a).start()

    z = z_ref[...]
    q = z[:, OFF_QMOBA:OFF_QMOBA + GROUP_W]
    k_new = z[:, OFF_KVMOBA:OFF_KVMOBA + GROUP_W]
    v_new = z[:, OFF_KVMOBA + GROUP_W:OFF_KVMOBA + 2 * GROUP_W]
    for a in every:
        copy(b, slot, *a).wait()

    for h in range(N_HEADS):
        lo, hi = h * HEAD_DIM, (h + 1) * HEAD_DIM
        qh = jnp.concatenate([q[:, lo:hi]] * 8, axis=0).astype(BF16)
        qr = q[:, lo:hi].astype(BF16).astype(F32)
        s_new = jnp.sum(qr * k_new[:, lo:hi].astype(BF16).astype(F32), axis=-1, keepdims=True) * ATTN_SCALE
        scores = []
        m = s_new
        for t in range(MOBA_TOPK):
            ok = idx_ref[b, h * MOBA_TOPK + t] < cur
            for k in range(ppb):
                sc = _dot(qh, buf_ref[slot, 0, h, t * ppb + k].astype(BF16))[0:1] * ATTN_SCALE
                sc = jnp.where(ok, sc, -jnp.inf)
                scores.append(sc)
                m = jnp.maximum(m, jnp.max(sc, axis=-1, keepdims=True))
        e_new = jnp.exp(s_new - m)
        d = e_new
        es = []
        for sc in scores:
            e = jnp.exp(sc - m)
            es.append(e)
            d = d + jnp.sum(e, axis=-1, keepdims=True)
        inv = 1.0 / d
        o = (e_new * inv).astype(BF16).astype(F32) * v_new[:, lo:hi].astype(BF16).astype(F32)
        for n in range(n_src):
            pn = jnp.concatenate([es[n] * inv] * 8, axis=0).astype(BF16)
            o = o + _dot_nt(pn, buf_ref[slot, 1, h, n].astype(BF16))[0:1]
        o_ref[:, lo:hi] = o


def _moba_sample(page_table, idx, cache_t, z3, layer, cur):
    s = z3.shape[0]
    page = cache_t.shape[-1]
    ppb = MOBA_BLOCK // page
    kern = functools.partial(_moba_sample_kernel, layer=layer, page=page, cur=cur)
    return pl.pallas_call(
        kern,
        grid_spec=pltpu.PrefetchScalarGridSpec(
            num_scalar_prefetch=2,
            grid=(s,),
            in_specs=[pl.BlockSpec(memory_space=pl.ANY),
                      pl.BlockSpec((None, 1, N_IN_PAD), lambda b, pt, ix: (b, 0, 0))],
            out_specs=pl.BlockSpec((None, 1, GROUP_W), lambda b, pt, ix: (b, 0, 0)),
            scratch_shapes=[pltpu.VMEM((2, 2, N_HEADS, MOBA_TOPK * ppb, HEAD_DIM, page), F32),
                            pltpu.SemaphoreType.DMA((2,))]),
        out_shape=jax.ShapeDtypeStruct((s, 1, GROUP_W), F32),
        compiler_params=_cparams(("arbitrary",)),
        name="moba_sample",
    )(page_table, idx, cache_t, z3)


def _cmp_weight_stack(w1):
    w = w1.reshape(2, 2, CMP_STRIDE, HEAD_DIM, HEAD_DIM)
    zero = jnp.zeros((CMP_STRIDE, HEAD_DIM, 2 * HEAD_DIM), F32)
    top = jnp.concatenate([w[0, 0], w[0, 1], zero], axis=-1)
    bot = jnp.concatenate([zero, w[1, 0], w[1, 1]], axis=-1)
    per_r = jnp.concatenate([top, bot], axis=1)
    return per_r.reshape(CMP_STRIDE // 2, 4 * HEAD_DIM, 4 * HEAD_DIM).astype(BF16)


def kernel(x_prompt, x_sample, cache_moba_kv, cache_nsa_kv, state_nsa_win, state_conv, state_pool, page_table,
           w_in, conv_dw, conv_dw_b, conv_ln_g, conv_ln_b, conv_pw, nsa_cmp_pos, nsa_cmp_w1, nsa_cmp_w2,
           pool_w, pool_scale, w_out, ln1_g, ln1_b, w_up, w_down, ln2_g, ln2_b):
    bp, t, _ = x_prompt.shape
    bs = x_sample.shape[0]
    depth = w_in.shape[0]
    page = cache_nsa_kv.shape[2]
    past_len = page_table.shape[1] * page
    win_len = state_nsa_win.shape[2]
    assert x_sample.shape[1] == 1 and win_len == WINDOW and t % 512 == 0 and t >= WINDOW + 128
    assert page % L_SEL == 0 and MOBA_BLOCK % page == 0 and past_len % MOBA_BLOCK == 0
    assert past_len // MOBA_BLOCK >= MOBA_TOPK and past_len // L_SEL + 1 >= N_SEL and past_len >= POOL_MAX
    assert page_table.shape[1] % _CMP_PAGES == 0 and page_table.shape[1] % _MOBA_PAGES == 0

    nsa_t = jnp.transpose(cache_nsa_kv, (0, 1, 3, 4, 2))
    moba_t = jnp.transpose(cache_moba_kv, (0, 1, 3, 4, 5, 2))
    win_t = jnp.transpose(state_nsa_win, (0, 1, 3, 4, 2))
    conv_t = jnp.transpose(state_conv, (0, 2, 1, 3))
    pool_t = jnp.transpose(state_pool, (0, 2, 1, 3))

    perm = _in_perm()
    row = lambda v: v.reshape(1, -1)
    yp = x_prompt.reshape(bp * t, D_MODEL)
    ys = x_sample.reshape(bs, D_MODEL)
    st_p, st_s = [], []
    for l in range(depth):
        w_in_l = jnp.pad(w_in[l][:, perm], ((0, 0), (0, N_IN_PAD - N_IN))).astype(BF16)
        pw = conv_pw[l].astype(BF16)
        plw = jax.scipy.linalg.block_diag(*[pool_w[l, g] for g in range(len(POOL_WINDOWS))]).astype(BF16)
        w1 = nsa_cmp_w1[l].astype(BF16)
        w2 = nsa_cmp_w2[l].astype(BF16)
        pos = jnp.broadcast_to(nsa_cmp_pos[l].reshape(2, 1, L_CMP * HEAD_DIM), (2, 8, L_CMP * HEAD_DIM))
        w_out_l = w_out[l].astype(BF16)
        w_up_l = w_up[l].astype(BF16)
        w_down_l = w_down[l].astype(BF16)
        small = (conv_dw[l], row(conv_dw_b[l]), row(conv_ln_g[l]), row(conv_ln_b[l]), pw, plw, row(pool_scale[l]))

        z = _in_proj(yp, w_in_l)
        z3 = z.reshape(bp, t, N_IN_PAD)
        y_conv, y_pool, conv_new32 = _conv_pool_prompt(z3, *small)
        nchunk = t // CMP_STRIDE
        chunks_k = z3[:, :, OFF_KVNSA:OFF_KVNSA + HEAD_DIM].reshape(bp, nchunk, CMP_STRIDE * HEAD_DIM)
        chunks_v = z3[:, :, OFF_KVNSA + HEAD_DIM:OFF_KVNSA + 2 * HEAD_DIM].reshape(bp, nchunk, CMP_STRIDE * HEAD_DIM)
        kc, vc = _compress_prompt(chunks_k, chunks_v, pos, w1, w2)
        o_nsa = _nsa_prompt(z3, kc, vc)
        o_moba = _moba_prompt(z3)
        flat = lambda a: a.reshape(bp * t, GROUP_W)
        x1 = _out_proj((flat(y_conv), flat(o_nsa), flat(o_moba), flat(y_pool)), yp, w_out_l, row(ln1_g[l]), row(ln1_b[l]))
        yp = _ffn(x1, w_up_l, w_down_l, row(ln2_g[l]), row(ln2_b[l]))
        moba_rows = z3[:, :, OFF_KVMOBA:OFF_KVMOBA + 2 * GROUP_W].reshape(bp, t, 2, N_HEADS, HEAD_DIM)
        nsa_rows = z3[:, :, OFF_KVNSA:OFF_KVNSA + GROUP_W].reshape(bp, t, 4, HEAD_DIM)
        win_new = z3[:, t - win_len:, OFF_KVWIN:OFF_KVWIN + 2 * HEAD_DIM].reshape(bp, win_len, 2, HEAD_DIM)
        conv_new = conv_new32[:, _CONV_HALO - (CONV_W - 1):]
        pool_new = z3[:, t - (POOL_MAX - 1):, OFF_UPOOL:OFF_UPOOL + GROUP_W]
        st_p.append((moba_rows, nsa_rows, win_new, conv_new, pool_new))

        zs = _in_proj(ys, w_in_l)
        zs3 = zs.reshape(bs, 1, N_IN_PAD)
        ys_conv, ys_pool, conv_new_t, pool_new_t = _conv_pool_sample(zs, conv_t[l], pool_t[l], *small)
        c = _compress_pages(page_table, nsa_t, _cmp_weight_stack(nsa_cmp_w1[l]), l)
        o_cmp, sel_idx = _nsa_select(c, zs3, pos, w1, w2, past_len)
        os_nsa, win_new_t = _nsa_sample(page_table, sel_idx, nsa_t, zs3, o_cmp, win_t, l, past_len)
        q_m = zs[:, OFF_QMOBA:OFF_QMOBA + GROUP_W]
        qcol = jnp.broadcast_to(q_m[:, :, None], (bs, GROUP_W, page))
        top_idx = _moba_gate(page_table, moba_t, qcol, l, past_len // MOBA_BLOCK)
        os_moba = _moba_sample(page_table, top_idx.reshape(bs, LANES), moba_t, zs3, l, past_len // MOBA_BLOCK)
        x1s = _out_proj((ys_conv, os_nsa.reshape(bs, GROUP_W), os_moba.reshape(bs, GROUP_W), ys_pool), ys, w_out_l,
                        row(ln1_g[l]), row(ln1_b[l]))
        ys = _ffn(x1s, w_up_l, w_down_l, row(ln2_g[l]), row(ln2_b[l]))
        st_s.append((zs[:, OFF_KVMOBA:OFF_KVMOBA + 2 * GROUP_W].reshape(bs, 1, 2, N_HEADS, HEAD_DIM),
                     zs[:, OFF_KVNSA:OFF_KVNSA + GROUP_W].reshape(bs, 1, 4, HEAD_DIM),
                     jnp.transpose(win_new_t, (0, 3, 1, 2)),
                     jnp.transpose(conv_new_t, (1, 0, 2)),
                     jnp.transpose(pool_new_t, (1, 0, 2))))

    stk = lambda sts, i: jnp.stack([s[i] for s in sts], axis=0)
    return (yp.reshape(bp, t, D_MODEL), ys.reshape(bs, 1, D_MODEL),
            stk(st_p, 0), stk(st_s, 0), stk(st_p, 1), stk(st_s, 1), stk(st_p, 2), stk(st_s, 2),
            stk(st_p, 3), stk(st_s, 3), stk(st_p, 4), stk(st_s, 4))
```

```python
import functools

import numpy as np
import jax
import jax.numpy as jnp
from jax import lax
from jax.experimental import pallas as pl
from jax.experimental.pallas import tpu as pltpu

F32 = jnp.float32
BF16 = jnp.bfloat16
HIGHEST = lax.Precision.HIGHEST

D_MODEL = 1024
HEAD_DIM = 64
GROUP_W = 256
N_HEADS = 4
CONV_W = 31
L_CMP = 32
CMP_STRIDE = 16
L_SEL = 64
N_SEL = 16
WINDOW = 512
MOBA_BLOCK = 256
MOBA_TOPK = 3
POOL_WINDOWS = (2, 4, 8, 16)
POOL_GW = 64
POOL_MAX = 16
D_FF = 4096
DEPTH = 2
ALPHA = (2 * DEPTH) ** 0.25
LN_EPS = 1e-5
ATTN_SCALE = HEAD_DIM ** -0.5
NEG = -1e30

OFF_UCONV, OFF_QNSA, OFF_QMOBA, OFF_UPOOL, OFF_GATE = 0, 512, 768, 1024, 1280
N_MAIN = 1408
OFF_KVMOBA, OFF_KVNSA, OFF_KVWIN = 1408, 1920, 2176
N_IN_PAD = 2304
N_GATE = 12
LANES = 128
VMEM_LIMIT = 56 * 1024 * 1024


def _in_weights(w):
    main = np.concatenate([np.arange(0, 512), np.arange(512, 768), np.arange(1164, 1420), np.arange(1932, 2188),
                           np.arange(1152, 1164)])
    kv = np.concatenate([np.arange(1420, 1932), np.arange(768, 1024), np.arange(1024, 1152)])
    pad = jnp.zeros((w.shape[0], LANES - N_GATE), w.dtype)
    return jnp.concatenate([w[:, main], pad, w[:, kv]], axis=1).astype(BF16)


def _ln(x, g, b):
    mu = jnp.mean(x, axis=-1, keepdims=True)
    xc = x - mu
    var = jnp.mean(xc * xc, axis=-1, keepdims=True)
    return xc * lax.rsqrt(var + LN_EPS) * g + b


def _sigmoid(x):
    return 1.0 / (1.0 + jnp.exp(-x))


def _silu(x):
    return x * _sigmoid(x)


def _dot(a, b, **kw):
    return jnp.dot(a, b, preferred_element_type=F32, **kw)


def _dot_nt(a, b, **kw):
    return lax.dot_general(a, b, (((1,), (1,)), ((), ())), preferred_element_type=F32, **kw)


def _masked_softmax(s, mask):
    r = s.shape[0]
    s = jnp.where(mask, s, -jnp.inf)
    m = jnp.max(_lane_partial(s, jnp.maximum), axis=-1, keepdims=True)
    m = jnp.broadcast_to(jnp.where(m == -jnp.inf, 0.0, m), (r, LANES))
    e = _exp_shifted(s, m)
    d = jnp.sum(_lane_partial(e, jnp.add), axis=-1, keepdims=True)
    inv = jnp.broadcast_to(1.0 / jnp.where(d > 0, d, 1.0), (r, LANES))
    return jnp.concatenate([e[:, g * LANES:(g + 1) * LANES] * inv for g in range(s.shape[1] // LANES)], axis=-1)


def _lane_partial(s, op):
    out = s[:, 0:LANES]
    for g in range(1, s.shape[1] // LANES):
        out = op(out, s[:, g * LANES:(g + 1) * LANES])
    return out


def _exp_shifted(s, m):
    return jnp.concatenate([jnp.exp(s[:, g * LANES:(g + 1) * LANES] - m) for g in range(s.shape[1] // LANES)], axis=-1)


def _cparams(sem, vmem=None):
    return pltpu.CompilerParams(dimension_semantics=sem, vmem_limit_bytes=vmem or VMEM_LIMIT)


def _in_proj_kernel(x_ref, w_ref, o_ref):
    o_ref[...] = _dot(x_ref[...].astype(BF16), w_ref[...])


def _in_proj(x, w):
    m, k = x.shape
    n = w.shape[1]
    tm = min(m, 512)
    return pl.pallas_call(
        _in_proj_kernel,
        grid=(m // tm,),
        in_specs=[pl.BlockSpec((tm, k), lambda i: (i, 0)), pl.BlockSpec((k, n), lambda i: (0, 0))],
        out_specs=pl.BlockSpec((tm, n), lambda i: (i, 0)),
        out_shape=jax.ShapeDtypeStruct((m, n), F32),
        compiler_params=_cparams(("parallel",)),
        name="in_proj",
    )(x, w)


_IN_PARTS = ((0, N_MAIN), (OFF_KVMOBA, OFF_KVNSA), (OFF_KVNSA, OFF_KVWIN), (OFF_KVWIN, N_IN_PAD))


def _in_proj_split_kernel(x_ref, w_ref, *o_refs):
    z = _dot(x_ref[...].astype(BF16), w_ref[...])
    for (lo, hi), o_ref in zip(_IN_PARTS, o_refs):
        o_ref[...] = z[:, lo:hi]


def _in_proj_split(x, w):
    m, k = x.shape
    tm = min(m, 512)
    return pl.pallas_call(
        _in_proj_split_kernel,
        grid=(m // tm,),
        in_specs=[pl.BlockSpec((tm, k), lambda i: (i, 0)), pl.BlockSpec(w.shape, lambda i: (0, 0))],
        out_specs=[pl.BlockSpec((tm, hi - lo), lambda i: (i, 0)) for lo, hi in _IN_PARTS],
        out_shape=[jax.ShapeDtypeStruct((m, hi - lo), F32) for lo, hi in _IN_PARTS],
        compiler_params=_cparams(("parallel",)),
        name="in_proj_split",
    )(x, w)


def _out_proj_kernel(a_ref, b_ref, c_ref, d_ref, x_ref, w_ref, g_ref, bt_ref, o_ref):
    mix = jnp.concatenate([a_ref[...], b_ref[...], c_ref[...], d_ref[...]], axis=-1).astype(BF16)
    y = _dot(mix, w_ref[...])
    o_ref[...] = _ln(ALPHA * x_ref[...] + y, g_ref[...], bt_ref[...])


def _out_proj(pieces, x, w, layer, g, b):
    m = x.shape[0]
    tm = min(m, 512)
    pspec = pl.BlockSpec((tm, GROUP_W), lambda i: (i, 0))
    return pl.pallas_call(
        _out_proj_kernel,
        grid=(m // tm,),
        in_specs=[pspec, pspec, pspec, pspec,
                  pl.BlockSpec((tm, D_MODEL), lambda i: (i, 0)),
                  pl.BlockSpec((None, D_MODEL, D_MODEL), lambda i: (layer, 0, 0)),
                  pl.BlockSpec((1, D_MODEL), lambda i: (0, 0)),
                  pl.BlockSpec((1, D_MODEL), lambda i: (0, 0))],
        out_specs=pl.BlockSpec((tm, D_MODEL), lambda i: (i, 0)),
        out_shape=jax.ShapeDtypeStruct((m, D_MODEL), F32),
        compiler_params=_cparams(("parallel",)),
        name="out_proj_ln",
    )(*pieces, x, w, g, b)


def _ffn_kernel(x_ref, wu_ref, wd_ref, g_ref, b_ref, o_ref, acc_ref):
    j = pl.program_id(1)

    @pl.when(j == 0)
    def _():
        acc_ref[...] = jnp.zeros_like(acc_ref)

    h = jnp.maximum(_dot(x_ref[...].astype(BF16), wu_ref[...]), 0.0)
    acc_ref[...] += _dot((h * h).astype(BF16), wd_ref[...])

    @pl.when(j == pl.num_programs(1) - 1)
    def _():
        o_ref[...] = _ln(ALPHA * x_ref[...] + acc_ref[...], g_ref[...], b_ref[...])


def _ffn(x, wu, wd, layer, g, b):
    m = x.shape[0]
    tm = min(m, 1024)
    tf = 1024
    return pl.pallas_call(
        _ffn_kernel,
        grid=(m // tm, D_FF // tf),
        in_specs=[pl.BlockSpec((tm, D_MODEL), lambda i, j: (i, 0)),
                  pl.BlockSpec((None, D_MODEL, tf), lambda i, j: (layer, 0, j)),
                  pl.BlockSpec((None, tf, D_MODEL), lambda i, j: (layer, j, 0)),
                  pl.BlockSpec((1, D_MODEL), lambda i, j: (0, 0)),
                  pl.BlockSpec((1, D_MODEL), lambda i, j: (0, 0))],
        out_specs=pl.BlockSpec((tm, D_MODEL), lambda i, j: (i, 0)),
        out_shape=jax.ShapeDtypeStruct((m, D_MODEL), F32),
        scratch_shapes=[pltpu.VMEM((tm, D_MODEL), F32)],
        compiler_params=_cparams(("parallel", "arbitrary")),
        name="ffn_ln",
    )(x, wu, wd, g, b)


_CONV_HALO = 32
_POOL_HALO = 16


def _pool_groups(rows_ref, buf_ref, tt, cnt_fn):
    n = _POOL_HALO + tt
    lane = lax.broadcasted_iota(jnp.int32, (1, GROUP_W), 1)
    cur = rows_ref[pl.ds(_POOL_HALO, tt), :]
    out = jnp.zeros((tt, GROUP_W), F32)
    shift = 1
    for g, w in enumerate(POOL_WINDOWS):
        src = rows_ref if g == 0 else buf_ref.at[(g - 1) % 2]
        dst = buf_ref.at[g % 2]
        lo = 2 * shift - 1
        dst[pl.ds(lo, n - lo), :] = src[pl.ds(lo, n - lo), :] + src[pl.ds(lo - shift, n - lo), :]
        shift *= 2
        s = dst[pl.ds(_POOL_HALO, tt), :]
        mean = s / cnt_fn(w)
        out = jnp.where((lane >= g * POOL_GW) & (lane < (g + 1) * POOL_GW), mean, out)
    return out - cur


def _conv_pool_prompt_kernel(u_ref, uh_ref, p_ref, ph_ref, dw_ref, dwb_ref, lg_ref, lb_ref, pw_ref,
                             plw_ref, pls_ref, yc_ref, yp_ref, cnew_ref, zc_ref, res_ref, rows_ref, buf_ref, *, tt):
    i = pl.program_id(1)
    first = i == 0
    u = u_ref[...]
    zg = u[:, :GROUP_W] * _sigmoid(u[:, GROUP_W:])
    uh = uh_ref[...]
    zh = uh[:, :GROUP_W] * _sigmoid(uh[:, GROUP_W:])
    zc_ref[pl.ds(0, _CONV_HALO), :] = jnp.where(first, 0.0, zh)
    zc_ref[pl.ds(_CONV_HALO, tt), :] = zg
    zc_ref[pl.ds(_CONV_HALO + tt, 8), :] = jnp.zeros((8, GROUP_W), F32)
    off = _CONV_HALO - (CONV_W - 1)
    y = jnp.zeros((tt, GROUP_W), F32) + dwb_ref[...]
    for r in range(8):
        acc = None
        for a in range(-(-(off + CONV_W) // 8)):
            k = 8 * a + r - off
            if 0 <= k < CONV_W:
                term = dw_ref[k:k + 1, :] * zc_ref[pl.ds(8 * a, tt + 8), :]
                acc = term if acc is None else acc + term
        if r == 0:
            y = y + acc[0:tt]
        else:
            res_ref[r - 1] = acc
            y = y + res_ref[r - 1, pl.ds(r, tt), :]
    y = _silu(_ln(y, lg_ref[...], lb_ref[...]))
    yc_ref[...] = _dot(y.astype(BF16), pw_ref[...]).astype(yc_ref.dtype)

    @pl.when(i == pl.num_programs(1) - 1)
    def _():
        cnew_ref[...] = zc_ref[pl.ds(tt, _CONV_HALO), :]

    rows_ref[pl.ds(0, _POOL_HALO), :] = jnp.where(first, 0.0, ph_ref[...])
    rows_ref[pl.ds(_POOL_HALO, tt), :] = p_ref[...]
    qpos1 = i * tt + lax.broadcasted_iota(jnp.int32, (tt, 1), 0) + 1
    d = _pool_groups(rows_ref, buf_ref, tt, lambda w: jnp.minimum(w, qpos1).astype(F32))
    yp_ref[...] = (_dot(d.astype(BF16), plw_ref[...]) * pls_ref[...]).astype(yp_ref.dtype)


def _conv_pool_prompt(z3, dw, dwb, lg, lb, pw, plw, pls):
    b, t, _ = z3.shape
    tt = 512
    nt = t // tt
    kern = functools.partial(_conv_pool_prompt_kernel, tt=tt)
    cst = lambda shape: pl.BlockSpec(shape, lambda bi, i: (0,) * len(shape))
    return pl.pallas_call(
        kern,
        grid=(b, nt),
        in_specs=[
            pl.BlockSpec((None, tt, 512), lambda bi, i: (bi, i, OFF_UCONV // 512)),
            pl.BlockSpec((None, _CONV_HALO, 512),
                         lambda bi, i: (bi, jnp.maximum(i * (tt // _CONV_HALO) - 1, 0), OFF_UCONV // 512)),
            pl.BlockSpec((None, tt, GROUP_W), lambda bi, i: (bi, i, OFF_UPOOL // GROUP_W)),
            pl.BlockSpec((None, _POOL_HALO, GROUP_W),
                         lambda bi, i: (bi, jnp.maximum(i * (tt // _POOL_HALO) - 1, 0), OFF_UPOOL // GROUP_W)),
            cst((CONV_W, GROUP_W)), cst((1, GROUP_W)), cst((1, GROUP_W)), cst((1, GROUP_W)),
            cst((GROUP_W, GROUP_W)), cst((GROUP_W, GROUP_W)), cst((1, GROUP_W)),
        ],
        out_specs=[
            pl.BlockSpec((None, tt, GROUP_W), lambda bi, i: (bi, i, 0)),
            pl.BlockSpec((None, tt, GROUP_W), lambda bi, i: (bi, i, 0)),
            pl.BlockSpec((None, _CONV_HALO, GROUP_W), lambda bi, i: (bi, 0, 0)),
        ],
        out_shape=[jax.ShapeDtypeStruct((b, t, GROUP_W), BF16), jax.ShapeDtypeStruct((b, t, GROUP_W), BF16),
                   jax.ShapeDtypeStruct((b, _CONV_HALO, GROUP_W), F32)],
        scratch_shapes=[pltpu.VMEM((_CONV_HALO + tt + 8, GROUP_W), F32),
                        pltpu.VMEM((7, tt + 8, GROUP_W), F32),
                        pltpu.VMEM((_POOL_HALO + tt, GROUP_W), F32),
                        pltpu.VMEM((2, _POOL_HALO + tt, GROUP_W), F32)],
        compiler_params=_cparams(("parallel", "arbitrary")),
        name="conv_pool_prompt",
    )(z3, z3, z3, z3, dw, dwb, lg, lb, pw, plw, pls)


def _compress_tail(c, bias_k, bias_v, w2k, w2v):
    n = c.shape[0]

    def one(c0, c1, bias, w2):
        hid = c0 + pltpu.roll(c1, n - 1, 0) + bias
        return _dot(_silu(hid).astype(BF16), w2)

    kc = one(c[:, 0:64], c[:, 64:128], bias_k, w2k)
    vc = one(c[:, 128:192], c[:, 192:256], bias_v, w2v)
    return kc, vc


def _cmp_bias(pos_ref, w1_ref):
    bk = _dot(pos_ref[0].astype(BF16), w1_ref[0])[0:1]
    bv = _dot(pos_ref[1].astype(BF16), w1_ref[1])[0:1]
    return bk, bv


def _compress_prompt_kernel(ck_ref, cv_ref, pos_ref, w1_ref, w2_ref, kc_ref, vc_ref):
    bk, bv = _cmp_bias(pos_ref, w1_ref)
    half = CMP_STRIDE * HEAD_DIM
    ck = ck_ref[...].astype(BF16)
    cv = cv_ref[...].astype(BF16)
    c = jnp.concatenate([_dot(ck, w1_ref[0, 0:half, :]), _dot(ck, w1_ref[0, half:2 * half, :]),
                         _dot(cv, w1_ref[1, 0:half, :]), _dot(cv, w1_ref[1, half:2 * half, :])], axis=-1)
    kc, vc = _compress_tail(c, bk, bv, w2_ref[0], w2_ref[1])
    kc_ref[...] = kc
    vc_ref[...] = vc


def _compress_prompt(chunks_k, chunks_v, pos, w1, w2):
    b, nc, kk = chunks_k.shape
    cst = lambda shape: pl.BlockSpec(shape, lambda bi: (0,) * len(shape))
    return pl.pallas_call(
        _compress_prompt_kernel,
        grid=(b,),
        in_specs=[pl.BlockSpec((None, nc, kk), lambda bi: (bi, 0, 0)),
                  pl.BlockSpec((None, nc, kk), lambda bi: (bi, 0, 0)),
                  cst(pos.shape), cst(w1.shape), cst(w2.shape)],
        out_specs=[pl.BlockSpec((None, nc, HEAD_DIM), lambda bi: (bi, 0, 0)),
                   pl.BlockSpec((None, nc, HEAD_DIM), lambda bi: (bi, 0, 0))],
        out_shape=[jax.ShapeDtypeStruct((b, nc, HEAD_DIM), F32)] * 2,
        compiler_params=_cparams(("parallel",)),
        name="compress_prompt",
    )(chunks_k, chunks_v, pos, w1, w2)


def _rank_desc(v, n):
    lane = lax.broadcasted_iota(jnp.int32, v.shape, 1)
    rank = jnp.zeros(v.shape, F32)
    for j in range(n):
        col = v[:, j:j + 1]
        rank = rank + jnp.where((col > v) | ((col == v) & (lane > j)), 1.0, 0.0)
    return rank


def _cover_matrix(n_cmp_pad, n_cmp, n_sel_pad):
    j = np.arange(n_cmp_pad)[:, None]
    i = np.arange(n_sel_pad)[None, :]
    cov = (j * CMP_STRIDE < (i + 1) * L_SEL) & (j * CMP_STRIDE + L_CMP > i * L_SEL) & (j < n_cmp)
    return jnp.asarray(cov.astype(np.float32))


def _nsa_prompt_kernel(q_ref, g_ref, kv_ref, win_ref, kc_ref, vc_ref, covert_ref, o_ref,
                       kt_ref, v_ref, kwt_ref, vw_ref, qa_ref, so_ref, sp_ref, m_ref, l_ref, acc_ref, ow_ref,
                       *, c, tk, n_cmp, ns):
    i = pl.program_id(1)
    qs = i * c
    t = kv_ref.shape[0]
    nsp = LANES - HEAD_DIM

    @pl.when(i == 0)
    def _():
        rows = lax.broadcasted_iota(jnp.int32, (nsp, t), 0)
        cols = lax.broadcasted_iota(jnp.int32, (nsp, t), 1)
        kt_ref[HEAD_DIM:LANES, :] = jnp.where(rows == cols // L_SEL, 1.0, 0.0).astype(BF16)
        rt = 256
        for n in range(t // rt):
            x = kv_ref[n * rt:(n + 1) * rt, :]
            kt_ref[0:HEAD_DIM, n * rt:(n + 1) * rt] = x.T[2 * HEAD_DIM:3 * HEAD_DIM].astype(BF16)
            v_ref[n * rt:(n + 1) * rt, :] = x[:, 3 * HEAD_DIM:4 * HEAD_DIM].astype(BF16)
            w = win_ref[n * rt:(n + 1) * rt, :]
            kwt_ref[:, n * rt:(n + 1) * rt] = w.T[0:HEAD_DIM].astype(BF16)
            vw_ref[n * rt:(n + 1) * rt, :] = w[:, HEAD_DIM:2 * HEAD_DIM].astype(BF16)

    q = q_ref[...] * ATTN_SCALE
    q4 = jnp.concatenate([q[:, h * HEAD_DIM:(h + 1) * HEAD_DIM] for h in range(N_HEADS)], axis=0).astype(BF16)
    qpos = qs + lax.broadcasted_iota(jnp.int32, (c, 1), 0)
    qpos4 = jnp.concatenate([qpos] * N_HEADS, axis=0)

    wl = WINDOW + c
    ks = pl.multiple_of(jnp.maximum(qs - WINDOW, 0), c)
    rel = qpos - (ks + lax.broadcasted_iota(jnp.int32, (1, wl), 1))
    band = jnp.where((rel >= 0) & (rel <= WINDOW), 0.0, NEG)
    sw = _dot(q4, kwt_ref[:, pl.ds(ks, wl)]) + jnp.concatenate([band] * N_HEADS, axis=0)
    mw = jnp.broadcast_to(jnp.max(_lane_partial(sw, jnp.maximum), axis=-1, keepdims=True), (N_HEADS * c, LANES))
    ew = _exp_shifted(sw, mw)
    lw = jnp.sum(_lane_partial(ew, jnp.add), axis=-1, keepdims=True)
    ow_ref[...] = _dot(ew.astype(BF16), vw_ref[pl.ds(ks, wl), :]) * (1.0 / lw)

    nc = kc_ref.shape[0]
    s = _dot_nt(q4, kc_ref[...].astype(BF16))
    jj = lax.broadcasted_iota(jnp.int32, (1, nc), 1)
    p = _masked_softmax(s, (jj * CMP_STRIDE + (L_CMP - 1) <= qpos4) & (jj < n_cmp))
    o_cmp = _dot(p.astype(BF16), vc_ref[...].astype(BF16))
    psum = p[0:c] + p[c:2 * c] + p[2 * c:3 * c] + p[3 * c:4 * c]

    imp = _dot_nt(covert_ref[...], psum, precision=HIGHEST)
    blk = lax.broadcasted_iota(jnp.int32, (nsp, 1), 0)
    qpos_l = qs + lax.broadcasted_iota(jnp.int32, (1, c), 1)
    cur = qpos_l // L_SEL
    imp = jnp.where((blk == 0) | (blk == cur) | (blk == cur - 1), jnp.inf, imp)
    imp = jnp.where(blk * L_SEL > qpos_l, -jnp.inf, imp)
    rank = jnp.zeros((nsp, c), F32)
    for b in range(ns):
        rb = imp[b:b + 1, :]
        rank = rank + jnp.where((rb > imp) | ((rb == imp) & (blk > b)), 1.0, 0.0)
    sel = (rank < N_SEL) & (blk * L_SEL <= qpos_l)
    bias_t = jnp.where(sel, 0.0, NEG)
    bias = jnp.concatenate([bias_t, jnp.zeros((LANES - nsp, c), F32)], axis=0).T[:, 0:nsp]
    qa = jnp.concatenate([q4, jnp.concatenate([bias] * N_HEADS, axis=0).astype(BF16)], axis=-1)

    jt = qs // tk
    k_own = pl.multiple_of(jt * tk, tk)
    qa_ref[...] = qa
    kpos = k_own + lax.broadcasted_iota(jnp.int32, (1, tk), 1)
    causal = jnp.where(kpos <= qpos, 0.0, NEG)
    sc = _dot(qa, kt_ref[:, pl.ds(k_own, tk)]) + jnp.concatenate([causal] * N_HEADS, axis=0)
    so_ref[...] = sc
    m_ref[...] = _lane_partial(sc, jnp.maximum)

    def max_body(j, carry):
        k0 = pl.multiple_of(j * tk, tk)
        sc = _dot(qa_ref[...], kt_ref[:, pl.ds(k0, tk)])
        sp_ref[j] = sc
        m_ref[...] = jnp.maximum(m_ref[...], _lane_partial(sc, jnp.maximum))
        return carry

    lax.fori_loop(0, jt, max_body, 0)
    m = jnp.broadcast_to(jnp.max(m_ref[...], axis=-1, keepdims=True), m_ref.shape)
    m_ref[...] = m
    pe = _exp_shifted(so_ref[...], m)
    l_ref[...] = _lane_partial(pe, jnp.add)
    acc_ref[...] = _dot(pe.astype(BF16), v_ref[pl.ds(k_own, tk), :])

    def sum_body(j, carry):
        k0 = pl.multiple_of(j * tk, tk)
        pe = _exp_shifted(sp_ref[j], m_ref[...])
        l_ref[...] += _lane_partial(pe, jnp.add)
        acc_ref[...] += _dot(pe.astype(BF16), v_ref[pl.ds(k0, tk), :])
        return carry

    lax.fori_loop(0, jt, sum_body, 0)
    o_sel = acc_ref[...] * (1.0 / jnp.sum(l_ref[...], axis=-1, keepdims=True))

    o_win = ow_ref[...]
    g = _sigmoid(g_ref[...])

    def gate(br):
        return jnp.concatenate([g[:, br * N_HEADS + h:br * N_HEADS + h + 1] for h in range(N_HEADS)], axis=0)

    o = gate(0) * o_cmp + gate(1) * o_sel + gate(2) * o_win
    for h in range(N_HEADS):
        o_ref[:, h * HEAD_DIM:(h + 1) * HEAD_DIM] = o[h * c:(h + 1) * c].astype(o_ref.dtype)


def _nsa_prompt(z3, kvn3, win3, kc, vc):
    b, t, _ = z3.shape
    c = 128
    tk = 512
    nc = kc.shape[1]
    n_cmp = t // CMP_STRIDE - 1
    ns = t // L_SEL
    nsp = LANES - HEAD_DIM
    assert ns <= nsp and c == LANES
    covert = _cover_matrix(nc, n_cmp, nsp).T
    kern = functools.partial(_nsa_prompt_kernel, c=c, tk=tk, n_cmp=n_cmp, ns=ns)
    return pl.pallas_call(
        kern,
        grid=(b, t // c),
        in_specs=[
            pl.BlockSpec((None, c, GROUP_W), lambda bi, i: (bi, i, OFF_QNSA // GROUP_W)),
            pl.BlockSpec((None, c, LANES), lambda bi, i: (bi, i, OFF_GATE // LANES)),
            pl.BlockSpec((None, t, GROUP_W), lambda bi, i: (bi, 0, 0)),
            pl.BlockSpec((None, t, LANES), lambda bi, i: (bi, 0, 0)),
            pl.BlockSpec((None, nc, HEAD_DIM), lambda bi, i: (bi, 0, 0)),
            pl.BlockSpec((None, nc, HEAD_DIM), lambda bi, i: (bi, 0, 0)),
            pl.BlockSpec((nsp, nc), lambda bi, i: (0, 0)),
        ],
        out_specs=pl.BlockSpec((None, c, GROUP_W), lambda bi, i: (bi, i, 0)),
        out_shape=jax.ShapeDtypeStruct((b, t, GROUP_W), BF16),
        scratch_shapes=[pltpu.VMEM((LANES, t), BF16), pltpu.VMEM((t, HEAD_DIM), BF16),
                        pltpu.VMEM((HEAD_DIM, t), BF16), pltpu.VMEM((t, HEAD_DIM), BF16),
                        pltpu.VMEM((N_HEADS * c, LANES), BF16), pltpu.VMEM((N_HEADS * c, tk), F32),
                        pltpu.VMEM((t // tk - 1, N_HEADS * c, tk), F32),
                        pltpu.VMEM((N_HEADS * c, LANES), F32), pltpu.VMEM((N_HEADS * c, LANES), F32),
                        pltpu.VMEM((N_HEADS * c, HEAD_DIM), F32), pltpu.VMEM((N_HEADS * c, HEAD_DIM), F32)],
        compiler_params=_cparams(("parallel", "arbitrary")),
        name="nsa_prompt",
    )(z3, z3, kvn3, win3, kc, vc, covert)


def _moba_prompt_kernel(q_ref, kv_ref, o_ref, kmean_ref, kt_ref, v_ref, qa_ref, so_ref, sp_ref, m_ref, l_ref, acc_ref,
                        *, nb):
    i = pl.program_id(1)
    c = MOBA_BLOCK
    t = kv_ref.shape[0]

    @pl.when(i == 0)
    def _():
        rows = lax.broadcasted_iota(jnp.int32, (LANES - HEAD_DIM, t), 0)
        cols = lax.broadcasted_iota(jnp.int32, (LANES - HEAD_DIM, t), 1)
        onehot = jnp.where(rows == cols // c, 1.0, 0.0).astype(BF16)
        for h in range(N_HEADS):
            kt_ref[h, HEAD_DIM:LANES, :] = onehot
        kmean_ref[...] = jnp.zeros(kmean_ref.shape, F32)
        for n in range(nb):
            x = kv_ref[n * c:(n + 1) * c, :]
            kmean_ref[n:n + 1, :] = jnp.mean(x[:, 0:GROUP_W], axis=0, keepdims=True)
            xt = x[:, 0:GROUP_W].T
            for h in range(N_HEADS):
                lo, hi = h * HEAD_DIM, (h + 1) * HEAD_DIM
                kt_ref[h, 0:HEAD_DIM, n * c:(n + 1) * c] = xt[lo:hi].astype(BF16)
                v_ref[h, n * c:(n + 1) * c, :] = x[:, GROUP_W + lo:GROUP_W + hi].astype(BF16)

    q = q_ref[...]
    nbp = LANES - HEAD_DIM
    nbr = -(-nb // 8) * 8
    blk = lax.broadcasted_iota(jnp.int32, (nbr, 1), 0)
    fblk = blk.astype(F32)
    past = blk < i
    row = lax.broadcasted_iota(jnp.int32, (c, c), 0)
    col = lax.broadcasted_iota(jnp.int32, (c, c), 1)
    tri = col <= row
    own0 = pl.multiple_of(i * c, c)
    for h in range(N_HEADS):
        lo, hi = h * HEAD_DIM, (h + 1) * HEAD_DIM
        qh = q[:, lo:hi]
        work = jnp.where(past, _dot_nt(kmean_ref[0:nbr, lo:hi], qh, precision=HIGHEST), -jnp.inf)
        sel = jnp.zeros((nbr, c), jnp.bool_)
        for _ in range(MOBA_TOPK):
            mx = jnp.max(work, axis=0, keepdims=True)
            pick = jnp.min(jnp.where(work == mx, fblk, float(nbp)), axis=0, keepdims=True)
            hit = fblk == pick
            sel = sel | (hit & (mx > -jnp.inf))
            work = jnp.where(hit, -jnp.inf, work)
        bias_t = jnp.where(sel, 0.0, NEG)
        bias = jnp.concatenate([bias_t, jnp.full((LANES - nbr, c), NEG, F32)], axis=0).T[:, 0:nbp]
        qs = (qh * ATTN_SCALE).astype(BF16)
        qa_ref[h] = jnp.concatenate([qs, bias.astype(BF16)], axis=-1)
        sc = jnp.where(tri, _dot(qs, kt_ref[h, 0:HEAD_DIM, pl.ds(own0, c)]), NEG)
        so_ref[h] = sc
        m_ref[h] = _lane_partial(sc, jnp.maximum)

    def max_body(j, carry):
        k0 = pl.multiple_of(j * c, c)
        for h in range(N_HEADS):
            sc = _dot(qa_ref[h], kt_ref[h, :, pl.ds(k0, c)])
            sp_ref[h, j] = sc
            m_ref[h] = jnp.maximum(m_ref[h], _lane_partial(sc, jnp.maximum))
        return carry

    lax.fori_loop(0, i, max_body, 0)
    for h in range(N_HEADS):
        m = jnp.broadcast_to(jnp.max(m_ref[h], axis=-1, keepdims=True), (c, LANES))
        m_ref[h] = m
        pe = _exp_shifted(so_ref[h], m)
        l_ref[h] = _lane_partial(pe, jnp.add)
        acc_ref[h] = _dot(pe.astype(BF16), v_ref[h, pl.ds(own0, c), :])

    def sum_body(j, carry):
        k0 = pl.multiple_of(j * c, c)
        for h in range(N_HEADS):
            pe = _exp_shifted(sp_ref[h, j], m_ref[h])
            l_ref[h] += _lane_partial(pe, jnp.add)
            acc_ref[h] += _dot(pe.astype(BF16), v_ref[h, pl.ds(k0, c), :])
        return carry

    lax.fori_loop(0, i, sum_body, 0)
    for h in range(N_HEADS):
        l = jnp.sum(l_ref[h], axis=-1, keepdims=True)
        o_ref[:, h * HEAD_DIM:(h + 1) * HEAD_DIM] = (acc_ref[h] * (1.0 / l)).astype(o_ref.dtype)


def _moba_prompt(z3, kvm3):
    b, t, _ = z3.shape
    nb = t // MOBA_BLOCK
    assert nb <= LANES - HEAD_DIM
    kern = functools.partial(_moba_prompt_kernel, nb=nb)
    return pl.pallas_call(
        kern,
        grid=(b, nb),
        in_specs=[pl.BlockSpec((None, MOBA_BLOCK, GROUP_W), lambda bi, i: (bi, i, OFF_QMOBA // GROUP_W)),
                  pl.BlockSpec((None, t, 2 * GROUP_W), lambda bi, i: (bi, 0, 0))],
        out_specs=pl.BlockSpec((None, MOBA_BLOCK, GROUP_W), lambda bi, i: (bi, i, 0)),
        out_shape=jax.ShapeDtypeStruct((b, t, GROUP_W), BF16),
        scratch_shapes=[pltpu.VMEM((LANES - HEAD_DIM, GROUP_W), F32),
                        pltpu.VMEM((N_HEADS, LANES, t), BF16),
                        pltpu.VMEM((N_HEADS, t, HEAD_DIM), BF16),
                        pltpu.VMEM((N_HEADS, MOBA_BLOCK, LANES), BF16),
                        pltpu.VMEM((N_HEADS, MOBA_BLOCK, MOBA_BLOCK), F32),
                        pltpu.VMEM((N_HEADS, nb - 1, MOBA_BLOCK, MOBA_BLOCK), F32),
                        pltpu.VMEM((N_HEADS, MOBA_BLOCK, LANES), F32),
                        pltpu.VMEM((N_HEADS, MOBA_BLOCK, LANES), F32),
                        pltpu.VMEM((N_HEADS, MOBA_BLOCK, HEAD_DIM), F32)],
        compiler_params=_cparams(("parallel", "arbitrary")),
        name="moba_prompt",
    )(z3, kvm3)


def _conv_pool_sample_kernel(z_ref, cst_ref, pst_ref, dw_ref, dwb_ref, lg_ref, lb_ref, pw_ref, plw_ref, pls_ref,
                             yc_ref, yp_ref, cnew_ref, pnew_ref):
    u = z_ref[:, OFF_UCONV:OFF_UCONV + 2 * GROUP_W]
    zg = u[:, :GROUP_W] * _sigmoid(u[:, GROUP_W:])
    nst = CONV_W - 1
    y = dwb_ref[...] + dw_ref[nst:nst + 1, :] * zg
    for k in range(nst):
        y = y + dw_ref[k:k + 1, :] * cst_ref[k]
    y = _silu(_ln(y, lg_ref[...], lb_ref[...]))
    yc_ref[...] = _dot(y.astype(BF16), pw_ref[...])
    for k in range(nst - 1):
        cnew_ref[k] = cst_ref[k + 1]
    cnew_ref[nst - 1] = zg

    p = z_ref[:, OFF_UPOOL:OFF_UPOOL + GROUP_W]
    npst = POOL_MAX - 1
    lane = lax.broadcasted_iota(jnp.int32, (1, GROUP_W), 1)
    run = p
    mean = jnp.zeros_like(p)
    k = 1
    for g, w in enumerate(POOL_WINDOWS):
        while k < w:
            run = run + pst_ref[npst - k]
            k += 1
        mean = jnp.where((lane >= g * POOL_GW) & (lane < (g + 1) * POOL_GW), run / float(w), mean)
    d = mean - p
    yp_ref[...] = _dot(d.astype(BF16), plw_ref[...]) * pls_ref[...]
    for k in range(npst - 1):
        pnew_ref[k] = pst_ref[k + 1]
    pnew_ref[npst - 1] = p


def _conv_pool_sample(z, cst, pst, dw, dwb, lg, lb, pw, plw, pls):
    s = z.shape[0]
    return pl.pallas_call(
        _conv_pool_sample_kernel,
        out_shape=[jax.ShapeDtypeStruct((s, GROUP_W), F32), jax.ShapeDtypeStruct((s, GROUP_W), F32),
                   jax.ShapeDtypeStruct(cst.shape, F32), jax.ShapeDtypeStruct(pst.shape, F32)],
        compiler_params=pltpu.CompilerParams(vmem_limit_bytes=VMEM_LIMIT),
        name="conv_pool_sample",
    )(z, cst, pst, dw, dwb, lg, lb, pw, plw, pls)


_CMP_PAGES = 16
_CMP_GROUPS = 1


def _compress_pages_kernel(pt_ref, cache_ref, w_ref, c_ref, buf_ref, rows_ref, sem_ref, *, layer, n_steps, page):
    b = pl.program_id(0)
    s = pl.program_id(1)
    g = _CMP_PAGES
    step = b * n_steps + s
    total = pl.num_programs(0) * n_steps

    def copies(stp, slot):
        bb = stp // n_steps
        ss = stp % n_steps
        return [pltpu.make_async_copy(cache_ref.at[layer, pt_ref[bb, ss * g + k], pl.ds(0, 2)],
                                      buf_ref.at[slot, k], sem_ref.at[slot]) for k in range(g)]

    @pl.when(step == 0)
    def _():
        for cp in copies(step, 0):
            cp.start()

    slot = step % 2

    @pl.when(step + 1 < total)
    def _():
        for cp in copies(step + 1, 1 - slot):
            cp.start()

    pltpu.make_async_copy(cache_ref.at[layer, pl.ds(0, g), pl.ds(0, 2)], buf_ref.at[slot], sem_ref.at[slot]).wait()

    gp = g // _CMP_GROUPS
    nchunk = gp * page // CMP_STRIDE
    for grp in range(_CMP_GROUPS):
        rows = rows_ref.at[grp]
        for k in range(gp):
            x = buf_ref[slot, grp * gp + k].reshape(2 * HEAD_DIM, page)
            rows[pl.ds(k * page, page), :] = x.T
        acc = jnp.zeros((nchunk, 4 * HEAD_DIM), F32)
        for r in range(0, CMP_STRIDE, 2):
            xr = jnp.concatenate([rows[pl.ds(r, nchunk, stride=CMP_STRIDE), :],
                                  rows[pl.ds(r + 1, nchunk, stride=CMP_STRIDE), :]], axis=-1).astype(BF16)
            acc = acc + _dot(xr, w_ref[r // 2])
        c_ref[pl.ds(grp * nchunk, nchunk), :] = acc


def _compress_pages(page_table, cache_t, wstack, layer):
    s, npg = page_table.shape
    page = cache_t.shape[-1]
    g = _CMP_PAGES
    n_steps = npg // g
    nchunk = g * page // CMP_STRIDE
    kern = functools.partial(_compress_pages_kernel, layer=layer, n_steps=n_steps, page=page)
    return pl.pallas_call(
        kern,
        grid_spec=pltpu.PrefetchScalarGridSpec(
            num_scalar_prefetch=1,
            grid=(s, n_steps),
            in_specs=[pl.BlockSpec(memory_space=pl.ANY),
                      pl.BlockSpec(wstack.shape, lambda b, i, pt: (0, 0, 0))],
            out_specs=pl.BlockSpec((None, nchunk, 4 * HEAD_DIM), lambda b, i, pt: (b, i, 0)),
            scratch_shapes=[pltpu.VMEM((2, g, 2, HEAD_DIM, page), F32),
                            pltpu.VMEM((_CMP_GROUPS, g // _CMP_GROUPS * page, 2 * HEAD_DIM), F32),
                            pltpu.SemaphoreType.DMA((2,))]),
        out_shape=jax.ShapeDtypeStruct((s, n_steps * nchunk, 4 * HEAD_DIM), F32),
        compiler_params=_cparams(("arbitrary", "arbitrary")),
        name="compress_pages",
    )(page_table, cache_t, wstack)


def _stack_heads(q):
    rows = [q[:, h * HEAD_DIM:(h + 1) * HEAD_DIM] for h in range(N_HEADS)]
    return jnp.concatenate(rows + [jnp.zeros((8 - N_HEADS, HEAD_DIM), q.dtype)], axis=0)


def _nsa_select_kernel(c_ref, z_ref, pos_ref, w1_ref, w2_ref, cover_ref, ocmp_ref, idx_ref, psum_ref,
                       *, n_cmp, n_sel, qpos):
    bk, bv = _cmp_bias(pos_ref, w1_ref)
    kc, vc = _compress_tail(c_ref[...], bk, bv, w2_ref[0], w2_ref[1])
    q = z_ref[:, OFF_QNSA:OFF_QNSA + GROUP_W]
    q4 = _stack_heads(q)
    nc = kc.shape[0]
    s = _dot_nt(q4.astype(BF16), kc.astype(BF16)) * ATTN_SCALE
    jj = lax.broadcasted_iota(jnp.int32, (1, nc), 1)
    p = _masked_softmax(s, (jj * CMP_STRIDE + (L_CMP - 1) <= qpos) & (jj < n_cmp))
    ocmp_ref[...] = _dot(p.astype(BF16), vc.astype(BF16))[0:N_HEADS]
    b = pl.program_id(0)
    psum_ref[pl.ds(b, 1), :] = jnp.sum(p[0:N_HEADS], axis=0, keepdims=True)

    @pl.when(b == pl.num_programs(0) - 1)
    def _():
        imp = _dot(psum_ref[...], cover_ref[...], precision=HIGHEST)
        ns, nsp = imp.shape
        blk = lax.broadcasted_iota(jnp.int32, (1, nsp), 1)
        fblk = blk.astype(F32)
        cur = qpos // L_SEL
        imp = jnp.where((blk == 0) | (blk == cur) | (blk == cur - 1), jnp.inf, imp)
        imp = jnp.where(blk * L_SEL > qpos, -jnp.inf, imp)
        work = jnp.where(blk < n_sel, jnp.maximum(imp, -3e38), -jnp.inf)
        out_lane = lax.broadcasted_iota(jnp.int32, (1, LANES), 1)
        idx = jnp.zeros((ns, LANES), jnp.int32)
        for t in range(N_SEL):
            mx = jnp.max(work, axis=-1, keepdims=True)
            pick = jnp.min(jnp.where(work == mx, fblk, float(nsp)), axis=-1, keepdims=True)
            idx = jnp.where(out_lane == t, pick.astype(jnp.int32), idx)
            work = jnp.where(fblk == pick, -jnp.inf, work)
        idx_ref[...] = idx


def _nsa_select(c, z3, pos, w1, w2, qpos):
    s, nc, _ = c.shape
    n_cmp = nc - 1
    n_sel = qpos // L_SEL + 1
    nsp = -(-n_sel // LANES) * LANES
    cover = _cover_matrix(nc, n_cmp, nsp)
    kern = functools.partial(_nsa_select_kernel, n_cmp=n_cmp, n_sel=n_sel, qpos=qpos)
    cst = lambda shape: pl.BlockSpec(shape, lambda b: (0,) * len(shape))
    return pl.pallas_call(
        kern,
        grid=(s,),
        in_specs=[pl.BlockSpec((None, nc, 4 * HEAD_DIM), lambda b: (b, 0, 0)),
                  pl.BlockSpec((None, 1, N_IN_PAD), lambda b: (b, 0, 0)),
                  cst(pos.shape), cst(w1.shape), cst(w2.shape), cst(cover.shape)],
        out_specs=[pl.BlockSpec((None, N_HEADS, HEAD_DIM), lambda b: (b, 0, 0)),
                   pl.BlockSpec((s, LANES), lambda b: (0, 0))],
        out_shape=[jax.ShapeDtypeStruct((s, N_HEADS, HEAD_DIM), F32), jax.ShapeDtypeStruct((s, LANES), jnp.int32)],
        scratch_shapes=[pltpu.VMEM((s, nc), F32)],
        compiler_params=_cparams(("arbitrary",)),
        name="nsa_select",
    )(c, z3, pos, w1, w2, cover)


def _col_from_row(row):
    n = row.shape[1]
    eye = lax.broadcasted_iota(jnp.int32, (n, n), 0) == lax.broadcasted_iota(jnp.int32, (n, n), 1)
    return jnp.sum(jnp.where(eye, row, 0.0), axis=-1, keepdims=True)


def _nsa_sample_kernel(pt_ref, idx_ref, cache_ref, z_ref, ocmp_ref, win_ref, o_ref, wnew_ref, buf_ref, sem_ref,
                       *, layer, page, qpos):
    b = pl.program_id(0)
    npg = pt_ref.shape[1]
    per_page = page // L_SEL

    slot = b % 2

    def copy(bb, t, sl):
        pg = jnp.minimum(idx_ref[bb, t] // per_page, npg - 1)
        return pltpu.make_async_copy(cache_ref.at[layer, pt_ref[bb, pg], pl.ds(2, 2)], buf_ref.at[sl, t],
                                     sem_ref.at[sl])

    @pl.when(b == 0)
    def _():
        for t in range(N_SEL):
            copy(b, t, 0).start()

    @pl.when(b + 1 < pl.num_programs(0))
    def _():
        for t in range(N_SEL):
            copy(b + 1, t, 1 - slot).start()

    z = z_ref[...]
    q = z[:, OFF_QNSA:OFF_QNSA + GROUP_W]
    q4b = _stack_heads(q).astype(BF16)
    kvn = z[:, OFF_KVNSA:OFF_KVNSA + GROUP_W]
    ks_new, vs_new = kvn[:, 2 * HEAD_DIM:3 * HEAD_DIM], kvn[:, 3 * HEAD_DIM:4 * HEAD_DIM]
    kvw = z[:, OFF_KVWIN:OFF_KVWIN + 2 * HEAD_DIM]
    kw_new, vw_new = kvw[:, 0:HEAD_DIM], kvw[:, HEAD_DIM:2 * HEAD_DIM]
    q4r = q4b.astype(F32)

    def new_score(k_new):
        kb = k_new.astype(BF16).astype(F32)
        return jnp.sum(q4r * kb, axis=-1, keepdims=True) * ATTN_SCALE

    wk = win_ref[0]
    wv = win_ref[1]
    sw = _dot(q4b, wk.astype(BF16)) * ATTN_SCALE
    sw_new = new_score(kw_new)
    mw = jnp.maximum(jnp.max(sw, axis=-1, keepdims=True), sw_new)
    ew = jnp.exp(sw - mw)
    ew_new = jnp.exp(sw_new - mw)
    dw = jnp.sum(ew, axis=-1, keepdims=True) + ew_new
    pw = ew * (1.0 / dw)
    pw_new = ew_new * (1.0 / dw)
    o_win = _dot_nt(pw.astype(BF16), wv.astype(BF16)) \
        + pw_new.astype(BF16).astype(F32) * vw_new.astype(BF16).astype(F32)
    wl = wk.shape[1]
    lane = lax.broadcasted_iota(jnp.int32, (1, wl), 1)
    wnew_ref[0] = jnp.where(lane == wl - 1, _col_from_row(kw_new), pltpu.roll(wk, wl - 1, 1))
    wnew_ref[1] = jnp.where(lane == wl - 1, _col_from_row(vw_new), pltpu.roll(wv, wl - 1, 1))

    pltpu.make_async_copy(cache_ref.at[layer, pl.ds(0, N_SEL), pl.ds(2, 2)], buf_ref.at[slot],
                          sem_ref.at[slot]).wait()

    n_past = (qpos // L_SEL)
    plane = lax.broadcasted_iota(jnp.int32, (1, page), 1)
    scores = []
    new_taken = jnp.int32(0)
    for t in range(N_SEL):
        bid = idx_ref[b, t]
        in_past = bid < n_past
        half = bid % per_page
        valid = (plane >= half * L_SEL) & (plane < (half + 1) * L_SEL) & in_past
        sc = _dot(q4b, buf_ref[slot, t, 0].astype(BF16)) * ATTN_SCALE
        scores.append(jnp.where(valid, sc, -jnp.inf))
        new_taken = new_taken + (bid == n_past).astype(jnp.int32)
    has_new = new_taken > 0
    ss_new = jnp.where(has_new, new_score(ks_new), -jnp.inf)
    ms = ss_new
    for sc in scores:
        ms = jnp.maximum(ms, jnp.max(sc, axis=-1, keepdims=True))
    ms = jnp.where(ms == -jnp.inf, 0.0, ms)
    es_new = jnp.exp(ss_new - ms)
    ds = es_new
    es = []
    for sc in scores:
        e = jnp.exp(sc - ms)
        es.append(e)
        ds = ds + jnp.sum(e, axis=-1, keepdims=True)
    inv = 1.0 / jnp.where(ds > 0, ds, 1.0)
    o_sel = (es_new * inv).astype(BF16).astype(F32) * vs_new.astype(BF16).astype(F32)
    for t in range(N_SEL):
        o_sel = o_sel + _dot_nt((es[t] * inv).astype(BF16), buf_ref[slot, t, 1].astype(BF16))

    g = _sigmoid(z[:, OFF_GATE:OFF_GATE + LANES])
    o_cmp = ocmp_ref[...]
    for h in range(N_HEADS):
        oh = (g[:, h:h + 1] * o_cmp[h:h + 1] + g[:, N_HEADS + h:N_HEADS + h + 1] * o_sel[h:h + 1]
              + g[:, 2 * N_HEADS + h:2 * N_HEADS + h + 1] * o_win[h:h + 1])
        o_ref[:, h * HEAD_DIM:(h + 1) * HEAD_DIM] = oh


def _nsa_sample(page_table, idx, cache_t, z3, ocmp, win_t, layer, qpos):
    s = z3.shape[0]
    page = cache_t.shape[-1]
    wl = win_t.shape[-1]
    kern = functools.partial(_nsa_sample_kernel, layer=layer, page=page, qpos=qpos)
    return pl.pallas_call(
        kern,
        grid_spec=pltpu.PrefetchScalarGridSpec(
            num_scalar_prefetch=2,
            grid=(s,),
            in_specs=[pl.BlockSpec(memory_space=pl.ANY),
                      pl.BlockSpec((None, 1, N_IN_PAD), lambda b, pt, ix: (b, 0, 0)),
                      pl.BlockSpec((None, N_HEADS, HEAD_DIM), lambda b, pt, ix: (b, 0, 0)),
                      pl.BlockSpec((None, None, 2, HEAD_DIM, wl), lambda b, pt, ix: (layer, b, 0, 0, 0))],
            out_specs=[pl.BlockSpec((None, 1, GROUP_W), lambda b, pt, ix: (b, 0, 0)),
                       pl.BlockSpec((None, 2, HEAD_DIM, wl), lambda b, pt, ix: (b, 0, 0, 0))],
            scratch_shapes=[pltpu.VMEM((2, N_SEL, 2, HEAD_DIM, page), F32), pltpu.SemaphoreType.DMA((2,))]),
        out_shape=[jax.ShapeDtypeStruct((s, 1, GROUP_W), F32), jax.ShapeDtypeStruct((s, 2, HEAD_DIM, wl), F32)],
        compiler_params=_cparams(("arbitrary",)),
        name="nsa_sample",
    )(page_table, idx, cache_t, z3, ocmp, win_t)


_MOBA_PAGES = 16


def _moba_gate_kernel(pt_ref, cache_ref, qcol_ref, idx_ref, buf_ref, part_ref, sem_ref,
                      *, layer, n_steps, page, nb, cur):
    b = pl.program_id(0)
    s = pl.program_id(1)
    g = _MOBA_PAGES
    step = b * n_steps + s
    total = pl.num_programs(0) * n_steps
    ppb = MOBA_BLOCK // page
    bps = g // ppb

    def copies(stp, slot):
        bb = stp // n_steps
        ss = stp % n_steps
        return [pltpu.make_async_copy(cache_ref.at[layer, pt_ref[bb, ss * g + k], 0],
                                      buf_ref.at[slot, k], sem_ref.at[slot]) for k in range(g)]

    @pl.when(step == 0)
    def _():
        for cp in copies(step, 0):
            cp.start()

    slot = step % 2

    @pl.when(step + 1 < total)
    def _():
        for cp in copies(step + 1, 1 - slot):
            cp.start()

    pltpu.make_async_copy(cache_ref.at[layer, pl.ds(0, g), 0], buf_ref.at[slot], sem_ref.at[slot]).wait()

    for j in range(bps):
        for h in range(N_HEADS):
            qh = qcol_ref[h * HEAD_DIM:(h + 1) * HEAD_DIM, :]
            acc = buf_ref[slot, j * ppb, h] * qh
            for k in range(1, ppb):
                acc = acc + buf_ref[slot, j * ppb + k, h] * qh
            part_ref[h * nb + s * bps + j] = jnp.sum(acc.reshape(HEAD_DIM // 8, 8, page), axis=0)

    @pl.when(s == n_steps - 1)
    def _():
        ones = jnp.ones((8, page), F32)
        part = jnp.sum(part_ref[...], axis=1)
        gate = _dot_nt(ones, part, precision=HIGHEST)[0:1] * (1.0 / MOBA_BLOCK)
        lane = lax.broadcasted_iota(jnp.int32, gate.shape, 1)
        out_lane = lax.broadcasted_iota(jnp.int32, (1, LANES), 1)
        idx = jnp.zeros((1, LANES), jnp.int32)
        for h in range(N_HEADS):
            n_of = lane - h * nb
            f_of = n_of.astype(F32)
            inh = (n_of >= 0) & (n_of < nb) & (n_of < cur)
            work = jnp.where(inh, jnp.maximum(gate, -3e38), -jnp.inf)
            for t in range(MOBA_TOPK):
                mx = jnp.max(work, axis=-1, keepdims=True)
                pick = jnp.min(jnp.where((work == mx) & inh, f_of, float(4 * nb)), axis=-1, keepdims=True)
                idx = jnp.where(out_lane == h * MOBA_TOPK + t, pick.astype(jnp.int32), idx)
                work = jnp.where(f_of == pick, -jnp.inf, work)
        idx_ref[...] = idx


def _moba_gate(page_table, cache_t, qcol, layer, cur):
    s, npg = page_table.shape
    page = cache_t.shape[-1]
    g = _MOBA_PAGES
    n_steps = npg // g
    nb = npg * page // MOBA_BLOCK
    kern = functools.partial(_moba_gate_kernel, layer=layer, n_steps=n_steps, page=page, nb=nb, cur=cur)
    return pl.pallas_call(
        kern,
        grid_spec=pltpu.PrefetchScalarGridSpec(
            num_scalar_prefetch=1,
            grid=(s, n_steps),
            in_specs=[pl.BlockSpec(memory_space=pl.ANY),
                      pl.BlockSpec((None, N_HEADS * HEAD_DIM, page), lambda b, i, pt: (b, 0, 0))],
            out_specs=pl.BlockSpec((None, 1, LANES), lambda b, i, pt: (b, 0, 0)),
            scratch_shapes=[pltpu.VMEM((2, g, N_HEADS, HEAD_DIM, page), F32),
                            pltpu.VMEM((N_HEADS * nb, 8, page), F32),
                            pltpu.SemaphoreType.DMA((2,))]),
        out_shape=jax.ShapeDtypeStruct((s, 1, LANES), jnp.int32),
        compiler_params=_cparams(("arbitrary", "arbitrary")),
        name="moba_gate",
    )(page_table, cache_t, qcol)


def _moba_sample_kernel(pt_ref, idx_ref, cache_ref, z_ref, o_ref, buf_ref, sem_ref, *, layer, page, cur):
    b = pl.program_id(0)
    npg = pt_ref.shape[1]
    ppb = MOBA_BLOCK // page
    n_src = MOBA_TOPK * ppb

    slot = b % 2

    def copy(bb, sl, h, t, k, kv):
        blk = jnp.minimum(idx_ref[bb, h * MOBA_TOPK + t], npg // ppb - 1)
        return pltpu.make_async_copy(cache_ref.at[layer, pt_ref[bb, blk * ppb + k], kv, h],
                                     buf_ref.at[sl, t * ppb + k, kv, h], sem_ref.at[sl])

    every = [(h, t, k, kv) for h in range(N_HEADS) for t in range(MOBA_TOPK) for k in range(ppb) for kv in range(2)]

    @pl.when(b == 0)
    def _():
        for a in every:
            copy(b, 0, *a).start()

    @pl.when(b + 1 < pl.num_programs(0))
    def _():
        for a in every:
            copy(b + 1, 1 - slot, *a).start()

    z = z_ref[...]
    q = z[:, OFF_QMOBA:OFF_QMOBA + GROUP_W]
    k_new = z[:, OFF_KVMOBA:OFF_KVMOBA + GROUP_W]
    v_new = z[:, OFF_KVMOBA + GROUP_W:OFF_KVMOBA + 2 * GROUP_W]
    pltpu.make_async_copy(cache_ref.at[layer, pl.ds(0, n_src)], buf_ref.at[slot], sem_ref.at[slot]).wait()

    for h in range(N_HEADS):
        lo, hi = h * HEAD_DIM, (h + 1) * HEAD_DIM
        qh = jnp.concatenate([q[:, lo:hi]] * 8, axis=0).astype(BF16)
        qr = q[:, lo:hi].astype(BF16).astype(F32)
        s_new = jnp.sum(qr * k_new[:, lo:hi].astype(BF16).astype(F32), axis=-1, keepdims=True) * ATTN_SCALE
        scores = []
        m = s_new
        for t in range(MOBA_TOPK):
            ok = idx_ref[b, h * MOBA_TOPK + t] < cur
            for k in range(ppb):
                sc = _dot(qh, buf_ref[slot, t * ppb + k, 0, h].astype(BF16))[0:1] * ATTN_SCALE
                sc = jnp.where(ok, sc, -jnp.inf)
                scores.append(sc)
                m = jnp.maximum(m, jnp.max(sc, axis=-1, keepdims=True))
        e_new = jnp.exp(s_new - m)
        d = e_new
        es = []
        for sc in scores:
            e = jnp.exp(sc - m)
            es.append(e)
            d = d + jnp.sum(e, axis=-1, keepdims=True)
        inv = 1.0 / d
        o = (e_new * inv).astype(BF16).astype(F32) * v_new[:, lo:hi].astype(BF16).astype(F32)
        for n in range(n_src):
            pn = jnp.concatenate([es[n] * inv] * 8, axis=0).astype(BF16)
            o = o + _dot_nt(pn, buf_ref[slot, n, 1, h].astype(BF16))[0:1]
        o_ref[:, lo:hi] = o


def _moba_sample(page_table, idx, cache_t, z3, layer, cur):
    s = z3.shape[0]
    page = cache_t.shape[-1]
    ppb = MOBA_BLOCK // page
    kern = functools.partial(_moba_sample_kernel, layer=layer, page=page, cur=cur)
    return pl.pallas_call(
        kern,
        grid_spec=pltpu.PrefetchScalarGridSpec(
            num_scalar_prefetch=2,
            grid=(s,),
            in_specs=[pl.BlockSpec(memory_space=pl.ANY),
                      pl.BlockSpec((None, 1, N_IN_PAD), lambda b, pt, ix: (b, 0, 0))],
            out_specs=pl.BlockSpec((None, 1, GROUP_W), lambda b, pt, ix: (b, 0, 0)),
            scratch_shapes=[pltpu.VMEM((2, MOBA_TOPK * ppb, 2, N_HEADS, HEAD_DIM, page), F32),
                            pltpu.SemaphoreType.DMA((2,))]),
        out_shape=jax.ShapeDtypeStruct((s, 1, GROUP_W), F32),
        compiler_params=_cparams(("arbitrary",)),
        name="moba_sample",
    )(page_table, idx, cache_t, z3)


def _cmp_weight_stack(w1):
    w = w1.reshape(2, 2, CMP_STRIDE, HEAD_DIM, HEAD_DIM)
    zero = jnp.zeros((CMP_STRIDE, HEAD_DIM, 2 * HEAD_DIM), F32)
    top = jnp.concatenate([w[0, 0], w[0, 1], zero], axis=-1)
    bot = jnp.concatenate([zero, w[1, 0], w[1, 1]], axis=-1)
    per_r = jnp.concatenate([top, bot], axis=1)
    return per_r.reshape(CMP_STRIDE // 2, 4 * HEAD_DIM, 4 * HEAD_DIM).astype(BF16)


def kernel(x_prompt, x_sample, cache_moba_kv, cache_nsa_kv, state_nsa_win, state_conv, state_pool, page_table,
           w_in, conv_dw, conv_dw_b, conv_ln_g, conv_ln_b, conv_pw, nsa_cmp_pos, nsa_cmp_w1, nsa_cmp_w2,
           pool_w, pool_scale, w_out, ln1_g, ln1_b, w_up, w_down, ln2_g, ln2_b):
    bp, t, _ = x_prompt.shape
    bs = x_sample.shape[0]
    depth = w_in.shape[0]
    page = cache_nsa_kv.shape[2]
    past_len = page_table.shape[1] * page
    win_len = state_nsa_win.shape[2]
    assert x_sample.shape[1] == 1 and win_len == WINDOW and t % 512 == 0 and t >= WINDOW + 128
    assert page % L_SEL == 0 and MOBA_BLOCK % page == 0 and past_len % MOBA_BLOCK == 0
    assert past_len // MOBA_BLOCK >= MOBA_TOPK and past_len // L_SEL + 1 >= N_SEL and past_len >= POOL_MAX
    assert page_table.shape[1] % _CMP_PAGES == 0 and page_table.shape[1] % _MOBA_PAGES == 0

    nsa_t = jnp.transpose(cache_nsa_kv, (0, 1, 3, 4, 2))
    moba_t = jnp.transpose(cache_moba_kv, (0, 1, 3, 4, 5, 2))
    win_t = jnp.transpose(state_nsa_win, (0, 1, 3, 4, 2))
    conv_t = jnp.transpose(state_conv, (0, 2, 1, 3))
    pool_t = jnp.transpose(state_pool, (0, 2, 1, 3))

    row = lambda v: v.reshape(1, -1)
    yp = x_prompt.reshape(bp * t, D_MODEL)
    ys = x_sample.reshape(bs, D_MODEL)
    st_p, st_s = [], []
    w_out_b, w_up_b, w_down_b = w_out.astype(BF16), w_up.astype(BF16), w_down.astype(BF16)
    for l in range(depth):
        w_in_l = _in_weights(w_in[l])
        pw = conv_pw[l].astype(BF16)
        plw = jax.scipy.linalg.block_diag(*[pool_w[l, g] for g in range(len(POOL_WINDOWS))]).astype(BF16)
        w1 = nsa_cmp_w1[l].astype(BF16)
        w2 = nsa_cmp_w2[l].astype(BF16)
        pos = jnp.broadcast_to(nsa_cmp_pos[l].reshape(2, 1, L_CMP * HEAD_DIM), (2, 8, L_CMP * HEAD_DIM))
        small = (conv_dw[l], row(conv_dw_b[l]), row(conv_ln_g[l]), row(conv_ln_b[l]), pw, plw, row(pool_scale[l]))

        z, kvm, kvn, kvw = _in_proj_split(yp, w_in_l)
        z3 = z.reshape(bp, t, N_MAIN)
        kvm3 = kvm.reshape(bp, t, 2 * GROUP_W)
        kvn3 = kvn.reshape(bp, t, GROUP_W)
        win3 = kvw.reshape(bp, t, 2 * HEAD_DIM)
        y_conv, y_pool, conv_new32 = _conv_pool_prompt(z3, *small)
        nchunk = t // CMP_STRIDE
        chunks_k = kvn3[:, :, 0:HEAD_DIM].reshape(bp, nchunk, CMP_STRIDE * HEAD_DIM)
        chunks_v = kvn3[:, :, HEAD_DIM:2 * HEAD_DIM].reshape(bp, nchunk, CMP_STRIDE * HEAD_DIM)
        kc, vc = _compress_prompt(chunks_k, chunks_v, pos, w1, w2)
        o_nsa = _nsa_prompt(z3, kvn3, win3, kc, vc)
        o_moba = _moba_prompt(z3, kvm3)
        flat = lambda a: a.reshape(bp * t, GROUP_W)
        x1 = _out_proj((flat(y_conv), flat(o_nsa), flat(o_moba), flat(y_pool)), yp, w_out_b, l,
                       row(ln1_g[l]), row(ln1_b[l]))
        yp = _ffn(x1, w_up_b, w_down_b, l, row(ln2_g[l]), row(ln2_b[l]))
        moba_rows = kvm3.reshape(bp, t, 2, N_HEADS, HEAD_DIM)
        nsa_rows = kvn3.reshape(bp, t, 4, HEAD_DIM)
        win_new = win3[:, t - win_len:].reshape(bp, win_len, 2, HEAD_DIM)
        conv_new = conv_new32[:, _CONV_HALO - (CONV_W - 1):]
        pool_new = z3[:, t - (POOL_MAX - 1):, OFF_UPOOL:OFF_UPOOL + GROUP_W]
        st_p.append((moba_rows, nsa_rows, win_new, conv_new, pool_new))

        zs = _in_proj(ys, w_in_l)
        zs3 = zs.reshape(bs, 1, N_IN_PAD)
        ys_conv, ys_pool, conv_new_t, pool_new_t = _conv_pool_sample(zs, conv_t[l], pool_t[l], *small)
        c = _compress_pages(page_table, nsa_t, _cmp_weight_stack(nsa_cmp_w1[l]), l)
        o_cmp, sel_idx = _nsa_select(c, zs3, pos, w1, w2, past_len)
        os_nsa, win_new_t = _nsa_sample(page_table, sel_idx, nsa_t, zs3, o_cmp, win_t, l, past_len)
        q_m = zs[:, OFF_QMOBA:OFF_QMOBA + GROUP_W]
        qcol = jnp.broadcast_to(q_m[:, :, None], (bs, GROUP_W, page))
        top_idx = _moba_gate(page_table, moba_t, qcol, l, past_len // MOBA_BLOCK)
        os_moba = _moba_sample(page_table, top_idx.reshape(bs, LANES), moba_t, zs3, l, past_len // MOBA_BLOCK)
        x1s = _out_proj((ys_conv, os_nsa.reshape(bs, GROUP_W), os_moba.reshape(bs, GROUP_W), ys_pool), ys, w_out_b, l,
                        row(ln1_g[l]), row(ln1_b[l]))
        ys = _ffn(x1s, w_up_b, w_down_b, l, row(ln2_g[l]), row(ln2_b[l]))
        st_s.append((zs[:, OFF_KVMOBA:OFF_KVMOBA + 2 * GROUP_W].reshape(bs, 1, 2, N_HEADS, HEAD_DIM),
                     zs[:, OFF_KVNSA:OFF_KVNSA + GROUP_W].reshape(bs, 1, 4, HEAD_DIM),
                     jnp.transpose(win_new_t, (0, 3, 1, 2)),
                     jnp.transpose(conv_new_t, (1, 0, 2)),
                     jnp.transpose(pool_new_t, (1, 0, 2))))

    stk = lambda sts, i: jnp.stack([s[i] for s in sts], axis=0)
    return (yp.reshape(bp, t, D_MODEL), ys.reshape(bs, 1, D_MODEL),
            stk(st_p, 0), stk(st_s, 0), stk(st_p, 1), stk(st_s, 1), stk(st_p, 2), stk(st_s, 2),
            stk(st_p, 3), stk(st_s, 3), stk(st_p, 4), stk(st_s, 4))
```

```python
import functools

import numpy as np
import jax
import jax.numpy as jnp
from jax import lax
from jax.experimental import pallas as pl
from jax.experimental.pallas import tpu as pltpu

F32 = jnp.float32
BF16 = jnp.bfloat16
HIGHEST = lax.Precision.HIGHEST

D_MODEL = 1024
HEAD_DIM = 64
GROUP_W = 256
N_HEADS = 4
CONV_W = 31
L_CMP = 32
CMP_STRIDE = 16
L_SEL = 64
N_SEL = 16
WINDOW = 512
MOBA_BLOCK = 256
MOBA_TOPK = 3
POOL_WINDOWS = (2, 4, 8, 16)
POOL_GW = 64
POOL_MAX = 16
D_FF = 4096
DEPTH = 2
ALPHA = (2 * DEPTH) ** 0.25
LN_EPS = 1e-5
ATTN_SCALE = HEAD_DIM ** -0.5
NEG = -1e30

OFF_UCONV, OFF_QNSA, OFF_QMOBA, OFF_UPOOL, OFF_GATE = 0, 512, 768, 1024, 1280
N_MAIN = 1408
OFF_KVMOBA, OFF_KVNSA, OFF_KVWIN = 1408, 1920, 2176
N_IN_PAD = 2304
N_GATE = 12
LANES = 128
VMEM_LIMIT = 56 * 1024 * 1024


def _in_weights(w):
    main = np.concatenate([np.arange(0, 512), np.arange(512, 768), np.arange(1164, 1420), np.arange(1932, 2188),
                           np.arange(1152, 1164)])
    kv = np.concatenate([np.arange(1420, 1932), np.arange(768, 1024), np.arange(1024, 1152)])
    pad = jnp.zeros((w.shape[0], LANES - N_GATE), w.dtype)
    return jnp.concatenate([w[:, main], pad, w[:, kv]], axis=1).astype(BF16)


def _ln(x, g, b):
    mu = jnp.mean(x, axis=-1, keepdims=True)
    xc = x - mu
    var = jnp.mean(xc * xc, axis=-1, keepdims=True)
    return xc * lax.rsqrt(var + LN_EPS) * g + b


def _sigmoid(x):
    return 1.0 / (1.0 + jnp.exp(-x))


def _silu(x):
    return x * _sigmoid(x)


def _dot(a, b, **kw):
    return jnp.dot(a, b, preferred_element_type=F32, **kw)


def _dot_nt(a, b, **kw):
    return lax.dot_general(a, b, (((1,), (1,)), ((), ())), preferred_element_type=F32, **kw)


def _masked_softmax(s, mask):
    r = s.shape[0]
    s = jnp.where(mask, s, -jnp.inf)
    m = jnp.max(_lane_partial(s, jnp.maximum), axis=-1, keepdims=True)
    m = jnp.broadcast_to(jnp.where(m == -jnp.inf, 0.0, m), (r, LANES))
    e = _exp_shifted(s, m)
    d = jnp.sum(_lane_partial(e, jnp.add), axis=-1, keepdims=True)
    inv = jnp.broadcast_to(1.0 / jnp.where(d > 0, d, 1.0), (r, LANES))
    return jnp.concatenate([e[:, g * LANES:(g + 1) * LANES] * inv for g in range(s.shape[1] // LANES)], axis=-1)


def _lane_partial(s, op):
    out = s[:, 0:LANES]
    for g in range(1, s.shape[1] // LANES):
        out = op(out, s[:, g * LANES:(g + 1) * LANES])
    return out


def _exp_shifted(s, m):
    return jnp.concatenate([jnp.exp(s[:, g * LANES:(g + 1) * LANES] - m) for g in range(s.shape[1] // LANES)], axis=-1)


def _cparams(sem, vmem=None):
    return pltpu.CompilerParams(dimension_semantics=sem, vmem_limit_bytes=vmem or VMEM_LIMIT)


def _in_proj_kernel(x_ref, w_ref, o_ref):
    o_ref[...] = _dot(x_ref[...].astype(BF16), w_ref[...])


def _in_proj(x, w):
    m, k = x.shape
    n = w.shape[1]
    tm = min(m, 512)
    return pl.pallas_call(
        _in_proj_kernel,
        grid=(m // tm,),
        in_specs=[pl.BlockSpec((tm, k), lambda i: (i, 0)), pl.BlockSpec((k, n), lambda i: (0, 0))],
        out_specs=pl.BlockSpec((tm, n), lambda i: (i, 0)),
        out_shape=jax.ShapeDtypeStruct((m, n), F32),
        compiler_params=_cparams(("parallel",)),
        name="in_proj",
    )(x, w)


_IN_PARTS = ((0, N_MAIN), (OFF_KVMOBA, OFF_KVNSA), (OFF_KVNSA, OFF_KVWIN), (OFF_KVWIN, N_IN_PAD))


def _in_proj_split_kernel(x_ref, w_ref, *o_refs):
    z = _dot(x_ref[...].astype(BF16), w_ref[...])
    for (lo, hi), o_ref in zip(_IN_PARTS, o_refs):
        o_ref[...] = z[:, lo:hi]


def _in_proj_split(x, w):
    m, k = x.shape
    tm = min(m, 512)
    return pl.pallas_call(
        _in_proj_split_kernel,
        grid=(m // tm,),
        in_specs=[pl.BlockSpec((tm, k), lambda i: (i, 0)), pl.BlockSpec(w.shape, lambda i: (0, 0))],
        out_specs=[pl.BlockSpec((tm, hi - lo), lambda i: (i, 0)) for lo, hi in _IN_PARTS],
        out_shape=[jax.ShapeDtypeStruct((m, hi - lo), F32) for lo, hi in _IN_PARTS],
        compiler_params=_cparams(("parallel",)),
        name="in_proj_split",
    )(x, w)


def _out_proj_kernel(a_ref, b_ref, c_ref, d_ref, x_ref, w_ref, g_ref, bt_ref, o_ref):
    mix = jnp.concatenate([a_ref[...], b_ref[...], c_ref[...], d_ref[...]], axis=-1).astype(BF16)
    y = _dot(mix, w_ref[...])
    o_ref[...] = _ln(ALPHA * x_ref[...] + y, g_ref[...], bt_ref[...])


def _out_proj(pieces, x, w, layer, g, b):
    m = x.shape[0]
    tm = min(m, 512)
    pspec = pl.BlockSpec((tm, GROUP_W), lambda i: (i, 0))
    return pl.pallas_call(
        _out_proj_kernel,
        grid=(m // tm,),
        in_specs=[pspec, pspec, pspec, pspec,
                  pl.BlockSpec((tm, D_MODEL), lambda i: (i, 0)),
                  pl.BlockSpec((None, D_MODEL, D_MODEL), lambda i: (layer, 0, 0)),
                  pl.BlockSpec((1, D_MODEL), lambda i: (0, 0)),
                  pl.BlockSpec((1, D_MODEL), lambda i: (0, 0))],
        out_specs=pl.BlockSpec((tm, D_MODEL), lambda i: (i, 0)),
        out_shape=jax.ShapeDtypeStruct((m, D_MODEL), F32),
        compiler_params=_cparams(("parallel",)),
        name="out_proj_ln",
    )(*pieces, x, w, g, b)


def _ffn_kernel(x_ref, wu_ref, wd_ref, g_ref, b_ref, o_ref, acc_ref):
    j = pl.program_id(1)

    @pl.when(j == 0)
    def _():
        acc_ref[...] = jnp.zeros_like(acc_ref)

    h = jnp.maximum(_dot(x_ref[...].astype(BF16), wu_ref[...]), 0.0)
    acc_ref[...] += _dot((h * h).astype(BF16), wd_ref[...])

    @pl.when(j == pl.num_programs(1) - 1)
    def _():
        o_ref[...] = _ln(ALPHA * x_ref[...] + acc_ref[...], g_ref[...], b_ref[...])


def _ffn(x, wu, wd, layer, g, b):
    m = x.shape[0]
    tm = min(m, 1024)
    tf = 1024
    return pl.pallas_call(
        _ffn_kernel,
        grid=(m // tm, D_FF // tf),
        in_specs=[pl.BlockSpec((tm, D_MODEL), lambda i, j: (i, 0)),
                  pl.BlockSpec((None, D_MODEL, tf), lambda i, j: (layer, 0, j)),
                  pl.BlockSpec((None, tf, D_MODEL), lambda i, j: (layer, j, 0)),
                  pl.BlockSpec((1, D_MODEL), lambda i, j: (0, 0)),
                  pl.BlockSpec((1, D_MODEL), lambda i, j: (0, 0))],
        out_specs=pl.BlockSpec((tm, D_MODEL), lambda i, j: (i, 0)),
        out_shape=jax.ShapeDtypeStruct((m, D_MODEL), F32),
        scratch_shapes=[pltpu.VMEM((tm, D_MODEL), F32)],
        compiler_params=_cparams(("parallel", "arbitrary")),
        name="ffn_ln",
    )(x, wu, wd, g, b)


_CONV_HALO = 32
_POOL_HALO = 16


def _pool_groups(rows_ref, buf_ref, tt, cnt_fn):
    n = _POOL_HALO + tt
    lane = lax.broadcasted_iota(jnp.int32, (1, GROUP_W), 1)
    cur = rows_ref[pl.ds(_POOL_HALO, tt), :]
    out = jnp.zeros((tt, GROUP_W), F32)
    shift = 1
    for g, w in enumerate(POOL_WINDOWS):
        src = rows_ref if g == 0 else buf_ref.at[(g - 1) % 2]
        dst = buf_ref.at[g % 2]
        lo = 2 * shift - 1
        dst[pl.ds(lo, n - lo), :] = src[pl.ds(lo, n - lo), :] + src[pl.ds(lo - shift, n - lo), :]
        shift *= 2
        s = dst[pl.ds(_POOL_HALO, tt), :]
        mean = s / cnt_fn(w)
        out = jnp.where((lane >= g * POOL_GW) & (lane < (g + 1) * POOL_GW), mean, out)
    return out - cur


def _conv_pool_prompt_kernel(u_ref, uh_ref, p_ref, ph_ref, dw_ref, dwb_ref, lg_ref, lb_ref, pw_ref,
                             plw_ref, pls_ref, yc_ref, yp_ref, cnew_ref, zc_ref, res_ref, rows_ref, buf_ref, *, tt):
    i = pl.program_id(1)
    first = i == 0
    u = u_ref[...]
    zg = u[:, :GROUP_W] * _sigmoid(u[:, GROUP_W:])
    uh = uh_ref[...]
    zh = uh[:, :GROUP_W] * _sigmoid(uh[:, GROUP_W:])
    zc_ref[pl.ds(0, _CONV_HALO), :] = jnp.where(first, 0.0, zh)
    zc_ref[pl.ds(_CONV_HALO, tt), :] = zg
    zc_ref[pl.ds(_CONV_HALO + tt, 8), :] = jnp.zeros((8, GROUP_W), F32)
    off = _CONV_HALO - (CONV_W - 1)
    y = jnp.zeros((tt, GROUP_W), F32) + dwb_ref[...]
    for r in range(8):
        acc = None
        for a in range(-(-(off + CONV_W) // 8)):
            k = 8 * a + r - off
            if 0 <= k < CONV_W:
                term = dw_ref[k:k + 1, :] * zc_ref[pl.ds(8 * a, tt + 8), :]
                acc = term if acc is None else acc + term
        if r == 0:
            y = y + acc[0:tt]
        else:
            res_ref[r - 1] = acc
            y = y + res_ref[r - 1, pl.ds(r, tt), :]
    y = _silu(_ln(y, lg_ref[...], lb_ref[...]))
    yc_ref[...] = _dot(y.astype(BF16), pw_ref[...]).astype(yc_ref.dtype)

    @pl.when(i == pl.num_programs(1) - 1)
    def _():
        cnew_ref[...] = zc_ref[pl.ds(tt, _CONV_HALO), :]

    rows_ref[pl.ds(0, _POOL_HALO), :] = jnp.where(first, 0.0, ph_ref[...])
    rows_ref[pl.ds(_POOL_HALO, tt), :] = p_ref[...]
    qpos1 = i * tt + lax.broadcasted_iota(jnp.int32, (tt, 1), 0) + 1
    d = _pool_groups(rows_ref, buf_ref, tt, lambda w: jnp.minimum(w, qpos1).astype(F32))
    yp_ref[...] = (_dot(d.astype(BF16), plw_ref[...]) * pls_ref[...]).astype(yp_ref.dtype)


def _conv_pool_prompt(z3, dw, dwb, lg, lb, pw, plw, pls):
    b, t, _ = z3.shape
    tt = 512
    nt = t // tt
    kern = functools.partial(_conv_pool_prompt_kernel, tt=tt)
    cst = lambda shape: pl.BlockSpec(shape, lambda bi, i: (0,) * len(shape))
    return pl.pallas_call(
        kern,
        grid=(b, nt),
        in_specs=[
            pl.BlockSpec((None, tt, 512), lambda bi, i: (bi, i, OFF_UCONV // 512)),
            pl.BlockSpec((None, _CONV_HALO, 512),
                         lambda bi, i: (bi, jnp.maximum(i * (tt // _CONV_HALO) - 1, 0), OFF_UCONV // 512)),
            pl.BlockSpec((None, tt, GROUP_W), lambda bi, i: (bi, i, OFF_UPOOL // GROUP_W)),
            pl.BlockSpec((None, _POOL_HALO, GROUP_W),
                         lambda bi, i: (bi, jnp.maximum(i * (tt // _POOL_HALO) - 1, 0), OFF_UPOOL // GROUP_W)),
            cst((CONV_W, GROUP_W)), cst((1, GROUP_W)), cst((1, GROUP_W)), cst((1, GROUP_W)),
            cst((GROUP_W, GROUP_W)), cst((GROUP_W, GROUP_W)), cst((1, GROUP_W)),
        ],
        out_specs=[
            pl.BlockSpec((None, tt, GROUP_W), lambda bi, i: (bi, i, 0)),
            pl.BlockSpec((None, tt, GROUP_W), lambda bi, i: (bi, i, 0)),
            pl.BlockSpec((None, _CONV_HALO, GROUP_W), lambda bi, i: (bi, 0, 0)),
        ],
        out_shape=[jax.ShapeDtypeStruct((b, t, GROUP_W), BF16), jax.ShapeDtypeStruct((b, t, GROUP_W), BF16),
                   jax.ShapeDtypeStruct((b, _CONV_HALO, GROUP_W), F32)],
        scratch_shapes=[pltpu.VMEM((_CONV_HALO + tt + 8, GROUP_W), F32),
                        pltpu.VMEM((7, tt + 8, GROUP_W), F32),
                        pltpu.VMEM((_POOL_HALO + tt, GROUP_W), F32),
                        pltpu.VMEM((2, _POOL_HALO + tt, GROUP_W), F32)],
        compiler_params=_cparams(("parallel", "arbitrary")),
        name="conv_pool_prompt",
    )(z3, z3, z3, z3, dw, dwb, lg, lb, pw, plw, pls)


def _compress_tail(c, bias_k, bias_v, w2k, w2v):
    n = c.shape[0]

    def one(c0, c1, bias, w2):
        hid = c0 + pltpu.roll(c1, n - 1, 0) + bias
        return _dot(_silu(hid).astype(BF16), w2)

    kc = one(c[:, 0:64], c[:, 64:128], bias_k, w2k)
    vc = one(c[:, 128:192], c[:, 192:256], bias_v, w2v)
    return kc, vc


def _cmp_bias(pos_ref, w1_ref):
    bk = _dot(pos_ref[0].astype(BF16), w1_ref[0])[0:1]
    bv = _dot(pos_ref[1].astype(BF16), w1_ref[1])[0:1]
    return bk, bv


def _compress_prompt_kernel(ck_ref, cv_ref, pos_ref, w1_ref, w2_ref, kc_ref, vc_ref):
    bk, bv = _cmp_bias(pos_ref, w1_ref)
    half = CMP_STRIDE * HEAD_DIM
    ck = ck_ref[...].astype(BF16)
    cv = cv_ref[...].astype(BF16)
    c = jnp.concatenate([_dot(ck, w1_ref[0, 0:half, :]), _dot(ck, w1_ref[0, half:2 * half, :]),
                         _dot(cv, w1_ref[1, 0:half, :]), _dot(cv, w1_ref[1, half:2 * half, :])], axis=-1)
    kc, vc = _compress_tail(c, bk, bv, w2_ref[0], w2_ref[1])
    kc_ref[...] = kc
    vc_ref[...] = vc


def _compress_prompt(chunks_k, chunks_v, pos, w1, w2):
    b, nc, kk = chunks_k.shape
    cst = lambda shape: pl.BlockSpec(shape, lambda bi: (0,) * len(shape))
    return pl.pallas_call(
        _compress_prompt_kernel,
        grid=(b,),
        in_specs=[pl.BlockSpec((None, nc, kk), lambda bi: (bi, 0, 0)),
                  pl.BlockSpec((None, nc, kk), lambda bi: (bi, 0, 0)),
                  cst(pos.shape), cst(w1.shape), cst(w2.shape)],
        out_specs=[pl.BlockSpec((None, nc, HEAD_DIM), lambda bi: (bi, 0, 0)),
                   pl.BlockSpec((None, nc, HEAD_DIM), lambda bi: (bi, 0, 0))],
        out_shape=[jax.ShapeDtypeStruct((b, nc, HEAD_DIM), F32)] * 2,
        compiler_params=_cparams(("parallel",)),
        name="compress_prompt",
    )(chunks_k, chunks_v, pos, w1, w2)


def _rank_desc(v, n):
    lane = lax.broadcasted_iota(jnp.int32, v.shape, 1)
    rank = jnp.zeros(v.shape, F32)
    for j in range(n):
        col = v[:, j:j + 1]
        rank = rank + jnp.where((col > v) | ((col == v) & (lane > j)), 1.0, 0.0)
    return rank


def _cover_matrix(n_cmp_pad, n_cmp, n_sel_pad):
    j = np.arange(n_cmp_pad)[:, None]
    i = np.arange(n_sel_pad)[None, :]
    cov = (j * CMP_STRIDE < (i + 1) * L_SEL) & (j * CMP_STRIDE + L_CMP > i * L_SEL) & (j < n_cmp)
    return jnp.asarray(cov.astype(np.float32))


def _nsa_prompt_kernel(q_ref, g_ref, kv_ref, win_ref, kc_ref, vc_ref, covert_ref, o_ref,
                       kt_ref, v_ref, kwt_ref, vw_ref, qa_ref, so_ref, sp_ref, m_ref, l_ref, acc_ref, ow_ref,
                       *, c, tk, n_cmp, ns):
    i = pl.program_id(1)
    qs = i * c
    t = kv_ref.shape[0]
    nsp = LANES - HEAD_DIM

    @pl.when(i == 0)
    def _():
        rows = lax.broadcasted_iota(jnp.int32, (nsp, t), 0)
        cols = lax.broadcasted_iota(jnp.int32, (nsp, t), 1)
        kt_ref[HEAD_DIM:LANES, :] = jnp.where(rows == cols // L_SEL, 1.0, 0.0).astype(BF16)
        rt = 256
        for n in range(t // rt):
            x = kv_ref[n * rt:(n + 1) * rt, :]
            kt_ref[0:HEAD_DIM, n * rt:(n + 1) * rt] = x.T[2 * HEAD_DIM:3 * HEAD_DIM].astype(BF16)
            v_ref[n * rt:(n + 1) * rt, :] = x[:, 3 * HEAD_DIM:4 * HEAD_DIM].astype(BF16)
            w = win_ref[n * rt:(n + 1) * rt, :]
            kwt_ref[:, n * rt:(n + 1) * rt] = w.T[0:HEAD_DIM].astype(BF16)
            vw_ref[n * rt:(n + 1) * rt, :] = w[:, HEAD_DIM:2 * HEAD_DIM].astype(BF16)

    q = q_ref[...] * ATTN_SCALE
    q4 = jnp.concatenate([q[:, h * HEAD_DIM:(h + 1) * HEAD_DIM] for h in range(N_HEADS)], axis=0).astype(BF16)
    qpos = qs + lax.broadcasted_iota(jnp.int32, (c, 1), 0)
    qpos4 = jnp.concatenate([qpos] * N_HEADS, axis=0)

    wl = WINDOW + c
    ks = pl.multiple_of(jnp.maximum(qs - WINDOW, 0), c)
    rel = qpos - (ks + lax.broadcasted_iota(jnp.int32, (1, wl), 1))
    band = jnp.where((rel >= 0) & (rel <= WINDOW), 0.0, NEG)
    sw = _dot(q4, kwt_ref[:, pl.ds(ks, wl)]) + jnp.concatenate([band] * N_HEADS, axis=0)
    mw = jnp.broadcast_to(jnp.max(_lane_partial(sw, jnp.maximum), axis=-1, keepdims=True), (N_HEADS * c, LANES))
    ew = _exp_shifted(sw, mw)
    lw = jnp.sum(_lane_partial(ew, jnp.add), axis=-1, keepdims=True)
    ow_ref[...] = _dot(ew.astype(BF16), vw_ref[pl.ds(ks, wl), :]) * (1.0 / lw)

    nc = kc_ref.shape[0]
    s = _dot_nt(q4, kc_ref[...].astype(BF16))
    jj = lax.broadcasted_iota(jnp.int32, (1, nc), 1)
    p = _masked_softmax(s, (jj * CMP_STRIDE + (L_CMP - 1) <= qpos4) & (jj < n_cmp))
    o_cmp = _dot(p.astype(BF16), vc_ref[...].astype(BF16))
    psum = p[0:c] + p[c:2 * c] + p[2 * c:3 * c] + p[3 * c:4 * c]

    imp = _dot_nt(covert_ref[...], psum, precision=HIGHEST)
    blk = lax.broadcasted_iota(jnp.int32, (nsp, 1), 0)
    qpos_l = qs + lax.broadcasted_iota(jnp.int32, (1, c), 1)
    cur = qpos_l // L_SEL
    imp = jnp.where((blk == 0) | (blk == cur) | (blk == cur - 1), jnp.inf, imp)
    imp = jnp.where(blk * L_SEL > qpos_l, -jnp.inf, imp)
    rank = jnp.zeros((nsp, c), F32)
    for b in range(ns):
        rb = imp[b:b + 1, :]
        rank = rank + jnp.where((rb > imp) | ((rb == imp) & (blk > b)), 1.0, 0.0)
    sel = (rank < N_SEL) & (blk * L_SEL <= qpos_l)
    bias_t = jnp.where(sel, 0.0, NEG)
    bias = jnp.concatenate([bias_t, jnp.zeros((LANES - nsp, c), F32)], axis=0).T[:, 0:nsp]
    qa = jnp.concatenate([q4, jnp.concatenate([bias] * N_HEADS, axis=0).astype(BF16)], axis=-1)

    jt = qs // tk
    k_own = pl.multiple_of(jt * tk, tk)
    qa_ref[...] = qa
    kpos = k_own + lax.broadcasted_iota(jnp.int32, (1, tk), 1)
    causal = jnp.where(kpos <= qpos, 0.0, NEG)
    sc = _dot(qa, kt_ref[:, pl.ds(k_own, tk)]) + jnp.concatenate([causal] * N_HEADS, axis=0)
    so_ref[...] = sc
    m_ref[...] = _lane_partial(sc, jnp.maximum)

    def max_body(j, carry):
        k0 = pl.multiple_of(j * tk, tk)
        sc = _dot(qa_ref[...], kt_ref[:, pl.ds(k0, tk)])
        sp_ref[j] = sc
        m_ref[...] = jnp.maximum(m_ref[...], _lane_partial(sc, jnp.maximum))
        return carry

    lax.fori_loop(0, jt, max_body, 0)
    m = jnp.broadcast_to(jnp.max(m_ref[...], axis=-1, keepdims=True), m_ref.shape)
    m_ref[...] = m
    pe = _exp_shifted(so_ref[...], m)
    l_ref[...] = _lane_partial(pe, jnp.add)
    acc_ref[...] = _dot(pe.astype(BF16), v_ref[pl.ds(k_own, tk), :])

    def sum_body(j, carry):
        k0 = pl.multiple_of(j * tk, tk)
        pe = _exp_shifted(sp_ref[j], m_ref[...])
        l_ref[...] += _lane_partial(pe, jnp.add)
        acc_ref[...] += _dot(pe.astype(BF16), v_ref[pl.ds(k0, tk), :])
        return carry

    lax.fori_loop(0, jt, sum_body, 0)
    o_sel = acc_ref[...] * (1.0 / jnp.sum(l_ref[...], axis=-1, keepdims=True))

    o_win = ow_ref[...]
    g = _sigmoid(g_ref[...])

    def gate(br):
        return jnp.concatenate([g[:, br * N_HEADS + h:br * N_HEADS + h + 1] for h in range(N_HEADS)], axis=0)

    o = gate(0) * o_cmp + gate(1) * o_sel + gate(2) * o_win
    for h in range(N_HEADS):
        o_ref[:, h * HEAD_DIM:(h + 1) * HEAD_DIM] = o[h * c:(h + 1) * c].astype(o_ref.dtype)


def _nsa_prompt(z3, kvn3, win3, kc, vc):
    b, t, _ = z3.shape
    c = 128
    tk = 512
    nc = kc.shape[1]
    n_cmp = t // CMP_STRIDE - 1
    ns = t // L_SEL
    nsp = LANES - HEAD_DIM
    assert ns <= nsp and c == LANES
    covert = _cover_matrix(nc, n_cmp, nsp).T
    kern = functools.partial(_nsa_prompt_kernel, c=c, tk=tk, n_cmp=n_cmp, ns=ns)
    return pl.pallas_call(
        kern,
        grid=(b, t // c),
        in_specs=[
            pl.BlockSpec((None, c, GROUP_W), lambda bi, i: (bi, i, OFF_QNSA // GROUP_W)),
            pl.BlockSpec((None, c, LANES), lambda bi, i: (bi, i, OFF_GATE // LANES)),
            pl.BlockSpec((None, t, GROUP_W), lambda bi, i: (bi, 0, 0)),
            pl.BlockSpec((None, t, LANES), lambda bi, i: (bi, 0, 0)),
            pl.BlockSpec((None, nc, HEAD_DIM), lambda bi, i: (bi, 0, 0)),
            pl.BlockSpec((None, nc, HEAD_DIM), lambda bi, i: (bi, 0, 0)),
            pl.BlockSpec((nsp, nc), lambda bi, i: (0, 0)),
        ],
        out_specs=pl.BlockSpec((None, c, GROUP_W), lambda bi, i: (bi, i, 0)),
        out_shape=jax.ShapeDtypeStruct((b, t, GROUP_W), BF16),
        scratch_shapes=[pltpu.VMEM((LANES, t), BF16), pltpu.VMEM((t, HEAD_DIM), BF16),
                        pltpu.VMEM((HEAD_DIM, t), BF16), pltpu.VMEM((t, HEAD_DIM), BF16),
                        pltpu.VMEM((N_HEADS * c, LANES), BF16), pltpu.VMEM((N_HEADS * c, tk), F32),
                        pltpu.VMEM((t // tk - 1, N_HEADS * c, tk), F32),
                        pltpu.VMEM((N_HEADS * c, LANES), F32), pltpu.VMEM((N_HEADS * c, LANES), F32),
                        pltpu.VMEM((N_HEADS * c, HEAD_DIM), F32), pltpu.VMEM((N_HEADS * c, HEAD_DIM), F32)],
        compiler_params=_cparams(("parallel", "arbitrary")),
        name="nsa_prompt",
    )(z3, z3, kvn3, win3, kc, vc, covert)


def _moba_prompt_kernel(q_ref, kv_ref, o_ref, kmean_ref, kt_ref, v_ref, qa_ref, so_ref, sp_ref, m_ref, l_ref, acc_ref,
                        *, nb):
    i = pl.program_id(1)
    c = MOBA_BLOCK
    t = kv_ref.shape[0]

    @pl.when(i == 0)
    def _():
        rows = lax.broadcasted_iota(jnp.int32, (LANES - HEAD_DIM, t), 0)
        cols = lax.broadcasted_iota(jnp.int32, (LANES - HEAD_DIM, t), 1)
        onehot = jnp.where(rows == cols // c, 1.0, 0.0).astype(BF16)
        for h in range(N_HEADS):
            kt_ref[h, HEAD_DIM:LANES, :] = onehot
        kmean_ref[...] = jnp.zeros(kmean_ref.shape, F32)
        for n in range(nb):
            x = kv_ref[n * c:(n + 1) * c, :]
            kmean_ref[n:n + 1, :] = jnp.mean(x[:, 0:GROUP_W], axis=0, keepdims=True)
            xt = x[:, 0:GROUP_W].T
            for h in range(N_HEADS):
                lo, hi = h * HEAD_DIM, (h + 1) * HEAD_DIM
                kt_ref[h, 0:HEAD_DIM, n * c:(n + 1) * c] = xt[lo:hi].astype(BF16)
                v_ref[h, n * c:(n + 1) * c, :] = x[:, GROUP_W + lo:GROUP_W + hi].astype(BF16)

    q = q_ref[...]
    nbp = LANES - HEAD_DIM
    nbr = -(-nb // 8) * 8
    blk = lax.broadcasted_iota(jnp.int32, (nbr, 1), 0)
    fblk = blk.astype(F32)
    past = blk < i
    row = lax.broadcasted_iota(jnp.int32, (c, c), 0)
    col = lax.broadcasted_iota(jnp.int32, (c, c), 1)
    tri = col <= row
    own0 = pl.multiple_of(i * c, c)
    for h in range(N_HEADS):
        lo, hi = h * HEAD_DIM, (h + 1) * HEAD_DIM
        qh = q[:, lo:hi]
        work = jnp.where(past, _dot_nt(kmean_ref[0:nbr, lo:hi], qh, precision=HIGHEST), -jnp.inf)
        sel = jnp.zeros((nbr, c), jnp.bool_)
        for _ in range(MOBA_TOPK):
            mx = jnp.max(work, axis=0, keepdims=True)
            pick = jnp.min(jnp.where(work == mx, fblk, float(nbp)), axis=0, keepdims=True)
            hit = fblk == pick
            sel = sel | (hit & (mx > -jnp.inf))
            work = jnp.where(hit, -jnp.inf, work)
        bias_t = jnp.where(sel, 0.0, NEG)
        bias = jnp.concatenate([bias_t, jnp.full((LANES - nbr, c), NEG, F32)], axis=0).T[:, 0:nbp]
        qs = (qh * ATTN_SCALE).astype(BF16)
        qa_ref[h] = jnp.concatenate([qs, bias.astype(BF16)], axis=-1)
        sc = jnp.where(tri, _dot(qs, kt_ref[h, 0:HEAD_DIM, pl.ds(own0, c)]), NEG)
        so_ref[h] = sc
        m_ref[h] = _lane_partial(sc, jnp.maximum)

    def max_body(j, carry):
        k0 = pl.multiple_of(j * c, c)
        for h in range(N_HEADS):
            sc = _dot(qa_ref[h], kt_ref[h, :, pl.ds(k0, c)])
            sp_ref[h, j] = sc
            m_ref[h] = jnp.maximum(m_ref[h], _lane_partial(sc, jnp.maximum))
        return carry

    lax.fori_loop(0, i, max_body, 0)
    for h in range(N_HEADS):
        m = jnp.broadcast_to(jnp.max(m_ref[h], axis=-1, keepdims=True), (c, LANES))
        m_ref[h] = m
        pe = _exp_shifted(so_ref[h], m)
        l_ref[h] = _lane_partial(pe, jnp.add)
        acc_ref[h] = _dot(pe.astype(BF16), v_ref[h, pl.ds(own0, c), :])

    def sum_body(j, carry):
        k0 = pl.multiple_of(j * c, c)
        for h in range(N_HEADS):
            pe = _exp_shifted(sp_ref[h, j], m_ref[h])
            l_ref[h] += _lane_partial(pe, jnp.add)
            acc_ref[h] += _dot(pe.astype(BF16), v_ref[h, pl.ds(k0, c), :])
        return carry

    lax.fori_loop(0, i, sum_body, 0)
    for h in range(N_HEADS):
        l = jnp.sum(l_ref[h], axis=-1, keepdims=True)
        o_ref[:, h * HEAD_DIM:(h + 1) * HEAD_DIM] = (acc_ref[h] * (1.0 / l)).astype(o_ref.dtype)


def _moba_prompt(z3, kvm3):
    b, t, _ = z3.shape
    nb = t // MOBA_BLOCK
    assert nb <= LANES - HEAD_DIM
    kern = functools.partial(_moba_prompt_kernel, nb=nb)
    return pl.pallas_call(
        kern,
        grid=(b, nb),
        in_specs=[pl.BlockSpec((None, MOBA_BLOCK, GROUP_W), lambda bi, i: (bi, i, OFF_QMOBA // GROUP_W)),
                  pl.BlockSpec((None, t, 2 * GROUP_W), lambda bi, i: (bi, 0, 0))],
        out_specs=pl.BlockSpec((None, MOBA_BLOCK, GROUP_W), lambda bi, i: (bi, i, 0)),
        out_shape=jax.ShapeDtypeStruct((b, t, GROUP_W), BF16),
        scratch_shapes=[pltpu.VMEM((LANES - HEAD_DIM, GROUP_W), F32),
                        pltpu.VMEM((N_HEADS, LANES, t), BF16),
                        pltpu.VMEM((N_HEADS, t, HEAD_DIM), BF16),
                        pltpu.VMEM((N_HEADS, MOBA_BLOCK, LANES), BF16),
                        pltpu.VMEM((N_HEADS, MOBA_BLOCK, MOBA_BLOCK), F32),
                        pltpu.VMEM((N_HEADS, nb - 1, MOBA_BLOCK, MOBA_BLOCK), F32),
                        pltpu.VMEM((N_HEADS, MOBA_BLOCK, LANES), F32),
                        pltpu.VMEM((N_HEADS, MOBA_BLOCK, LANES), F32),
                        pltpu.VMEM((N_HEADS, MOBA_BLOCK, HEAD_DIM), F32)],
        compiler_params=_cparams(("parallel", "arbitrary")),
        name="moba_prompt",
    )(z3, kvm3)


def _conv_pool_sample_kernel(z_ref, cst_ref, pst_ref, dw_ref, dwb_ref, lg_ref, lb_ref, pw_ref, plw_ref, pls_ref,
                             yc_ref, yp_ref, cnew_ref, pnew_ref):
    u = z_ref[:, OFF_UCONV:OFF_UCONV + 2 * GROUP_W]
    zg = u[:, :GROUP_W] * _sigmoid(u[:, GROUP_W:])
    nst = CONV_W - 1
    y = dwb_ref[...] + dw_ref[nst:nst + 1, :] * zg
    for k in range(nst):
        y = y + dw_ref[k:k + 1, :] * cst_ref[k]
    y = _silu(_ln(y, lg_ref[...], lb_ref[...]))
    yc_ref[...] = _dot(y.astype(BF16), pw_ref[...])
    for k in range(nst - 1):
        cnew_ref[k] = cst_ref[k + 1]
    cnew_ref[nst - 1] = zg

    p = z_ref[:, OFF_UPOOL:OFF_UPOOL + GROUP_W]
    npst = POOL_MAX - 1
    lane = lax.broadcasted_iota(jnp.int32, (1, GROUP_W), 1)
    run = p
    mean = jnp.zeros_like(p)
    k = 1
    for g, w in enumerate(POOL_WINDOWS):
        while k < w:
            run = run + pst_ref[npst - k]
            k += 1
        mean = jnp.where((lane >= g * POOL_GW) & (lane < (g + 1) * POOL_GW), run / float(w), mean)
    d = mean - p
    yp_ref[...] = _dot(d.astype(BF16), plw_ref[...]) * pls_ref[...]
    for k in range(npst - 1):
        pnew_ref[k] = pst_ref[k + 1]
    pnew_ref[npst - 1] = p


def _conv_pool_sample(z, cst, pst, dw, dwb, lg, lb, pw, plw, pls):
    s = z.shape[0]
    return pl.pallas_call(
        _conv_pool_sample_kernel,
        out_shape=[jax.ShapeDtypeStruct((s, GROUP_W), F32), jax.ShapeDtypeStruct((s, GROUP_W), F32),
                   jax.ShapeDtypeStruct(cst.shape, F32), jax.ShapeDtypeStruct(pst.shape, F32)],
        compiler_params=pltpu.CompilerParams(vmem_limit_bytes=VMEM_LIMIT),
        name="conv_pool_sample",
    )(z, cst, pst, dw, dwb, lg, lb, pw, plw, pls)


_STREAM_SLOTS = 3
_CMP_PAGES = 16


def _compress_pages_kernel(pt_ref, cache_ref, w_ref, perm_ref, c_ref, buf_ref, rows_ref, sem_ref,
                           *, layer, n_steps, page):
    b = pl.program_id(0)
    s = pl.program_id(1)
    g = _CMP_PAGES
    step = b * n_steps + s
    total = pl.num_programs(0) * n_steps

    def copies(stp, slot):
        bb = stp // n_steps
        ss = stp % n_steps
        return [pltpu.make_async_copy(cache_ref.at[layer, pt_ref[bb, ss * g + k], pl.ds(0, 2)],
                                      buf_ref.at[slot, k], sem_ref.at[slot]) for k in range(g)]

    ahead = _STREAM_SLOTS - 1

    @pl.when(step == 0)
    def _():
        for d in range(ahead):
            @pl.when(d < total)
            def _():
                for cp in copies(d, d):
                    cp.start()

    slot = step % _STREAM_SLOTS

    @pl.when(step + ahead < total)
    def _():
        for cp in copies(step + ahead, (step + ahead) % _STREAM_SLOTS):
            cp.start()

    pltpu.make_async_copy(cache_ref.at[layer, pl.ds(0, g), pl.ds(0, 2)], buf_ref.at[slot], sem_ref.at[slot]).wait()

    nchunk = g * page // CMP_STRIDE
    for k in range(g):
        x = buf_ref[slot, k].reshape(2 * HEAD_DIM, page).astype(BF16)
        rows_ref[k] = _dot(x, perm_ref[...]).T.reshape(CMP_STRIDE, page // CMP_STRIDE, 2 * HEAD_DIM)
    acc = jnp.zeros((nchunk, 4 * HEAD_DIM), F32)
    for r in range(0, CMP_STRIDE, 2):
        xr = jnp.concatenate([rows_ref[:, r].reshape(nchunk, 2 * HEAD_DIM),
                              rows_ref[:, r + 1].reshape(nchunk, 2 * HEAD_DIM)], axis=-1).astype(BF16)
        acc = acc + _dot(xr, w_ref[r // 2])
    c_ref[...] = acc


def _row_perm(page):
    j = np.arange(page)
    src = CMP_STRIDE * (j % (page // CMP_STRIDE)) + j // (page // CMP_STRIDE)
    return jnp.asarray((np.arange(page)[:, None] == src[None, :]).astype(np.float32)).astype(BF16)


def _compress_pages(page_table, cache_t, wstack, layer):
    s, npg = page_table.shape
    page = cache_t.shape[-1]
    g = _CMP_PAGES
    n_steps = npg // g
    nchunk = g * page // CMP_STRIDE
    kern = functools.partial(_compress_pages_kernel, layer=layer, n_steps=n_steps, page=page)
    return pl.pallas_call(
        kern,
        grid_spec=pltpu.PrefetchScalarGridSpec(
            num_scalar_prefetch=1,
            grid=(s, n_steps),
            in_specs=[pl.BlockSpec(memory_space=pl.ANY),
                      pl.BlockSpec(wstack.shape, lambda b, i, pt: (0, 0, 0)),
                      pl.BlockSpec((page, page), lambda b, i, pt: (0, 0))],
            out_specs=pl.BlockSpec((None, nchunk, 4 * HEAD_DIM), lambda b, i, pt: (b, i, 0)),
            scratch_shapes=[pltpu.VMEM((_STREAM_SLOTS, g, 2, HEAD_DIM, page), F32),
                            pltpu.VMEM((g, CMP_STRIDE, page // CMP_STRIDE, 2 * HEAD_DIM), F32),
                            pltpu.SemaphoreType.DMA((_STREAM_SLOTS,))]),
        out_shape=jax.ShapeDtypeStruct((s, n_steps * nchunk, 4 * HEAD_DIM), F32),
        compiler_params=_cparams(("arbitrary", "arbitrary")),
        name="compress_pages",
    )(page_table, cache_t, wstack, _row_perm(page))


def _stack_heads(q):
    rows = [q[:, h * HEAD_DIM:(h + 1) * HEAD_DIM] for h in range(N_HEADS)]
    return jnp.concatenate(rows + [jnp.zeros((8 - N_HEADS, HEAD_DIM), q.dtype)], axis=0)


def _nsa_select_kernel(c_ref, z_ref, pos_ref, w1_ref, w2_ref, cover_ref, ocmp_ref, idx_ref, psum_ref,
                       *, n_cmp, n_sel, qpos):
    bk, bv = _cmp_bias(pos_ref, w1_ref)
    kc, vc = _compress_tail(c_ref[...], bk, bv, w2_ref[0], w2_ref[1])
    q = z_ref[:, OFF_QNSA:OFF_QNSA + GROUP_W]
    q4 = _stack_heads(q)
    nc = kc.shape[0]
    s = _dot_nt(q4.astype(BF16), kc.astype(BF16)) * ATTN_SCALE
    jj = lax.broadcasted_iota(jnp.int32, (1, nc), 1)
    p = _masked_softmax(s, (jj * CMP_STRIDE + (L_CMP - 1) <= qpos) & (jj < n_cmp))
    ocmp_ref[...] = _dot(p.astype(BF16), vc.astype(BF16))[0:N_HEADS]
    b = pl.program_id(0)
    psum_ref[pl.ds(b, 1), :] = jnp.sum(p[0:N_HEADS], axis=0, keepdims=True)

    @pl.when(b == pl.num_programs(0) - 1)
    def _():
        imp = _dot(psum_ref[...], cover_ref[...], precision=HIGHEST)
        ns, nsp = imp.shape
        blk = lax.broadcasted_iota(jnp.int32, (1, nsp), 1)
        fblk = blk.astype(F32)
        cur = qpos // L_SEL
        imp = jnp.where((blk == 0) | (blk == cur) | (blk == cur - 1), jnp.inf, imp)
        imp = jnp.where(blk * L_SEL > qpos, -jnp.inf, imp)
        work = jnp.where(blk < n_sel, jnp.maximum(imp, -3e38), -jnp.inf)
        out_lane = lax.broadcasted_iota(jnp.int32, (1, LANES), 1)
        idx = jnp.zeros((ns, LANES), jnp.int32)
        for t in range(N_SEL):
            mx = jnp.max(work, axis=-1, keepdims=True)
            pick = jnp.min(jnp.where(work == mx, fblk, float(nsp)), axis=-1, keepdims=True)
            idx = jnp.where(out_lane == t, pick.astype(jnp.int32), idx)
            work = jnp.where(fblk == pick, -jnp.inf, work)
        idx_ref[...] = idx


def _nsa_select(c, z3, pos, w1, w2, qpos):
    s, nc, _ = c.shape
    n_cmp = nc - 1
    n_sel = qpos // L_SEL + 1
    nsp = -(-n_sel // LANES) * LANES
    cover = _cover_matrix(nc, n_cmp, nsp)
    kern = functools.partial(_nsa_select_kernel, n_cmp=n_cmp, n_sel=n_sel, qpos=qpos)
    cst = lambda shape: pl.BlockSpec(shape, lambda b: (0,) * len(shape))
    return pl.pallas_call(
        kern,
        grid=(s,),
        in_specs=[pl.BlockSpec((None, nc, 4 * HEAD_DIM), lambda b: (b, 0, 0)),
                  pl.BlockSpec((None, 1, N_IN_PAD), lambda b: (b, 0, 0)),
                  cst(pos.shape), cst(w1.shape), cst(w2.shape), cst(cover.shape)],
        out_specs=[pl.BlockSpec((None, N_HEADS, HEAD_DIM), lambda b: (b, 0, 0)),
                   pl.BlockSpec((s, LANES), lambda b: (0, 0))],
        out_shape=[jax.ShapeDtypeStruct((s, N_HEADS, HEAD_DIM), F32), jax.ShapeDtypeStruct((s, LANES), jnp.int32)],
        scratch_shapes=[pltpu.VMEM((s, nc), F32)],
        compiler_params=_cparams(("arbitrary",)),
        name="nsa_select",
    )(c, z3, pos, w1, w2, cover)


def _col_from_row(row):
    n = row.shape[1]
    eye = lax.broadcasted_iota(jnp.int32, (n, n), 0) == lax.broadcasted_iota(jnp.int32, (n, n), 1)
    return jnp.sum(jnp.where(eye, row, 0.0), axis=-1, keepdims=True)


def _nsa_sample_kernel(pt_ref, idx_ref, cache_ref, z_ref, ocmp_ref, win_ref, o_ref, wnew_ref, buf_ref, sem_ref,
                       *, layer, page, qpos):
    b = pl.program_id(0)
    npg = pt_ref.shape[1]
    per_page = page // L_SEL

    slot = b % 2

    def copy(bb, t, sl):
        pg = jnp.minimum(idx_ref[bb, t] // per_page, npg - 1)
        return pltpu.make_async_copy(cache_ref.at[layer, pt_ref[bb, pg], pl.ds(2, 2)], buf_ref.at[sl, t],
                                     sem_ref.at[sl])

    @pl.when(b == 0)
    def _():
        for t in range(N_SEL):
            copy(b, t, 0).start()

    @pl.when(b + 1 < pl.num_programs(0))
    def _():
        for t in range(N_SEL):
            copy(b + 1, t, 1 - slot).start()

    z = z_ref[...]
    q = z[:, OFF_QNSA:OFF_QNSA + GROUP_W]
    q4b = _stack_heads(q).astype(BF16)
    kvn = z[:, OFF_KVNSA:OFF_KVNSA + GROUP_W]
    ks_new, vs_new = kvn[:, 2 * HEAD_DIM:3 * HEAD_DIM], kvn[:, 3 * HEAD_DIM:4 * HEAD_DIM]
    kvw = z[:, OFF_KVWIN:OFF_KVWIN + 2 * HEAD_DIM]
    kw_new, vw_new = kvw[:, 0:HEAD_DIM], kvw[:, HEAD_DIM:2 * HEAD_DIM]
    q4r = q4b.astype(F32)

    def new_score(k_new):
        kb = k_new.astype(BF16).astype(F32)
        return jnp.sum(q4r * kb, axis=-1, keepdims=True) * ATTN_SCALE

    wk = win_ref[0]
    wv = win_ref[1]
    sw = _dot(q4b, wk.astype(BF16)) * ATTN_SCALE
    sw_new = new_score(kw_new)
    mw = jnp.maximum(jnp.max(sw, axis=-1, keepdims=True), sw_new)
    ew = jnp.exp(sw - mw)
    ew_new = jnp.exp(sw_new - mw)
    dw = jnp.sum(ew, axis=-1, keepdims=True) + ew_new
    pw = ew * (1.0 / dw)
    pw_new = ew_new * (1.0 / dw)
    o_win = _dot_nt(pw.astype(BF16), wv.astype(BF16)) \
        + pw_new.astype(BF16).astype(F32) * vw_new.astype(BF16).astype(F32)
    wl = wk.shape[1]
    lane = lax.broadcasted_iota(jnp.int32, (1, wl), 1)
    wnew_ref[0] = jnp.where(lane == wl - 1, _col_from_row(kw_new), pltpu.roll(wk, wl - 1, 1))
    wnew_ref[1] = jnp.where(lane == wl - 1, _col_from_row(vw_new), pltpu.roll(wv, wl - 1, 1))

    pltpu.make_async_copy(cache_ref.at[layer, pl.ds(0, N_SEL), pl.ds(2, 2)], buf_ref.at[slot],
                          sem_ref.at[slot]).wait()

    n_past = (qpos // L_SEL)
    plane = lax.broadcasted_iota(jnp.int32, (1, page), 1)
    scores = []
    new_taken = jnp.int32(0)
    for t in range(N_SEL):
        bid = idx_ref[b, t]
        in_past = bid < n_past
        half = bid % per_page
        valid = (plane >= half * L_SEL) & (plane < (half + 1) * L_SEL) & in_past
        sc = _dot(q4b, buf_ref[slot, t, 0].astype(BF16)) * ATTN_SCALE
        scores.append(jnp.where(valid, sc, -jnp.inf))
        new_taken = new_taken + (bid == n_past).astype(jnp.int32)
    has_new = new_taken > 0
    ss_new = jnp.where(has_new, new_score(ks_new), -jnp.inf)
    ms = ss_new
    for sc in scores:
        ms = jnp.maximum(ms, jnp.max(sc, axis=-1, keepdims=True))
    ms = jnp.where(ms == -jnp.inf, 0.0, ms)
    es_new = jnp.exp(ss_new - ms)
    ds = es_new
    es = []
    for sc in scores:
        e = jnp.exp(sc - ms)
        es.append(e)
        ds = ds + jnp.sum(e, axis=-1, keepdims=True)
    inv = 1.0 / jnp.where(ds > 0, ds, 1.0)
    o_sel = (es_new * inv).astype(BF16).astype(F32) * vs_new.astype(BF16).astype(F32)
    for t in range(N_SEL):
        o_sel = o_sel + _dot_nt((es[t] * inv).astype(BF16), buf_ref[slot, t, 1].astype(BF16))

    g = _sigmoid(z[:, OFF_GATE:OFF_GATE + LANES])
    o_cmp = ocmp_ref[...]
    for h in range(N_HEADS):
        oh = (g[:, h:h + 1] * o_cmp[h:h + 1] + g[:, N_HEADS + h:N_HEADS + h + 1] * o_sel[h:h + 1]
              + g[:, 2 * N_HEADS + h:2 * N_HEADS + h + 1] * o_win[h:h + 1])
        o_ref[:, h * HEAD_DIM:(h + 1) * HEAD_DIM] = oh


def _nsa_sample(page_table, idx, cache_t, z3, ocmp, win_t, layer, qpos):
    s = z3.shape[0]
    page = cache_t.shape[-1]
    wl = win_t.shape[-1]
    kern = functools.partial(_nsa_sample_kernel, layer=layer, page=page, qpos=qpos)
    return pl.pallas_call(
        kern,
        grid_spec=pltpu.PrefetchScalarGridSpec(
            num_scalar_prefetch=2,
            grid=(s,),
            in_specs=[pl.BlockSpec(memory_space=pl.ANY),
                      pl.BlockSpec((None, 1, N_IN_PAD), lambda b, pt, ix: (b, 0, 0)),
                      pl.BlockSpec((None, N_HEADS, HEAD_DIM), lambda b, pt, ix: (b, 0, 0)),
                      pl.BlockSpec((None, None, 2, HEAD_DIM, wl), lambda b, pt, ix: (layer, b, 0, 0, 0))],
            out_specs=[pl.BlockSpec((None, 1, GROUP_W), lambda b, pt, ix: (b, 0, 0)),
                       pl.BlockSpec((None, 2, HEAD_DIM, wl), lambda b, pt, ix: (b, 0, 0, 0))],
            scratch_shapes=[pltpu.VMEM((2, N_SEL, 2, HEAD_DIM, page), F32), pltpu.SemaphoreType.DMA((2,))]),
        out_shape=[jax.ShapeDtypeStruct((s, 1, GROUP_W), F32), jax.ShapeDtypeStruct((s, 2, HEAD_DIM, wl), F32)],
        compiler_params=_cparams(("arbitrary",)),
        name="nsa_sample",
    )(page_table, idx, cache_t, z3, ocmp, win_t)


_MOBA_PAGES = 16


def _moba_gate_kernel(pt_ref, cache_ref, qcol_ref, idx_ref, buf_ref, part_ref, sem_ref,
                      *, layer, n_steps, page, nb, cur):
    b = pl.program_id(0)
    s = pl.program_id(1)
    g = _MOBA_PAGES
    step = b * n_steps + s
    total = pl.num_programs(0) * n_steps
    ppb = MOBA_BLOCK // page
    bps = g // ppb

    def copies(stp, slot):
        bb = stp // n_steps
        ss = stp % n_steps
        return [pltpu.make_async_copy(cache_ref.at[layer, pt_ref[bb, ss * g + k], 0],
                                      buf_ref.at[slot, k], sem_ref.at[slot]) for k in range(g)]

    ahead = _STREAM_SLOTS - 1

    @pl.when(step == 0)
    def _():
        for d in range(ahead):
            @pl.when(d < total)
            def _():
                for cp in copies(d, d):
                    cp.start()

    slot = step % _STREAM_SLOTS

    @pl.when(step + ahead < total)
    def _():
        for cp in copies(step + ahead, (step + ahead) % _STREAM_SLOTS):
            cp.start()

    pltpu.make_async_copy(cache_ref.at[layer, pl.ds(0, g), 0], buf_ref.at[slot], sem_ref.at[slot]).wait()

    for j in range(bps):
        for h in range(N_HEADS):
            qh = qcol_ref[h * HEAD_DIM:(h + 1) * HEAD_DIM, :]
            acc = buf_ref[slot, j * ppb, h] * qh
            for k in range(1, ppb):
                acc = acc + buf_ref[slot, j * ppb + k, h] * qh
            part_ref[h * nb + s * bps + j] = jnp.sum(acc.reshape(HEAD_DIM // 8, 8, page), axis=0)

    @pl.when(s == n_steps - 1)
    def _():
        ones = jnp.ones((8, page), F32)
        part = jnp.sum(part_ref[...], axis=1)
        gate = _dot_nt(ones, part, precision=HIGHEST)[0:1] * (1.0 / MOBA_BLOCK)
        lane = lax.broadcasted_iota(jnp.int32, gate.shape, 1)
        out_lane = lax.broadcasted_iota(jnp.int32, (1, LANES), 1)
        idx = jnp.zeros((1, LANES), jnp.int32)
        for h in range(N_HEADS):
            n_of = lane - h * nb
            f_of = n_of.astype(F32)
            inh = (n_of >= 0) & (n_of < nb) & (n_of < cur)
            work = jnp.where(inh, jnp.maximum(gate, -3e38), -jnp.inf)
            for t in range(MOBA_TOPK):
                mx = jnp.max(work, axis=-1, keepdims=True)
                pick = jnp.min(jnp.where((work == mx) & inh, f_of, float(4 * nb)), axis=-1, keepdims=True)
                idx = jnp.where(out_lane == h * MOBA_TOPK + t, pick.astype(jnp.int32), idx)
                work = jnp.where(f_of == pick, -jnp.inf, work)
        idx_ref[...] = idx


def _moba_gate(page_table, cache_t, qcol, layer, cur):
    s, npg = page_table.shape
    page = cache_t.shape[-1]
    g = _MOBA_PAGES
    n_steps = npg // g
    nb = npg * page // MOBA_BLOCK
    kern = functools.partial(_moba_gate_kernel, layer=layer, n_steps=n_steps, page=page, nb=nb, cur=cur)
    return pl.pallas_call(
        kern,
        grid_spec=pltpu.PrefetchScalarGridSpec(
            num_scalar_prefetch=1,
            grid=(s, n_steps),
            in_specs=[pl.BlockSpec(memory_space=pl.ANY),
                      pl.BlockSpec((None, N_HEADS * HEAD_DIM, page), lambda b, i, pt: (b, 0, 0))],
            out_specs=pl.BlockSpec((None, 1, LANES), lambda b, i, pt: (b, 0, 0)),
            scratch_shapes=[pltpu.VMEM((_STREAM_SLOTS, g, N_HEADS, HEAD_DIM, page), F32),
                            pltpu.VMEM((N_HEADS * nb, 8, page), F32),
                            pltpu.SemaphoreType.DMA((_STREAM_SLOTS,))]),
        out_shape=jax.ShapeDtypeStruct((s, 1, LANES), jnp.int32),
        compiler_params=_cparams(("arbitrary", "arbitrary")),
        name="moba_gate",
    )(page_table, cache_t, qcol)


def _moba_sample_kernel(pt_ref, idx_ref, cache_ref, z_ref, o_ref, buf_ref, sem_ref, *, layer, page, cur):
    b = pl.program_id(0)
    npg = pt_ref.shape[1]
    ppb = MOBA_BLOCK // page
    n_src = MOBA_TOPK * ppb

    slot = b % 2

    def copy(bb, sl, h, t, k, kv):
        blk = jnp.minimum(idx_ref[bb, h * MOBA_TOPK + t], npg // ppb - 1)
        return pltpu.make_async_copy(cache_ref.at[layer, pt_ref[bb, blk * ppb + k], kv, h],
                                     buf_ref.at[sl, t * ppb + k, kv, h], sem_ref.at[sl])

    every = [(h, t, k, kv) for h in range(N_HEADS) for t in range(MOBA_TOPK) for k in range(ppb) for kv in range(2)]

    @pl.when(b == 0)
    def _():
        for a in every:
            copy(b, 0, *a).start()

    @pl.when(b + 1 < pl.num_programs(0))
    def _():
        for a in every:
            copy(b + 1, 1 - slot, *a).start()

    z = z_ref[...]
    q = z[:, OFF_QMOBA:OFF_QMOBA + GROUP_W]
    k_new = z[:, OFF_KVMOBA:OFF_KVMOBA + GROUP_W]
    v_new = z[:, OFF_KVMOBA + GROUP_W:OFF_KVMOBA + 2 * GROUP_W]
    pltpu.make_async_copy(cache_ref.at[layer, pl.ds(0, n_src)], buf_ref.at[slot], sem_ref.at[slot]).wait()

    for h in range(N_HEADS):
        lo, hi = h * HEAD_DIM, (h + 1) * HEAD_DIM
        qh = jnp.concatenate([q[:, lo:hi]] * 8, axis=0).astype(BF16)
        qr = q[:, lo:hi].astype(BF16).astype(F32)
        s_new = jnp.sum(qr * k_new[:, lo:hi].astype(BF16).astype(F32), axis=-1, keepdims=True) * ATTN_SCALE
        scores = []
        m = s_new
        for t in range(MOBA_TOPK):
            ok = idx_ref[b, h * MOBA_TOPK + t] < cur
            for k in range(ppb):
                sc = _dot(qh, buf_ref[slot, t * ppb + k, 0, h].astype(BF16))[0:1] * ATTN_SCALE
                sc = jnp.where(ok, sc, -jnp.inf)
                scores.append(sc)
                m = jnp.maximum(m, jnp.max(sc, axis=-1, keepdims=True))
        e_new = jnp.exp(s_new - m)
        d = e_new
        es = []
        for sc in scores:
            e = jnp.exp(sc - m)
            es.append(e)
            d = d + jnp.sum(e, axis=-1, keepdims=True)
        inv = 1.0 / d
        o = (e_new * inv).astype(BF16).astype(F32) * v_new[:, lo:hi].astype(BF16).astype(F32)
        for n in range(n_src):
            pn = jnp.concatenate([es[n] * inv] * 8, axis=0).astype(BF16)
            o = o + _dot_nt(pn, buf_ref[slot, n, 1, h].astype(BF16))[0:1]
        o_ref[:, lo:hi] = o


def _moba_sample(page_table, idx, cache_t, z3, layer, cur):
    s = z3.shape[0]
    page = cache_t.shape[-1]
    ppb = MOBA_BLOCK // page
    kern = functools.partial(_moba_sample_kernel, layer=layer, page=page, cur=cur)
    return pl.pallas_call(
        kern,
        grid_spec=pltpu.PrefetchScalarGridSpec(
            num_scalar_prefetch=2,
            grid=(s,),
            in_specs=[pl.BlockSpec(memory_space=pl.ANY),
                      pl.BlockSpec((None, 1, N_IN_PAD), lambda b, pt, ix: (b, 0, 0))],
            out_specs=pl.BlockSpec((None, 1, GROUP_W), lambda b, pt, ix: (b, 0, 0)),
            scratch_shapes=[pltpu.VMEM((2, MOBA_TOPK * ppb, 2, N_HEADS, HEAD_DIM, page), F32),
                            pltpu.SemaphoreType.DMA((2,))]),
        out_shape=jax.ShapeDtypeStruct((s, 1, GROUP_W), F32),
        compiler_params=_cparams(("arbitrary",)),
        name="moba_sample",
    )(page_table, idx, cache_t, z3)


def _cmp_weight_stack(w1):
    w = w1.reshape(2, 2, CMP_STRIDE, HEAD_DIM, HEAD_DIM)
    zero = jnp.zeros((CMP_STRIDE, HEAD_DIM, 2 * HEAD_DIM), F32)
    top = jnp.concatenate([w[0, 0], w[0, 1], zero], axis=-1)
    bot = jnp.concatenate([zero, w[1, 0], w[1, 1]], axis=-1)
    per_r = jnp.concatenate([top, bot], axis=1)
    return per_r.reshape(CMP_STRIDE // 2, 4 * HEAD_DIM, 4 * HEAD_DIM).astype(BF16)


def kernel(x_prompt, x_sample, cache_moba_kv, cache_nsa_kv, state_nsa_win, state_conv, state_pool, page_table,
           w_in, conv_dw, conv_dw_b, conv_ln_g, conv_ln_b, conv_pw, nsa_cmp_pos, nsa_cmp_w1, nsa_cmp_w2,
           pool_w, pool_scale, w_out, ln1_g, ln1_b, w_up, w_down, ln2_g, ln2_b):
    bp, t, _ = x_prompt.shape
    bs = x_sample.shape[0]
    depth = w_in.shape[0]
    page = cache_nsa_kv.shape[2]
    past_len = page_table.shape[1] * page
    win_len = state_nsa_win.shape[2]
    assert x_sample.shape[1] == 1 and win_len == WINDOW and t % 512 == 0 and t >= WINDOW + 128
    assert page % L_SEL == 0 and MOBA_BLOCK % page == 0 and past_len % MOBA_BLOCK == 0
    assert past_len // MOBA_BLOCK >= MOBA_TOPK and past_len // L_SEL + 1 >= N_SEL and past_len >= POOL_MAX
    assert page_table.shape[1] % _CMP_PAGES == 0 and page_table.shape[1] % _MOBA_PAGES == 0

    nsa_t = jnp.transpose(cache_nsa_kv, (0, 1, 3, 4, 2))
    moba_t = jnp.transpose(cache_moba_kv, (0, 1, 3, 4, 5, 2))
    win_t = jnp.transpose(state_nsa_win, (0, 1, 3, 4, 2))
    conv_t = jnp.transpose(state_conv, (0, 2, 1, 3))
    pool_t = jnp.transpose(state_pool, (0, 2, 1, 3))

    row = lambda v: v.reshape(1, -1)
    yp = x_prompt.reshape(bp * t, D_MODEL)
    ys = x_sample.reshape(bs, D_MODEL)
    st_p, st_s = [], []
    w_out_b, w_up_b, w_down_b = w_out.astype(BF16), w_up.astype(BF16), w_down.astype(BF16)
    for l in range(depth):
        w_in_l = _in_weights(w_in[l])
        pw = conv_pw[l].astype(BF16)
        plw = jax.scipy.linalg.block_diag(*[pool_w[l, g] for g in range(len(POOL_WINDOWS))]).astype(BF16)
        w1 = nsa_cmp_w1[l].astype(BF16)
        w2 = nsa_cmp_w2[l].astype(BF16)
        pos = jnp.broadcast_to(nsa_cmp_pos[l].reshape(2, 1, L_CMP * HEAD_DIM), (2, 8, L_CMP * HEAD_DIM))
        small = (conv_dw[l], row(conv_dw_b[l]), row(conv_ln_g[l]), row(conv_ln_b[l]), pw, plw, row(pool_scale[l]))

        z, kvm, kvn, kvw = _in_proj_split(yp, w_in_l)
        z3 = z.reshape(bp, t, N_MAIN)
        kvm3 = kvm.reshape(bp, t, 2 * GROUP_W)
        kvn3 = kvn.reshape(bp, t, GROUP_W)
        win3 = kvw.reshape(bp, t, 2 * HEAD_DIM)
        y_conv, y_pool, conv_new32 = _conv_pool_prompt(z3, *small)
        nchunk = t // CMP_STRIDE
        chunks_k = kvn3[:, :, 0:HEAD_DIM].reshape(bp, nchunk, CMP_STRIDE * HEAD_DIM)
        chunks_v = kvn3[:, :, HEAD_DIM:2 * HEAD_DIM].reshape(bp, nchunk, CMP_STRIDE * HEAD_DIM)
        kc, vc = _compress_prompt(chunks_k, chunks_v, pos, w1, w2)
        o_nsa = _nsa_prompt(z3, kvn3, win3, kc, vc)
        o_moba = _moba_prompt(z3, kvm3)
        flat = lambda a: a.reshape(bp * t, GROUP_W)
        x1 = _out_proj((flat(y_conv), flat(o_nsa), flat(o_moba), flat(y_pool)), yp, w_out_b, l,
                       row(ln1_g[l]), row(ln1_b[l]))
        yp = _ffn(x1, w_up_b, w_down_b, l, row(ln2_g[l]), row(ln2_b[l]))
        moba_rows = kvm3.reshape(bp, t, 2, N_HEADS, HEAD_DIM)
        nsa_rows = kvn3.reshape(bp, t, 4, HEAD_DIM)
        win_new = win3[:, t - win_len:].reshape(bp, win_len, 2, HEAD_DIM)
        conv_new = conv_new32[:, _CONV_HALO - (CONV_W - 1):]
        pool_new = z3[:, t - (POOL_MAX - 1):, OFF_UPOOL:OFF_UPOOL + GROUP_W]
        st_p.append((moba_rows, nsa_rows, win_new, conv_new, pool_new))

        zs = _in_proj(ys, w_in_l)
        zs3 = zs.reshape(bs, 1, N_IN_PAD)
        ys_conv, ys_pool, conv_new_t, pool_new_t = _conv_pool_sample(zs, conv_t[l], pool_t[l], *small)
        c = _compress_pages(page_table, nsa_t, _cmp_weight_stack(nsa_cmp_w1[l]), l)
        o_cmp, sel_idx = _nsa_select(c, zs3, pos, w1, w2, past_len)
        os_nsa, win_new_t = _nsa_sample(page_table, sel_idx, nsa_t, zs3, o_cmp, win_t, l, past_len)
        q_m = zs[:, OFF_QMOBA:OFF_QMOBA + GROUP_W]
        qcol = jnp.broadcast_to(q_m[:, :, None], (bs, GROUP_W, page))
        top_idx = _moba_gate(page_table, moba_t, qcol, l, past_len // MOBA_BLOCK)
        os_moba = _moba_sample(page_table, top_idx.reshape(bs, LANES), moba_t, zs3, l, past_len // MOBA_BLOCK)
        x1s = _out_proj((ys_conv, os_nsa.reshape(bs, GROUP_W), os_moba.reshape(bs, GROUP_W), ys_pool), ys, w_out_b, l,
                        row(ln1_g[l]), row(ln1_b[l]))
        ys = _ffn(x1s, w_up_b, w_down_b, l, row(ln2_g[l]), row(ln2_b[l]))
        st_s.append((zs[:, OFF_KVMOBA:OFF_KVMOBA + 2 * GROUP_W].reshape(bs, 1, 2, N_HEADS, HEAD_DIM),
                     zs[:, OFF_KVNSA:OFF_KVNSA + GROUP_W].reshape(bs, 1, 4, HEAD_DIM),
                     jnp.transpose(win_new_t, (0, 3, 1, 2)),
                     jnp.transpose(conv_new_t, (1, 0, 2)),
                     jnp.transpose(pool_new_t, (1, 0, 2))))

    stk = lambda sts, i: jnp.stack([s[i] for s in sts], axis=0)
    return (yp.reshape(bp, t, D_MODEL), ys.reshape(bs, 1, D_MODEL),
            stk(st_p, 0), stk(st_s, 0), stk(st_p, 1), stk(st_s, 1), stk(st_p, 2), stk(st_s, 2),
            stk(st_p, 3), stk(st_s, 3), stk(st_p, 4), stk(st_s, 4))
```

```python
import functools

import numpy as np
import jax
import jax.numpy as jnp
from jax import lax
from jax.experimental import pallas as pl
from jax.experimental.pallas import tpu as pltpu

F32 = jnp.float32
BF16 = jnp.bfloat16
HIGHEST = lax.Precision.HIGHEST

D_MODEL = 1024
HEAD_DIM = 64
GROUP_W = 256
N_HEADS = 4
CONV_W = 31
L_CMP = 32
CMP_STRIDE = 16
L_SEL = 64
N_SEL = 16
WINDOW = 512
MOBA_BLOCK = 256
MOBA_TOPK = 3
POOL_WINDOWS = (2, 4, 8, 16)
POOL_GW = 64
POOL_MAX = 16
D_FF = 4096
DEPTH = 2
ALPHA = (2 * DEPTH) ** 0.25
LN_EPS = 1e-5
ATTN_SCALE = HEAD_DIM ** -0.5
NEG = -1e30

OFF_UCONV, OFF_QNSA, OFF_QMOBA, OFF_UPOOL, OFF_GATE = 0, 512, 768, 1024, 1280
N_MAIN = 1408
OFF_KVMOBA, OFF_KVNSA, OFF_KVWIN = 1408, 1920, 2176
N_IN_PAD = 2304
N_GATE = 12
LANES = 128
VMEM_LIMIT = 56 * 1024 * 1024


def _in_weights(w):
    main = np.concatenate([np.arange(0, 512), np.arange(512, 768), np.arange(1164, 1420), np.arange(1932, 2188),
                           np.arange(1152, 1164)])
    kv = np.concatenate([np.arange(1420, 1932), np.arange(768, 1024), np.arange(1024, 1152)])
    pad = jnp.zeros((w.shape[0], LANES - N_GATE), w.dtype)
    return jnp.concatenate([w[:, main], pad, w[:, kv]], axis=1).astype(BF16)


def _ln(x, g, b):
    mu = jnp.mean(x, axis=-1, keepdims=True)
    xc = x - mu
    var = jnp.mean(xc * xc, axis=-1, keepdims=True)
    return xc * lax.rsqrt(var + LN_EPS) * g + b


def _sigmoid(x):
    return 1.0 / (1.0 + jnp.exp(-x))


def _silu(x):
    return x * _sigmoid(x)


def _dot(a, b, **kw):
    return jnp.dot(a, b, preferred_element_type=F32, **kw)


def _dot_nt(a, b, **kw):
    return lax.dot_general(a, b, (((1,), (1,)), ((), ())), preferred_element_type=F32, **kw)


def _masked_softmax(s, mask):
    r = s.shape[0]
    s = jnp.where(mask, s, -jnp.inf)
    m = jnp.max(_lane_partial(s, jnp.maximum), axis=-1, keepdims=True)
    m = jnp.broadcast_to(jnp.where(m == -jnp.inf, 0.0, m), (r, LANES))
    e = _exp_shifted(s, m)
    d = jnp.sum(_lane_partial(e, jnp.add), axis=-1, keepdims=True)
    inv = jnp.broadcast_to(1.0 / jnp.where(d > 0, d, 1.0), (r, LANES))
    return jnp.concatenate([e[:, g * LANES:(g + 1) * LANES] * inv for g in range(s.shape[1] // LANES)], axis=-1)


def _lane_partial(s, op):
    out = s[:, 0:LANES]
    for g in range(1, s.shape[1] // LANES):
        out = op(out, s[:, g * LANES:(g + 1) * LANES])
    return out


def _exp_shifted(s, m):
    return jnp.concatenate([jnp.exp(s[:, g * LANES:(g + 1) * LANES] - m) for g in range(s.shape[1] // LANES)], axis=-1)


def _sweep(count, tiles_fn):
    def pair(jj, carry):
        tiles_fn(2 * jj, 2)
        return carry

    lax.fori_loop(0, count // 2, pair, 0)

    @pl.when(count % 2 == 1)
    def _():
        tiles_fn(count - 1, 1)


def _cparams(sem, vmem=None):
    return pltpu.CompilerParams(dimension_semantics=sem, vmem_limit_bytes=vmem or VMEM_LIMIT)


def _in_proj_kernel(x_ref, w_ref, o_ref):
    o_ref[...] = _dot(x_ref[...].astype(BF16), w_ref[...])


def _in_proj(x, w):
    m, k = x.shape
    n = w.shape[1]
    tm = min(m, 512)
    return pl.pallas_call(
        _in_proj_kernel,
        grid=(m // tm,),
        in_specs=[pl.BlockSpec((tm, k), lambda i: (i, 0)), pl.BlockSpec((k, n), lambda i: (0, 0))],
        out_specs=pl.BlockSpec((tm, n), lambda i: (i, 0)),
        out_shape=jax.ShapeDtypeStruct((m, n), F32),
        compiler_params=_cparams(("parallel",)),
        name="in_proj",
    )(x, w)


_IN_PARTS = ((0, N_MAIN), (OFF_KVMOBA, OFF_KVNSA), (OFF_KVNSA, OFF_KVWIN), (OFF_KVWIN, N_IN_PAD))


def _in_proj_split_kernel(x_ref, w_ref, *o_refs):
    z = _dot(x_ref[...].astype(BF16), w_ref[...])
    for (lo, hi), o_ref in zip(_IN_PARTS, o_refs):
        o_ref[...] = z[:, lo:hi]


def _in_proj_split(x, w):
    m, k = x.shape
    tm = min(m, 512)
    return pl.pallas_call(
        _in_proj_split_kernel,
        grid=(m // tm,),
        in_specs=[pl.BlockSpec((tm, k), lambda i: (i, 0)), pl.BlockSpec(w.shape, lambda i: (0, 0))],
        out_specs=[pl.BlockSpec((tm, hi - lo), lambda i: (i, 0)) for lo, hi in _IN_PARTS],
        out_shape=[jax.ShapeDtypeStruct((m, hi - lo), F32) for lo, hi in _IN_PARTS],
        compiler_params=_cparams(("parallel",)),
        name="in_proj_split",
    )(x, w)


def _out_proj_kernel(a_ref, b_ref, c_ref, d_ref, x_ref, w_ref, g_ref, bt_ref, o_ref):
    mix = jnp.concatenate([a_ref[...], b_ref[...], c_ref[...], d_ref[...]], axis=-1).astype(BF16)
    y = _dot(mix, w_ref[...])
    o_ref[...] = _ln(ALPHA * x_ref[...] + y, g_ref[...], bt_ref[...])


def _out_proj(pieces, x, w, layer, g, b):
    m = x.shape[0]
    tm = min(m, 512)
    pspec = pl.BlockSpec((tm, GROUP_W), lambda i: (i, 0))
    return pl.pallas_call(
        _out_proj_kernel,
        grid=(m // tm,),
        in_specs=[pspec, pspec, pspec, pspec,
                  pl.BlockSpec((tm, D_MODEL), lambda i: (i, 0)),
                  pl.BlockSpec((None, D_MODEL, D_MODEL), lambda i: (layer, 0, 0)),
                  pl.BlockSpec((1, D_MODEL), lambda i: (0, 0)),
                  pl.BlockSpec((1, D_MODEL), lambda i: (0, 0))],
        out_specs=pl.BlockSpec((tm, D_MODEL), lambda i: (i, 0)),
        out_shape=jax.ShapeDtypeStruct((m, D_MODEL), F32),
        compiler_params=_cparams(("parallel",)),
        name="out_proj_ln",
    )(*pieces, x, w, g, b)


def _ffn_kernel(x_ref, wu_ref, wd_ref, g_ref, b_ref, o_ref, acc_ref):
    j = pl.program_id(1)

    @pl.when(j == 0)
    def _():
        acc_ref[...] = jnp.zeros_like(acc_ref)

    h = jnp.maximum(_dot(x_ref[...].astype(BF16), wu_ref[...]), 0.0)
    acc_ref[...] += _dot((h * h).astype(BF16), wd_ref[...])

    @pl.when(j == pl.num_programs(1) - 1)
    def _():
        o_ref[...] = _ln(ALPHA * x_ref[...] + acc_ref[...], g_ref[...], b_ref[...])


def _ffn(x, wu, wd, layer, g, b):
    m = x.shape[0]
    tm = min(m, 1024)
    tf = 1024
    return pl.pallas_call(
        _ffn_kernel,
        grid=(m // tm, D_FF // tf),
        in_specs=[pl.BlockSpec((tm, D_MODEL), lambda i, j: (i, 0)),
                  pl.BlockSpec((None, D_MODEL, tf), lambda i, j: (layer, 0, j)),
                  pl.BlockSpec((None, tf, D_MODEL), lambda i, j: (layer, j, 0)),
                  pl.BlockSpec((1, D_MODEL), lambda i, j: (0, 0)),
                  pl.BlockSpec((1, D_MODEL), lambda i, j: (0, 0))],
        out_specs=pl.BlockSpec((tm, D_MODEL), lambda i, j: (i, 0)),
        out_shape=jax.ShapeDtypeStruct((m, D_MODEL), F32),
        scratch_shapes=[pltpu.VMEM((tm, D_MODEL), F32)],
        compiler_params=_cparams(("parallel", "arbitrary")),
        name="ffn_ln",
    )(x, wu, wd, g, b)


def _kv_rows_to_state_kernel(*refs):
    layer = pl.program_id(0)
    o_ref = refs[-1]
    for l, x_ref in enumerate(refs[:-1]):
        @pl.when(layer == l)
        def _():
            o_ref[...] = x_ref[...].T


def _kv_rows_to_state(per_layer):
    depth = len(per_layer)
    b, t, width = per_layer[0].shape
    tile = 512
    nt = t // tile

    def in_spec(l):
        def index(ll, bi, i):
            before, after = ll < l, ll > l
            return (jnp.where(before, 0, jnp.where(after, b - 1, bi)),
                    jnp.where(before, 0, jnp.where(after, nt - 1, i)), 0)
        return pl.BlockSpec((None, tile, width), index)

    return pl.pallas_call(
        _kv_rows_to_state_kernel,
        grid=(depth, b, nt),
        in_specs=[in_spec(l) for l in range(depth)],
        out_specs=pl.BlockSpec((None, None, width, tile), lambda ll, bi, i: (ll, bi, 0, i)),
        out_shape=jax.ShapeDtypeStruct((depth, b, width, t), F32),
        compiler_params=_cparams(("arbitrary", "arbitrary", "arbitrary")),
        name="kv_rows_to_state",
    )(*per_layer)


_CONV_HALO = 32
_POOL_HALO = 16


def _pool_groups(rows_ref, buf_ref, tt, cnt_fn):
    n = _POOL_HALO + tt
    lane = lax.broadcasted_iota(jnp.int32, (1, GROUP_W), 1)
    cur = rows_ref[pl.ds(_POOL_HALO, tt), :]
    out = jnp.zeros((tt, GROUP_W), F32)
    shift = 1
    for g, w in enumerate(POOL_WINDOWS):
        src = rows_ref if g == 0 else buf_ref.at[(g - 1) % 2]
        dst = buf_ref.at[g % 2]
        lo = 2 * shift - 1
        dst[pl.ds(lo, n - lo), :] = src[pl.ds(lo, n - lo), :] + src[pl.ds(lo - shift, n - lo), :]
        shift *= 2
        s = dst[pl.ds(_POOL_HALO, tt), :]
        mean = s / cnt_fn(w)
        out = jnp.where((lane >= g * POOL_GW) & (lane < (g + 1) * POOL_GW), mean, out)
    return out - cur


def _conv_pool_prompt_kernel(u_ref, uh_ref, p_ref, ph_ref, dw_ref, dwb_ref, lg_ref, lb_ref, pw_ref,
                             plw_ref, pls_ref, yc_ref, yp_ref, cnew_ref, zc_ref, res_ref, rows_ref, buf_ref, *, tt):
    i = pl.program_id(1)
    first = i == 0
    u = u_ref[...]
    zg = u[:, :GROUP_W] * _sigmoid(u[:, GROUP_W:])
    uh = uh_ref[...]
    zh = uh[:, :GROUP_W] * _sigmoid(uh[:, GROUP_W:])
    zc_ref[pl.ds(0, _CONV_HALO), :] = jnp.where(first, 0.0, zh)
    zc_ref[pl.ds(_CONV_HALO, tt), :] = zg
    zc_ref[pl.ds(_CONV_HALO + tt, 8), :] = jnp.zeros((8, GROUP_W), F32)
    off = _CONV_HALO - (CONV_W - 1)
    y = jnp.zeros((tt, GROUP_W), F32) + dwb_ref[...]
    for r in range(8):
        acc = None
        for a in range(-(-(off + CONV_W) // 8)):
            k = 8 * a + r - off
            if 0 <= k < CONV_W:
                term = dw_ref[k:k + 1, :] * zc_ref[pl.ds(8 * a, tt + 8), :]
                acc = term if acc is None else acc + term
        if r == 0:
            y = y + acc[0:tt]
        else:
            res_ref[r - 1] = acc
            y = y + res_ref[r - 1, pl.ds(r, tt), :]
    y = _silu(_ln(y, lg_ref[...], lb_ref[...]))
    yc_ref[...] = _dot(y.astype(BF16), pw_ref[...]).astype(yc_ref.dtype)

    @pl.when(i == pl.num_programs(1) - 1)
    def _():
        cnew_ref[...] = zc_ref[pl.ds(tt, _CONV_HALO), :]

    rows_ref[pl.ds(0, _POOL_HALO), :] = jnp.where(first, 0.0, ph_ref[...])
    rows_ref[pl.ds(_POOL_HALO, tt), :] = p_ref[...]
    qpos1 = i * tt + lax.broadcasted_iota(jnp.int32, (tt, 1), 0) + 1
    d = _pool_groups(rows_ref, buf_ref, tt, lambda w: jnp.minimum(w, qpos1).astype(F32))
    yp_ref[...] = (_dot(d.astype(BF16), plw_ref[...]) * pls_ref[...]).astype(yp_ref.dtype)


def _conv_pool_prompt(z3, dw, dwb, lg, lb, pw, plw, pls):
    b, t, _ = z3.shape
    tt = 512
    nt = t // tt
    kern = functools.partial(_conv_pool_prompt_kernel, tt=tt)
    cst = lambda shape: pl.BlockSpec(shape, lambda bi, i: (0,) * len(shape))
    return pl.pallas_call(
        kern,
        grid=(b, nt),
        in_specs=[
            pl.BlockSpec((None, tt, 512), lambda bi, i: (bi, i, OFF_UCONV // 512)),
            pl.BlockSpec((None, _CONV_HALO, 512),
                         lambda bi, i: (bi, jnp.maximum(i * (tt // _CONV_HALO) - 1, 0), OFF_UCONV // 512)),
            pl.BlockSpec((None, tt, GROUP_W), lambda bi, i: (bi, i, OFF_UPOOL // GROUP_W)),
            pl.BlockSpec((None, _POOL_HALO, GROUP_W),
                         lambda bi, i: (bi, jnp.maximum(i * (tt // _POOL_HALO) - 1, 0), OFF_UPOOL // GROUP_W)),
            cst((CONV_W, GROUP_W)), cst((1, GROUP_W)), cst((1, GROUP_W)), cst((1, GROUP_W)),
            cst((GROUP_W, GROUP_W)), cst((GROUP_W, GROUP_W)), cst((1, GROUP_W)),
        ],
        out_specs=[
            pl.BlockSpec((None, tt, GROUP_W), lambda bi, i: (bi, i, 0)),
            pl.BlockSpec((None, tt, GROUP_W), lambda bi, i: (bi, i, 0)),
            pl.BlockSpec((None, _CONV_HALO, GROUP_W), lambda bi, i: (bi, 0, 0)),
        ],
        out_shape=[jax.ShapeDtypeStruct((b, t, GROUP_W), BF16), jax.ShapeDtypeStruct((b, t, GROUP_W), BF16),
                   jax.ShapeDtypeStruct((b, _CONV_HALO, GROUP_W), F32)],
        scratch_shapes=[pltpu.VMEM((_CONV_HALO + tt + 8, GROUP_W), F32),
                        pltpu.VMEM((7, tt + 8, GROUP_W), F32),
                        pltpu.VMEM((_POOL_HALO + tt, GROUP_W), F32),
                        pltpu.VMEM((2, _POOL_HALO + tt, GROUP_W), F32)],
        compiler_params=_cparams(("parallel", "arbitrary")),
        name="conv_pool_prompt",
    )(z3, z3, z3, z3, dw, dwb, lg, lb, pw, plw, pls)


def _compress_tail(c, bias_k, bias_v, w2k, w2v):
    n = c.shape[0]

    def one(c0, c1, bias, w2):
        hid = c0 + pltpu.roll(c1, n - 1, 0) + bias
        return _dot(_silu(hid).astype(BF16), w2)

    kc = one(c[:, 0:64], c[:, 64:128], bias_k, w2k)
    vc = one(c[:, 128:192], c[:, 192:256], bias_v, w2v)
    return kc, vc


def _cmp_bias(pos_ref, w1_ref):
    bk = _dot(pos_ref[0].astype(BF16), w1_ref[0])[0:1]
    bv = _dot(pos_ref[1].astype(BF16), w1_ref[1])[0:1]
    return bk, bv


def _compress_prompt_kernel(ck_ref, cv_ref, pos_ref, w1_ref, w2_ref, kc_ref, vc_ref):
    bk, bv = _cmp_bias(pos_ref, w1_ref)
    half = CMP_STRIDE * HEAD_DIM
    ck = ck_ref[...].astype(BF16)
    cv = cv_ref[...].astype(BF16)
    c = jnp.concatenate([_dot(ck, w1_ref[0, 0:half, :]), _dot(ck, w1_ref[0, half:2 * half, :]),
                         _dot(cv, w1_ref[1, 0:half, :]), _dot(cv, w1_ref[1, half:2 * half, :])], axis=-1)
    kc, vc = _compress_tail(c, bk, bv, w2_ref[0], w2_ref[1])
    kc_ref[...] = kc
    vc_ref[...] = vc


def _compress_prompt(chunks_k, chunks_v, pos, w1, w2):
    b, nc, kk = chunks_k.shape
    cst = lambda shape: pl.BlockSpec(shape, lambda bi: (0,) * len(shape))
    return pl.pallas_call(
        _compress_prompt_kernel,
        grid=(b,),
        in_specs=[pl.BlockSpec((None, nc, kk), lambda bi: (bi, 0, 0)),
                  pl.BlockSpec((None, nc, kk), lambda bi: (bi, 0, 0)),
                  cst(pos.shape), cst(w1.shape), cst(w2.shape)],
        out_specs=[pl.BlockSpec((None, nc, HEAD_DIM), lambda bi: (bi, 0, 0)),
                   pl.BlockSpec((None, nc, HEAD_DIM), lambda bi: (bi, 0, 0))],
        out_shape=[jax.ShapeDtypeStruct((b, nc, HEAD_DIM), F32)] * 2,
        compiler_params=_cparams(("parallel",)),
        name="compress_prompt",
    )(chunks_k, chunks_v, pos, w1, w2)


def _rank_desc(v, n):
    lane = lax.broadcasted_iota(jnp.int32, v.shape, 1)
    rank = jnp.zeros(v.shape, F32)
    for j in range(n):
        col = v[:, j:j + 1]
        rank = rank + jnp.where((col > v) | ((col == v) & (lane > j)), 1.0, 0.0)
    return rank


def _cover_matrix(n_cmp_pad, n_cmp, n_sel_pad):
    j = np.arange(n_cmp_pad)[:, None]
    i = np.arange(n_sel_pad)[None, :]
    cov = (j * CMP_STRIDE < (i + 1) * L_SEL) & (j * CMP_STRIDE + L_CMP > i * L_SEL) & (j < n_cmp)
    return jnp.asarray(cov.astype(np.float32))


def _nsa_prompt_kernel(q_ref, g_ref, kv_ref, win_ref, kc_ref, vc_ref, covert_ref, o_ref,
                       kt_ref, v_ref, kwt_ref, vw_ref, qa_ref, so_ref, sp_ref, m_ref, l_ref, acc_ref, ow_ref,
                       *, c, tk, n_cmp, ns):
    i = pl.program_id(1)
    qs = i * c
    t = kv_ref.shape[0]
    nsp = LANES - HEAD_DIM

    @pl.when(i == 0)
    def _():
        rows = lax.broadcasted_iota(jnp.int32, (nsp, t), 0)
        cols = lax.broadcasted_iota(jnp.int32, (nsp, t), 1)
        kt_ref[HEAD_DIM:LANES, :] = jnp.where(rows == cols // L_SEL, 1.0, 0.0).astype(BF16)
        rt = 256
        for n in range(t // rt):
            x = kv_ref[n * rt:(n + 1) * rt, :]
            kt_ref[0:HEAD_DIM, n * rt:(n + 1) * rt] = x.T[2 * HEAD_DIM:3 * HEAD_DIM].astype(BF16)
            v_ref[n * rt:(n + 1) * rt, :] = x[:, 3 * HEAD_DIM:4 * HEAD_DIM].astype(BF16)
            w = win_ref[n * rt:(n + 1) * rt, :]
            kwt_ref[:, n * rt:(n + 1) * rt] = w.T[0:HEAD_DIM].astype(BF16)
            vw_ref[n * rt:(n + 1) * rt, :] = w[:, HEAD_DIM:2 * HEAD_DIM].astype(BF16)

    q = q_ref[...] * ATTN_SCALE
    q4 = jnp.concatenate([q[:, h * HEAD_DIM:(h + 1) * HEAD_DIM] for h in range(N_HEADS)], axis=0).astype(BF16)
    qpos = qs + lax.broadcasted_iota(jnp.int32, (c, 1), 0)
    qpos4 = jnp.concatenate([qpos] * N_HEADS, axis=0)

    wl = WINDOW + c
    ks = pl.multiple_of(jnp.maximum(qs - WINDOW, 0), c)
    rel = qpos - (ks + lax.broadcasted_iota(jnp.int32, (1, wl), 1))
    band = jnp.where((rel >= 0) & (rel <= WINDOW), 0.0, NEG)
    sw = _dot(q4, kwt_ref[:, pl.ds(ks, wl)]) + jnp.concatenate([band] * N_HEADS, axis=0)
    mw = jnp.broadcast_to(jnp.max(_lane_partial(sw, jnp.maximum), axis=-1, keepdims=True), (N_HEADS * c, LANES))
    ew = _exp_shifted(sw, mw)
    lw = jnp.sum(_lane_partial(ew, jnp.add), axis=-1, keepdims=True)
    ow_ref[...] = _dot(ew.astype(BF16), vw_ref[pl.ds(ks, wl), :]) * (1.0 / lw)

    nc = kc_ref.shape[0]
    s = _dot_nt(q4, kc_ref[...].astype(BF16))
    jj = lax.broadcasted_iota(jnp.int32, (1, nc), 1)
    p = _masked_softmax(s, (jj * CMP_STRIDE + (L_CMP - 1) <= qpos4) & (jj < n_cmp))
    o_cmp = _dot(p.astype(BF16), vc_ref[...].astype(BF16))
    psum = p[0:c] + p[c:2 * c] + p[2 * c:3 * c] + p[3 * c:4 * c]

    imp = _dot_nt(covert_ref[...], psum, precision=HIGHEST)
    blk = lax.broadcasted_iota(jnp.int32, (nsp, 1), 0)
    qpos_l = qs + lax.broadcasted_iota(jnp.int32, (1, c), 1)
    cur = qpos_l // L_SEL
    imp = jnp.where((blk == 0) | (blk == cur) | (blk == cur - 1), jnp.inf, imp)
    imp = jnp.where(blk * L_SEL > qpos_l, -jnp.inf, imp)
    sub = lax.broadcasted_iota(jnp.int32, (8, 1), 0)
    groups = [imp[8 * g:8 * (g + 1), :] for g in range(nsp // 8)]
    ranks = [jnp.zeros((8, c), F32) for _ in groups]
    for b in range(ns):
        rb = imp[b:b + 1, :]
        for g, v in enumerate(groups):
            if 8 * g > b:
                ahead = rb >= v
            elif 8 * g + 7 <= b:
                ahead = rb > v
            else:
                ahead = (rb > v) | ((rb == v) & (sub + 8 * g > b))
            ranks[g] = ranks[g] + jnp.where(ahead, 1.0, 0.0)
    rank = jnp.concatenate(ranks, axis=0)
    sel = (rank < N_SEL) & (blk * L_SEL <= qpos_l)
    bias_t = jnp.where(sel, 0.0, NEG)
    bias = jnp.concatenate([bias_t, jnp.zeros((LANES - nsp, c), F32)], axis=0).T[:, 0:nsp]
    qa = jnp.concatenate([q4, jnp.concatenate([bias] * N_HEADS, axis=0).astype(BF16)], axis=-1)

    jt = qs // tk
    k_own = pl.multiple_of(jt * tk, tk)
    qa_ref[...] = qa
    kpos = k_own + lax.broadcasted_iota(jnp.int32, (1, tk), 1)
    causal = jnp.where(kpos <= qpos, 0.0, NEG)
    sc = _dot(qa, kt_ref[:, pl.ds(k_own, tk)]) + jnp.concatenate([causal] * N_HEADS, axis=0)
    so_ref[...] = sc
    m_ref[...] = _lane_partial(sc, jnp.maximum)

    def max_tiles(j0, n):
        k0 = pl.multiple_of(j0 * tk, tk)
        sc = _dot(qa_ref[...], kt_ref[:, pl.ds(k0, n * tk)])
        for u in range(n):
            sp_ref[j0 + u] = sc[:, u * tk:(u + 1) * tk]
        m_ref[...] = jnp.maximum(m_ref[...], _lane_partial(sc, jnp.maximum))

    _sweep(jt, max_tiles)
    m = jnp.broadcast_to(jnp.max(m_ref[...], axis=-1, keepdims=True), m_ref.shape)
    m_ref[...] = m
    pe = _exp_shifted(so_ref[...], m)
    l_ref[...] = _lane_partial(pe, jnp.add)
    acc_ref[...] = _dot(pe.astype(BF16), v_ref[pl.ds(k_own, tk), :])

    def sum_tiles(j0, n):
        k0 = pl.multiple_of(j0 * tk, tk)
        pe = jnp.concatenate([_exp_shifted(sp_ref[j0 + u], m_ref[...]) for u in range(n)], axis=-1)
        l_ref[...] += _lane_partial(pe, jnp.add)
        acc_ref[...] += _dot(pe.astype(BF16), v_ref[pl.ds(k0, n * tk), :])

    _sweep(jt, sum_tiles)
    o_sel = acc_ref[...] * (1.0 / jnp.sum(l_ref[...], axis=-1, keepdims=True))

    o_win = ow_ref[...]
    g = _sigmoid(g_ref[...])

    def gate(br):
        return jnp.concatenate([g[:, br * N_HEADS + h:br * N_HEADS + h + 1] for h in range(N_HEADS)], axis=0)

    o = gate(0) * o_cmp + gate(1) * o_sel + gate(2) * o_win
    for h in range(N_HEADS):
        o_ref[:, h * HEAD_DIM:(h + 1) * HEAD_DIM] = o[h * c:(h + 1) * c].astype(o_ref.dtype)


def _nsa_prompt(z3, kvn3, win3, kc, vc):
    b, t, _ = z3.shape
    c = 128
    tk = 512
    nc = kc.shape[1]
    n_cmp = t // CMP_STRIDE - 1
    ns = t // L_SEL
    nsp = LANES - HEAD_DIM
    assert ns <= nsp and c == LANES
    covert = _cover_matrix(nc, n_cmp, nsp).T
    kern = functools.partial(_nsa_prompt_kernel, c=c, tk=tk, n_cmp=n_cmp, ns=ns)
    return pl.pallas_call(
        kern,
        grid=(b, t // c),
        in_specs=[
            pl.BlockSpec((None, c, GROUP_W), lambda bi, i: (bi, i, OFF_QNSA // GROUP_W)),
            pl.BlockSpec((None, c, LANES), lambda bi, i: (bi, i, OFF_GATE // LANES)),
            pl.BlockSpec((None, t, GROUP_W), lambda bi, i: (bi, 0, 0)),
            pl.BlockSpec((None, t, LANES), lambda bi, i: (bi, 0, 0)),
            pl.BlockSpec((None, nc, HEAD_DIM), lambda bi, i: (bi, 0, 0)),
            pl.BlockSpec((None, nc, HEAD_DIM), lambda bi, i: (bi, 0, 0)),
            pl.BlockSpec((nsp, nc), lambda bi, i: (0, 0)),
        ],
        out_specs=pl.BlockSpec((None, c, GROUP_W), lambda bi, i: (bi, i, 0)),
        out_shape=jax.ShapeDtypeStruct((b, t, GROUP_W), BF16),
        scratch_shapes=[pltpu.VMEM((LANES, t), BF16), pltpu.VMEM((t, HEAD_DIM), BF16),
                        pltpu.VMEM((HEAD_DIM, t), BF16), pltpu.VMEM((t, HEAD_DIM), BF16),
                        pltpu.VMEM((N_HEADS * c, LANES), BF16), pltpu.VMEM((N_HEADS * c, tk), F32),
                        pltpu.VMEM((t // tk - 1, N_HEADS * c, tk), F32),
                        pltpu.VMEM((N_HEADS * c, LANES), F32), pltpu.VMEM((N_HEADS * c, LANES), F32),
                        pltpu.VMEM((N_HEADS * c, HEAD_DIM), F32), pltpu.VMEM((N_HEADS * c, HEAD_DIM), F32)],
        compiler_params=_cparams(("parallel", "arbitrary")),
        name="nsa_prompt",
    )(z3, z3, kvn3, win3, kc, vc, covert)


def _moba_prompt_kernel(q_ref, kv_ref, o_ref, kmean_ref, kt_ref, v_ref, qa_ref, so_ref, sp_ref, m_ref, l_ref, acc_ref,
                        *, nb):
    i = pl.program_id(1)
    c = MOBA_BLOCK
    t = kv_ref.shape[0]

    @pl.when(i == 0)
    def _():
        rows = lax.broadcasted_iota(jnp.int32, (LANES - HEAD_DIM, t), 0)
        cols = lax.broadcasted_iota(jnp.int32, (LANES - HEAD_DIM, t), 1)
        onehot = jnp.where(rows == cols // c, 1.0, 0.0).astype(BF16)
        for h in range(N_HEADS):
            kt_ref[h, HEAD_DIM:LANES, :] = onehot
        kmean_ref[...] = jnp.zeros(kmean_ref.shape, F32)
        for n in range(nb):
            x = kv_ref[n * c:(n + 1) * c, :]
            kmean_ref[n:n + 1, :] = jnp.mean(x[:, 0:GROUP_W], axis=0, keepdims=True)
            xt = x[:, 0:GROUP_W].T
            for h in range(N_HEADS):
                lo, hi = h * HEAD_DIM, (h + 1) * HEAD_DIM
                kt_ref[h, 0:HEAD_DIM, n * c:(n + 1) * c] = xt[lo:hi].astype(BF16)
                v_ref[h, n * c:(n + 1) * c, :] = x[:, GROUP_W + lo:GROUP_W + hi].astype(BF16)

    q = q_ref[...]
    nbp = LANES - HEAD_DIM
    nbr = -(-nb // 8) * 8
    blk = lax.broadcasted_iota(jnp.int32, (nbr, 1), 0)
    fblk = blk.astype(F32)
    past = blk < i
    row = lax.broadcasted_iota(jnp.int32, (c, c), 0)
    col = lax.broadcasted_iota(jnp.int32, (c, c), 1)
    tri = col <= row
    own0 = pl.multiple_of(i * c, c)
    for h in range(N_HEADS):
        lo, hi = h * HEAD_DIM, (h + 1) * HEAD_DIM
        qh = q[:, lo:hi]
        work = jnp.where(past, _dot_nt(kmean_ref[0:nbr, lo:hi], qh, precision=HIGHEST), -jnp.inf)
        sel = jnp.zeros((nbr, c), jnp.bool_)
        for _ in range(MOBA_TOPK):
            mx = jnp.max(work, axis=0, keepdims=True)
            pick = jnp.min(jnp.where(work == mx, fblk, float(nbp)), axis=0, keepdims=True)
            hit = fblk == pick
            sel = sel | (hit & (mx > -jnp.inf))
            work = jnp.where(hit, -jnp.inf, work)
        bias_t = jnp.where(sel, 0.0, NEG)
        bias = jnp.concatenate([bias_t, jnp.full((LANES - nbr, c), NEG, F32)], axis=0).T[:, 0:nbp]
        qs = (qh * ATTN_SCALE).astype(BF16)
        qa_ref[h] = jnp.concatenate([qs, bias.astype(BF16)], axis=-1)
        sc = jnp.where(tri, _dot(qs, kt_ref[h, 0:HEAD_DIM, pl.ds(own0, c)]), NEG)
        so_ref[h] = sc
        m_ref[h] = _lane_partial(sc, jnp.maximum)

    def max_tiles(j0, n):
        k0 = pl.multiple_of(j0 * c, c)
        for h in range(N_HEADS):
            sc = _dot(qa_ref[h], kt_ref[h, :, pl.ds(k0, n * c)])
            for u in range(n):
                sp_ref[h, j0 + u] = sc[:, u * c:(u + 1) * c]
            m_ref[h] = jnp.maximum(m_ref[h], _lane_partial(sc, jnp.maximum))

    _sweep(i, max_tiles)
    for h in range(N_HEADS):
        m = jnp.broadcast_to(jnp.max(m_ref[h], axis=-1, keepdims=True), (c, LANES))
        m_ref[h] = m
        pe = _exp_shifted(so_ref[h], m)
        l_ref[h] = _lane_partial(pe, jnp.add)
        acc_ref[h] = _dot(pe.astype(BF16), v_ref[h, pl.ds(own0, c), :])

    def sum_tiles(j0, n):
        k0 = pl.multiple_of(j0 * c, c)
        for h in range(N_HEADS):
            pe = jnp.concatenate([_exp_shifted(sp_ref[h, j0 + u], m_ref[h]) for u in range(n)], axis=-1)
            l_ref[h] += _lane_partial(pe, jnp.add)
            acc_ref[h] += _dot(pe.astype(BF16), v_ref[h, pl.ds(k0, n * c), :])

    _sweep(i, sum_tiles)
    for h in range(N_HEADS):
        l = jnp.sum(l_ref[h], axis=-1, keepdims=True)
        o_ref[:, h * HEAD_DIM:(h + 1) * HEAD_DIM] = (acc_ref[h] * (1.0 / l)).astype(o_ref.dtype)


def _moba_prompt(z3, kvm3):
    b, t, _ = z3.shape
    nb = t // MOBA_BLOCK
    assert nb <= LANES - HEAD_DIM
    kern = functools.partial(_moba_prompt_kernel, nb=nb)
    return pl.pallas_call(
        kern,
        grid=(b, nb),
        in_specs=[pl.BlockSpec((None, MOBA_BLOCK, GROUP_W), lambda bi, i: (bi, i, OFF_QMOBA // GROUP_W)),
                  pl.BlockSpec((None, t, 2 * GROUP_W), lambda bi, i: (bi, 0, 0))],
        out_specs=pl.BlockSpec((None, MOBA_BLOCK, GROUP_W), lambda bi, i: (bi, i, 0)),
        out_shape=jax.ShapeDtypeStruct((b, t, GROUP_W), BF16),
        scratch_shapes=[pltpu.VMEM((LANES - HEAD_DIM, GROUP_W), F32),
                        pltpu.VMEM((N_HEADS, LANES, t), BF16),
                        pltpu.VMEM((N_HEADS, t, HEAD_DIM), BF16),
                        pltpu.VMEM((N_HEADS, MOBA_BLOCK, LANES), BF16),
                        pltpu.VMEM((N_HEADS, MOBA_BLOCK, MOBA_BLOCK), F32),
                        pltpu.VMEM((N_HEADS, nb - 1, MOBA_BLOCK, MOBA_BLOCK), F32),
                        pltpu.VMEM((N_HEADS, MOBA_BLOCK, LANES), F32),
                        pltpu.VMEM((N_HEADS, MOBA_BLOCK, LANES), F32),
                        pltpu.VMEM((N_HEADS, MOBA_BLOCK, HEAD_DIM), F32)],
        compiler_params=_cparams(("parallel", "arbitrary")),
        name="moba_prompt",
    )(z3, kvm3)


def _conv_pool_sample_kernel(z_ref, cst_ref, pst_ref, dw_ref, dwb_ref, lg_ref, lb_ref, pw_ref, plw_ref, pls_ref,
                             yc_ref, yp_ref, cnew_ref, pnew_ref):
    u = z_ref[:, OFF_UCONV:OFF_UCONV + 2 * GROUP_W]
    zg = u[:, :GROUP_W] * _sigmoid(u[:, GROUP_W:])
    nst = CONV_W - 1
    y = dwb_ref[...] + dw_ref[nst:nst + 1, :] * zg
    for k in range(nst):
        y = y + dw_ref[k:k + 1, :] * cst_ref[k]
    y = _silu(_ln(y, lg_ref[...], lb_ref[...]))
    yc_ref[...] = _dot(y.astype(BF16), pw_ref[...])
    for k in range(nst - 1):
        cnew_ref[k] = cst_ref[k + 1]
    cnew_ref[nst - 1] = zg

    p = z_ref[:, OFF_UPOOL:OFF_UPOOL + GROUP_W]
    npst = POOL_MAX - 1
    lane = lax.broadcasted_iota(jnp.int32, (1, GROUP_W), 1)
    run = p
    mean = jnp.zeros_like(p)
    k = 1
    for g, w in enumerate(POOL_WINDOWS):
        while k < w:
            run = run + pst_ref[npst - k]
            k += 1
        mean = jnp.where((lane >= g * POOL_GW) & (lane < (g + 1) * POOL_GW), run / float(w), mean)
    d = mean - p
    yp_ref[...] = _dot(d.astype(BF16), plw_ref[...]) * pls_ref[...]
    for k in range(npst - 1):
        pnew_ref[k] = pst_ref[k + 1]
    pnew_ref[npst - 1] = p


def _conv_pool_sample(z, cst, pst, dw, dwb, lg, lb, pw, plw, pls):
    s = z.shape[0]
    return pl.pallas_call(
        _conv_pool_sample_kernel,
        out_shape=[jax.ShapeDtypeStruct((s, GROUP_W), F32), jax.ShapeDtypeStruct((s, GROUP_W), F32),
                   jax.ShapeDtypeStruct(cst.shape, F32), jax.ShapeDtypeStruct(pst.shape, F32)],
        compiler_params=pltpu.CompilerParams(vmem_limit_bytes=VMEM_LIMIT),
        name="conv_pool_sample",
    )(z, cst, pst, dw, dwb, lg, lb, pw, plw, pls)


_STREAM_SLOTS = 3
_CMP_PAGES = 16


def _compress_pages_kernel(pt_ref, cache_ref, w_ref, perm_ref, c_ref, buf_ref, rows_ref, sem_ref,
                           *, layer, n_steps, page):
    b = pl.program_id(0)
    s = pl.program_id(1)
    g = _CMP_PAGES
    step = b * n_steps + s
    total = pl.num_programs(0) * n_steps

    def copies(stp, slot):
        bb = stp // n_steps
        ss = stp % n_steps
        return [pltpu.make_async_copy(cache_ref.at[layer, pt_ref[bb, ss * g + k], pl.ds(0, 2)],
                                      buf_ref.at[slot, k], sem_ref.at[slot]) for k in range(g)]

    ahead = _STREAM_SLOTS - 1

    @pl.when(step == 0)
    def _():
        for d in range(ahead):
            @pl.when(d < total)
            def _():
                for cp in copies(d, d):
                    cp.start()

    slot = step % _STREAM_SLOTS

    @pl.when(step + ahead < total)
    def _():
        for cp in copies(step + ahead, (step + ahead) % _STREAM_SLOTS):
            cp.start()

    pltpu.make_async_copy(cache_ref.at[layer, pl.ds(0, g), pl.ds(0, 2)], buf_ref.at[slot], sem_ref.at[slot]).wait()

    nchunk = g * page // CMP_STRIDE
    for k in range(g):
        x = buf_ref[slot, k].reshape(2 * HEAD_DIM, page).astype(BF16)
        rows_ref[k] = _dot(x, perm_ref[...]).T.reshape(CMP_STRIDE, page // CMP_STRIDE, 2 * HEAD_DIM)
    acc = jnp.zeros((nchunk, 4 * HEAD_DIM), F32)
    for r in range(0, CMP_STRIDE, 2):
        xr = jnp.concatenate([rows_ref[:, r].reshape(nchunk, 2 * HEAD_DIM),
                              rows_ref[:, r + 1].reshape(nchunk, 2 * HEAD_DIM)], axis=-1).astype(BF16)
        acc = acc + _dot(xr, w_ref[r // 2])
    c_ref[...] = acc


def _row_perm(page):
    j = np.arange(page)
    src = CMP_STRIDE * (j % (page // CMP_STRIDE)) + j // (page // CMP_STRIDE)
    return jnp.asarray((np.arange(page)[:, None] == src[None, :]).astype(np.float32)).astype(BF16)


def _compress_pages(page_table, cache_t, wstack, layer):
    s, npg = page_table.shape
    page = cache_t.shape[-1]
    g = _CMP_PAGES
    n_steps = npg // g
    nchunk = g * page // CMP_STRIDE
    kern = functools.partial(_compress_pages_kernel, layer=layer, n_steps=n_steps, page=page)
    return pl.pallas_call(
        kern,
        grid_spec=pltpu.PrefetchScalarGridSpec(
            num_scalar_prefetch=1,
            grid=(s, n_steps),
            in_specs=[pl.BlockSpec(memory_space=pl.ANY),
                      pl.BlockSpec(wstack.shape, lambda b, i, pt: (0, 0, 0)),
                      pl.BlockSpec((page, page), lambda b, i, pt: (0, 0))],
            out_specs=pl.BlockSpec((None, nchunk, 4 * HEAD_DIM), lambda b, i, pt: (b, i, 0)),
            scratch_shapes=[pltpu.VMEM((_STREAM_SLOTS, g, 2, HEAD_DIM, page), F32),
                            pltpu.VMEM((g, CMP_STRIDE, page // CMP_STRIDE, 2 * HEAD_DIM), F32),
                            pltpu.SemaphoreType.DMA((_STREAM_SLOTS,))]),
        out_shape=jax.ShapeDtypeStruct((s, n_steps * nchunk, 4 * HEAD_DIM), F32),
        compiler_params=_cparams(("arbitrary", "arbitrary")),
        name="compress_pages",
    )(page_table, cache_t, wstack, _row_perm(page))


def _stack_heads(q):
    rows = [q[:, h * HEAD_DIM:(h + 1) * HEAD_DIM] for h in range(N_HEADS)]
    return jnp.concatenate(rows + [jnp.zeros((8 - N_HEADS, HEAD_DIM), q.dtype)], axis=0)


def _nsa_select_kernel(c_ref, z_ref, pos_ref, w1_ref, w2_ref, cover_ref, ocmp_ref, idx_ref, psum_ref,
                       *, n_cmp, n_sel, qpos):
    bk, bv = _cmp_bias(pos_ref, w1_ref)
    kc, vc = _compress_tail(c_ref[...], bk, bv, w2_ref[0], w2_ref[1])
    q = z_ref[:, OFF_QNSA:OFF_QNSA + GROUP_W]
    q4 = _stack_heads(q)
    nc = kc.shape[0]
    s = _dot_nt(q4.astype(BF16), kc.astype(BF16)) * ATTN_SCALE
    jj = lax.broadcasted_iota(jnp.int32, (1, nc), 1)
    p = _masked_softmax(s, (jj * CMP_STRIDE + (L_CMP - 1) <= qpos) & (jj < n_cmp))
    ocmp_ref[...] = _dot(p.astype(BF16), vc.astype(BF16))[0:N_HEADS]
    b = pl.program_id(0)
    psum_ref[pl.ds(b, 1), :] = jnp.sum(p[0:N_HEADS], axis=0, keepdims=True)

    @pl.when(b == pl.num_programs(0) - 1)
    def _():
        imp = _dot(psum_ref[...], cover_ref[...], precision=HIGHEST)
        ns, nsp = imp.shape
        blk = lax.broadcasted_iota(jnp.int32, (1, nsp), 1)
        fblk = blk.astype(F32)
        cur = qpos // L_SEL
        imp = jnp.where((blk == 0) | (blk == cur) | (blk == cur - 1), jnp.inf, imp)
        imp = jnp.where(blk * L_SEL > qpos, -jnp.inf, imp)
        work = jnp.where(blk < n_sel, jnp.maximum(imp, -3e38), -jnp.inf)
        out_lane = lax.broadcasted_iota(jnp.int32, (1, LANES), 1)
        idx = jnp.zeros((ns, LANES), jnp.int32)
        for t in range(N_SEL):
            mx = jnp.max(work, axis=-1, keepdims=True)
            pick = jnp.min(jnp.where(work == mx, fblk, float(nsp)), axis=-1, keepdims=True)
            idx = jnp.where(out_lane == t, pick.astype(jnp.int32), idx)
            work = jnp.where(fblk == pick, -jnp.inf, work)
        idx_ref[...] = idx


def _nsa_select(c, z3, pos, w1, w2, qpos):
    s, nc, _ = c.shape
    n_cmp = nc - 1
    n_sel = qpos // L_SEL + 1
    nsp = -(-n_sel // LANES) * LANES
    cover = _cover_matrix(nc, n_cmp, nsp)
    kern = functools.partial(_nsa_select_kernel, n_cmp=n_cmp, n_sel=n_sel, qpos=qpos)
    cst = lambda shape: pl.BlockSpec(shape, lambda b: (0,) * len(shape))
    return pl.pallas_call(
        kern,
        grid=(s,),
        in_specs=[pl.BlockSpec((None, nc, 4 * HEAD_DIM), lambda b: (b, 0, 0)),
                  pl.BlockSpec((None, 1, N_IN_PAD), lambda b: (b, 0, 0)),
                  cst(pos.shape), cst(w1.shape), cst(w2.shape), cst(cover.shape)],
        out_specs=[pl.BlockSpec((None, N_HEADS, HEAD_DIM), lambda b: (b, 0, 0)),
                   pl.BlockSpec((s, LANES), lambda b: (0, 0))],
        out_shape=[jax.ShapeDtypeStruct((s, N_HEADS, HEAD_DIM), F32), jax.ShapeDtypeStruct((s, LANES), jnp.int32)],
        scratch_shapes=[pltpu.VMEM((s, nc), F32)],
        compiler_params=_cparams(("arbitrary",)),
        name="nsa_select",
    )(c, z3, pos, w1, w2, cover)


def _col_from_row(row):
    n = row.shape[1]
    eye = lax.broadcasted_iota(jnp.int32, (n, n), 0) == lax.broadcasted_iota(jnp.int32, (n, n), 1)
    return jnp.sum(jnp.where(eye, row, 0.0), axis=-1, keepdims=True)


def _nsa_sample_kernel(pt_ref, idx_ref, cache_ref, z_ref, ocmp_ref, win_ref, o_ref, wnew_ref, buf_ref, sem_ref,
                       *, layer, page, qpos):
    b = pl.program_id(0)
    npg = pt_ref.shape[1]
    per_page = page // L_SEL

    slot = b % 2

    def copy(bb, t, sl):
        pg = jnp.minimum(idx_ref[bb, t] // per_page, npg - 1)
        return pltpu.make_async_copy(cache_ref.at[layer, pt_ref[bb, pg], pl.ds(2, 2)], buf_ref.at[sl, t],
                                     sem_ref.at[sl])

    @pl.when(b == 0)
    def _():
        for t in range(N_SEL):
            copy(b, t, 0).start()

    @pl.when(b + 1 < pl.num_programs(0))
    def _():
        for t in range(N_SEL):
            copy(b + 1, t, 1 - slot).start()

    z = z_ref[...]
    q = z[:, OFF_QNSA:OFF_QNSA + GROUP_W]
    q4b = _stack_heads(q).astype(BF16)
    kvn = z[:, OFF_KVNSA:OFF_KVNSA + GROUP_W]
    ks_new, vs_new = kvn[:, 2 * HEAD_DIM:3 * HEAD_DIM], kvn[:, 3 * HEAD_DIM:4 * HEAD_DIM]
    kvw = z[:, OFF_KVWIN:OFF_KVWIN + 2 * HEAD_DIM]
    kw_new, vw_new = kvw[:, 0:HEAD_DIM], kvw[:, HEAD_DIM:2 * HEAD_DIM]
    q4r = q4b.astype(F32)

    def new_score(k_new):
        kb = k_new.astype(BF16).astype(F32)
        return jnp.sum(q4r * kb, axis=-1, keepdims=True) * ATTN_SCALE

    wk = win_ref[0]
    wv = win_ref[1]
    sw = _dot(q4b, wk.astype(BF16)) * ATTN_SCALE
    sw_new = new_score(kw_new)
    mw = jnp.maximum(jnp.max(sw, axis=-1, keepdims=True), sw_new)
    ew = jnp.exp(sw - mw)
    ew_new = jnp.exp(sw_new - mw)
    dw = jnp.sum(ew, axis=-1, keepdims=True) + ew_new
    pw = ew * (1.0 / dw)
    pw_new = ew_new * (1.0 / dw)
    o_win = _dot_nt(pw.astype(BF16), wv.astype(BF16)) \
        + pw_new.astype(BF16).astype(F32) * vw_new.astype(BF16).astype(F32)
    wl = wk.shape[1]
    lane = lax.broadcasted_iota(jnp.int32, (1, wl), 1)
    wnew_ref[0] = jnp.where(lane == wl - 1, _col_from_row(kw_new), pltpu.roll(wk, wl - 1, 1))
    wnew_ref[1] = jnp.where(lane == wl - 1, _col_from_row(vw_new), pltpu.roll(wv, wl - 1, 1))

    pltpu.make_async_copy(cache_ref.at[layer, pl.ds(0, N_SEL), pl.ds(2, 2)], buf_ref.at[slot],
                          sem_ref.at[slot]).wait()

    n_past = (qpos // L_SEL)
    plane = lax.broadcasted_iota(jnp.int32, (1, page), 1)
    scores = []
    new_taken = jnp.int32(0)
    for t in range(N_SEL):
        bid = idx_ref[b, t]
        in_past = bid < n_past
        half = bid % per_page
        valid = (plane >= half * L_SEL) & (plane < (half + 1) * L_SEL) & in_past
        sc = _dot(q4b, buf_ref[slot, t, 0].astype(BF16)) * ATTN_SCALE
        scores.append(jnp.where(valid, sc, -jnp.inf))
        new_taken = new_taken + (bid == n_past).astype(jnp.int32)
    has_new = new_taken > 0
    ss_new = jnp.where(has_new, new_score(ks_new), -jnp.inf)
    ms = ss_new
    for sc in scores:
        ms = jnp.maximum(ms, jnp.max(sc, axis=-1, keepdims=True))
    ms = jnp.where(ms == -jnp.inf, 0.0, ms)
    es_new = jnp.exp(ss_new - ms)
    ds = es_new
    es = []
    for sc in scores:
        e = jnp.exp(sc - ms)
        es.append(e)
        ds = ds + jnp.sum(e, axis=-1, keepdims=True)
    inv = 1.0 / jnp.where(ds > 0, ds, 1.0)
    o_sel = (es_new * inv).astype(BF16).astype(F32) * vs_new.astype(BF16).astype(F32)
    for t in range(N_SEL):
        o_sel = o_sel + _dot_nt((es[t] * inv).astype(BF16), buf_ref[slot, t, 1].astype(BF16))

    g = _sigmoid(z[:, OFF_GATE:OFF_GATE + LANES])
    o_cmp = ocmp_ref[...]
    for h in range(N_HEADS):
        oh = (g[:, h:h + 1] * o_cmp[h:h + 1] + g[:, N_HEADS + h:N_HEADS + h + 1] * o_sel[h:h + 1]
              + g[:, 2 * N_HEADS + h:2 * N_HEADS + h + 1] * o_win[h:h + 1])
        o_ref[:, h * HEAD_DIM:(h + 1) * HEAD_DIM] = oh


def _nsa_sample(page_table, idx, cache_t, z3, ocmp, win_t, layer, qpos):
    s = z3.shape[0]
    page = cache_t.shape[-1]
    wl = win_t.shape[-1]
    kern = functools.partial(_nsa_sample_kernel, layer=layer, page=page, qpos=qpos)
    return pl.pallas_call(
        kern,
        grid_spec=pltpu.PrefetchScalarGridSpec(
            num_scalar_prefetch=2,
            grid=(s,),
            in_specs=[pl.BlockSpec(memory_space=pl.ANY),
                      pl.BlockSpec((None, 1, N_IN_PAD), lambda b, pt, ix: (b, 0, 0)),
                      pl.BlockSpec((None, N_HEADS, HEAD_DIM), lambda b, pt, ix: (b, 0, 0)),
                      pl.BlockSpec((None, None, 2, HEAD_DIM, wl), lambda b, pt, ix: (layer, b, 0, 0, 0))],
            out_specs=[pl.BlockSpec((None, 1, GROUP_W), lambda b, pt, ix: (b, 0, 0)),
                       pl.BlockSpec((None, 2, HEAD_DIM, wl), lambda b, pt, ix: (b, 0, 0, 0))],
            scratch_shapes=[pltpu.VMEM((2, N_SEL, 2, HEAD_DIM, page), F32), pltpu.SemaphoreType.DMA((2,))]),
        out_shape=[jax.ShapeDtypeStruct((s, 1, GROUP_W), F32), jax.ShapeDtypeStruct((s, 2, HEAD_DIM, wl), F32)],
        compiler_params=_cparams(("arbitrary",)),
        name="nsa_sample",
    )(page_table, idx, cache_t, z3, ocmp, win_t)


_MOBA_PAGES = 16


def _moba_gate_kernel(pt_ref, cache_ref, qcol_ref, idx_ref, buf_ref, part_ref, sem_ref,
                      *, layer, n_steps, page, nb, cur):
    b = pl.program_id(0)
    s = pl.program_id(1)
    g = _MOBA_PAGES
    step = b * n_steps + s
    total = pl.num_programs(0) * n_steps
    ppb = MOBA_BLOCK // page
    bps = g // ppb

    def copies(stp, slot):
        bb = stp // n_steps
        ss = stp % n_steps
        return [pltpu.make_async_copy(cache_ref.at[layer, pt_ref[bb, ss * g + k], 0],
                                      buf_ref.at[slot, k], sem_ref.at[slot]) for k in range(g)]

    ahead = _STREAM_SLOTS - 1

    @pl.when(step == 0)
    def _():
        for d in range(ahead):
            @pl.when(d < total)
            def _():
                for cp in copies(d, d):
                    cp.start()

    slot = step % _STREAM_SLOTS

    @pl.when(step + ahead < total)
    def _():
        for cp in copies(step + ahead, (step + ahead) % _STREAM_SLOTS):
            cp.start()

    pltpu.make_async_copy(cache_ref.at[layer, pl.ds(0, g), 0], buf_ref.at[slot], sem_ref.at[slot]).wait()

    for j in range(bps):
        for h in range(N_HEADS):
            qh = qcol_ref[h * HEAD_DIM:(h + 1) * HEAD_DIM, :]
            acc = buf_ref[slot, j * ppb, h] * qh
            for k in range(1, ppb):
                acc = acc + buf_ref[slot, j * ppb + k, h] * qh
            part_ref[h * nb + s * bps + j] = jnp.sum(acc.reshape(HEAD_DIM // 8, 8, page), axis=0)

    @pl.when(s == n_steps - 1)
    def _():
        ones = jnp.ones((8, page), F32)
        part = jnp.sum(part_ref[...], axis=1)
        gate = _dot_nt(ones, part, precision=HIGHEST)[0:1] * (1.0 / MOBA_BLOCK)
        lane = lax.broadcasted_iota(jnp.int32, gate.shape, 1)
        out_lane = lax.broadcasted_iota(jnp.int32, (1, LANES), 1)
        idx = jnp.zeros((1, LANES), jnp.int32)
        for h in range(N_HEADS):
            n_of = lane - h * nb
            f_of = n_of.astype(F32)
            inh = (n_of >= 0) & (n_of < nb) & (n_of < cur)
            work = jnp.where(inh, jnp.maximum(gate, -3e38), -jnp.inf)
            for t in range(MOBA_TOPK):
                mx = jnp.max(work, axis=-1, keepdims=True)
                pick = jnp.min(jnp.where((work == mx) & inh, f_of, float(4 * nb)), axis=-1, keepdims=True)
                idx = jnp.where(out_lane == h * MOBA_TOPK + t, pick.astype(jnp.int32), idx)
                work = jnp.where(f_of == pick, -jnp.inf, work)
        idx_ref[...] = idx


def _moba_gate(page_table, cache_t, qcol, layer, cur):
    s, npg = page_table.shape
    page = cache_t.shape[-1]
    g = _MOBA_PAGES
    n_steps = npg // g
    nb = npg * page // MOBA_BLOCK
    kern = functools.partial(_moba_gate_kernel, layer=layer, n_steps=n_steps, page=page, nb=nb, cur=cur)
    return pl.pallas_call(
        kern,
        grid_spec=pltpu.PrefetchScalarGridSpec(
            num_scalar_prefetch=1,
            grid=(s, n_steps),
            in_specs=[pl.BlockSpec(memory_space=pl.ANY),
                      pl.BlockSpec((None, N_HEADS * HEAD_DIM, page), lambda b, i, pt: (b, 0, 0))],
            out_specs=pl.BlockSpec((None, 1, LANES), lambda b, i, pt: (b, 0, 0)),
            scratch_shapes=[pltpu.VMEM((_STREAM_SLOTS, g, N_HEADS, HEAD_DIM, page), F32),
                            pltpu.VMEM((N_HEADS * nb, 8, page), F32),
                            pltpu.SemaphoreType.DMA((_STREAM_SLOTS,))]),
        out_shape=jax.ShapeDtypeStruct((s, 1, LANES), jnp.int32),
        compiler_params=_cparams(("arbitrary", "arbitrary")),
        name="moba_gate",
    )(page_table, cache_t, qcol)


def _moba_sample_kernel(pt_ref, idx_ref, cache_ref, z_ref, o_ref, buf_ref, sem_ref, *, layer, page, cur):
    b = pl.program_id(0)
    npg = pt_ref.shape[1]
    ppb = MOBA_BLOCK // page
    n_src = MOBA_TOPK * ppb

    slot = b % 2

    def copy(bb, sl, h, t, k, kv):
        blk = jnp.minimum(idx_ref[bb, h * MOBA_TOPK + t], npg // ppb - 1)
        return pltpu.make_async_copy(cache_ref.at[layer, pt_ref[bb, blk * ppb + k], kv, h],
                                     buf_ref.at[sl, t * ppb + k, kv, h], sem_ref.at[sl])

    every = [(h, t, k, kv) for h in range(N_HEADS) for t in range(MOBA_TOPK) for k in range(ppb) for kv in range(2)]

    @pl.when(b == 0)
    def _():
        for a in every:
            copy(b, 0, *a).start()

    @pl.when(b + 1 < pl.num_programs(0))
    def _():
        for a in every:
            copy(b + 1, 1 - slot, *a).start()

    z = z_ref[...]
    q = z[:, OFF_QMOBA:OFF_QMOBA + GROUP_W]
    k_new = z[:, OFF_KVMOBA:OFF_KVMOBA + GROUP_W]
    v_new = z[:, OFF_KVMOBA + GROUP_W:OFF_KVMOBA + 2 * GROUP_W]
    pltpu.make_async_copy(cache_ref.at[layer, pl.ds(0, n_src)], buf_ref.at[slot], sem_ref.at[slot]).wait()

    for h in range(N_HEADS):
        lo, hi = h * HEAD_DIM, (h + 1) * HEAD_DIM
        qh = jnp.concatenate([q[:, lo:hi]] * 8, axis=0).astype(BF16)
        qr = q[:, lo:hi].astype(BF16).astype(F32)
        s_new = jnp.sum(qr * k_new[:, lo:hi].astype(BF16).astype(F32), axis=-1, keepdims=True) * ATTN_SCALE
        scores = []
        m = s_new
        for t in range(MOBA_TOPK):
            ok = idx_ref[b, h * MOBA_TOPK + t] < cur
            for k in range(ppb):
                sc = _dot(qh, buf_ref[slot, t * ppb + k, 0, h].astype(BF16))[0:1] * ATTN_SCALE
                sc = jnp.where(ok, sc, -jnp.inf)
                scores.append(sc)
                m = jnp.maximum(m, jnp.max(sc, axis=-1, keepdims=True))
        e_new = jnp.exp(s_new - m)
        d = e_new
        es = []
        for sc in scores:
            e = jnp.exp(sc - m)
            es.append(e)
            d = d + jnp.sum(e, axis=-1, keepdims=True)
        inv = 1.0 / d
        o = (e_new * inv).astype(BF16).astype(F32) * v_new[:, lo:hi].astype(BF16).astype(F32)
        for n in range(n_src):
            pn = jnp.concatenate([es[n] * inv] * 8, axis=0).astype(BF16)
            o = o + _dot_nt(pn, buf_ref[slot, n, 1, h].astype(BF16))[0:1]
        o_ref[:, lo:hi] = o


def _moba_sample(page_table, idx, cache_t, z3, layer, cur):
    s = z3.shape[0]
    page = cache_t.shape[-1]
    ppb = MOBA_BLOCK // page
    kern = functools.partial(_moba_sample_kernel, layer=layer, page=page, cur=cur)
    return pl.pallas_call(
        kern,
        grid_spec=pltpu.PrefetchScalarGridSpec(
            num_scalar_prefetch=2,
            grid=(s,),
            in_specs=[pl.BlockSpec(memory_space=pl.ANY),
                      pl.BlockSpec((None, 1, N_IN_PAD), lambda b, pt, ix: (b, 0, 0))],
            out_specs=pl.BlockSpec((None, 1, GROUP_W), lambda b, pt, ix: (b, 0, 0)),
            scratch_shapes=[pltpu.VMEM((2, MOBA_TOPK * ppb, 2, N_HEADS, HEAD_DIM, page), F32),
                            pltpu.SemaphoreType.DMA((2,))]),
        out_shape=jax.ShapeDtypeStruct((s, 1, GROUP_W), F32),
        compiler_params=_cparams(("arbitrary",)),
        name="moba_sample",
    )(page_table, idx, cache_t, z3)


def _cmp_weight_stack(w1):
    w = w1.reshape(2, 2, CMP_STRIDE, HEAD_DIM, HEAD_DIM)
    zero = jnp.zeros((CMP_STRIDE, HEAD_DIM, 2 * HEAD_DIM), F32)
    top = jnp.concatenate([w[0, 0], w[0, 1], zero], axis=-1)
    bot = jnp.concatenate([zero, w[1, 0], w[1, 1]], axis=-1)
    per_r = jnp.concatenate([top, bot], axis=1)
    return per_r.reshape(CMP_STRIDE // 2, 4 * HEAD_DIM, 4 * HEAD_DIM).astype(BF16)


def kernel(x_prompt, x_sample, cache_moba_kv, cache_nsa_kv, state_nsa_win, state_conv, state_pool, page_table,
           w_in, conv_dw, conv_dw_b, conv_ln_g, conv_ln_b, conv_pw, nsa_cmp_pos, nsa_cmp_w1, nsa_cmp_w2,
           pool_w, pool_scale, w_out, ln1_g, ln1_b, w_up, w_down, ln2_g, ln2_b):
    bp, t, _ = x_prompt.shape
    bs = x_sample.shape[0]
    depth = w_in.shape[0]
    page = cache_nsa_kv.shape[2]
    past_len = page_table.shape[1] * page
    win_len = state_nsa_win.shape[2]
    assert x_sample.shape[1] == 1 and win_len == WINDOW and t % 512 == 0 and t >= WINDOW + 128
    assert page % L_SEL == 0 and MOBA_BLOCK % page == 0 and past_len % MOBA_BLOCK == 0
    assert past_len // MOBA_BLOCK >= MOBA_TOPK and past_len // L_SEL + 1 >= N_SEL and past_len >= POOL_MAX
    assert page_table.shape[1] % _CMP_PAGES == 0 and page_table.shape[1] % _MOBA_PAGES == 0

    nsa_t = jnp.transpose(cache_nsa_kv, (0, 1, 3, 4, 2))
    moba_t = jnp.transpose(cache_moba_kv, (0, 1, 3, 4, 5, 2))
    win_t = jnp.transpose(state_nsa_win, (0, 1, 3, 4, 2))
    conv_t = jnp.transpose(state_conv, (0, 2, 1, 3))
    pool_t = jnp.transpose(state_pool, (0, 2, 1, 3))

    row = lambda v: v.reshape(1, -1)
    yp = x_prompt.reshape(bp * t, D_MODEL)
    ys = x_sample.reshape(bs, D_MODEL)
    st_p, st_s = [], []
    w_out_b, w_up_b, w_down_b = w_out.astype(BF16), w_up.astype(BF16), w_down.astype(BF16)
    for l in range(depth):
        w_in_l = _in_weights(w_in[l])
        pw = conv_pw[l].astype(BF16)
        plw = jax.scipy.linalg.block_diag(*[pool_w[l, g] for g in range(len(POOL_WINDOWS))]).astype(BF16)
        w1 = nsa_cmp_w1[l].astype(BF16)
        w2 = nsa_cmp_w2[l].astype(BF16)
        pos = jnp.broadcast_to(nsa_cmp_pos[l].reshape(2, 1, L_CMP * HEAD_DIM), (2, 8, L_CMP * HEAD_DIM))
        small = (conv_dw[l], row(conv_dw_b[l]), row(conv_ln_g[l]), row(conv_ln_b[l]), pw, plw, row(pool_scale[l]))

        z, kvm, kvn, kvw = _in_proj_split(yp, w_in_l)
        z3 = z.reshape(bp, t, N_MAIN)
        kvm3 = kvm.reshape(bp, t, 2 * GROUP_W)
        kvn3 = kvn.reshape(bp, t, GROUP_W)
        win3 = kvw.reshape(bp, t, 2 * HEAD_DIM)
        y_conv, y_pool, conv_new32 = _conv_pool_prompt(z3, *small)
        nchunk = t // CMP_STRIDE
        chunks_k = kvn3[:, :, 0:HEAD_DIM].reshape(bp, nchunk, CMP_STRIDE * HEAD_DIM)
        chunks_v = kvn3[:, :, HEAD_DIM:2 * HEAD_DIM].reshape(bp, nchunk, CMP_STRIDE * HEAD_DIM)
        kc, vc = _compress_prompt(chunks_k, chunks_v, pos, w1, w2)
        o_nsa = _nsa_prompt(z3, kvn3, win3, kc, vc)
        o_moba = _moba_prompt(z3, kvm3)
        flat = lambda a: a.reshape(bp * t, GROUP_W)
        x1 = _out_proj((flat(y_conv), flat(o_nsa), flat(o_moba), flat(y_pool)), yp, w_out_b, l,
                       row(ln1_g[l]), row(ln1_b[l]))
        yp = _ffn(x1, w_up_b, w_down_b, l, row(ln2_g[l]), row(ln2_b[l]))
        moba_rows, nsa_rows = kvm3, kvn3
        win_new = win3[:, t - win_len:].reshape(bp, win_len, 2, HEAD_DIM)
        conv_new = conv_new32[:, _CONV_HALO - (CONV_W - 1):]
        pool_new = z3[:, t - (POOL_MAX - 1):, OFF_UPOOL:OFF_UPOOL + GROUP_W]
        st_p.append((moba_rows, nsa_rows, win_new, conv_new, pool_new))

        zs = _in_proj(ys, w_in_l)
        zs3 = zs.reshape(bs, 1, N_IN_PAD)
        ys_conv, ys_pool, conv_new_t, pool_new_t = _conv_pool_sample(zs, conv_t[l], pool_t[l], *small)
        c = _compress_pages(page_table, nsa_t, _cmp_weight_stack(nsa_cmp_w1[l]), l)
        o_cmp, sel_idx = _nsa_select(c, zs3, pos, w1, w2, past_len)
        os_nsa, win_new_t = _nsa_sample(page_table, sel_idx, nsa_t, zs3, o_cmp, win_t, l, past_len)
        q_m = zs[:, OFF_QMOBA:OFF_QMOBA + GROUP_W]
        qcol = jnp.broadcast_to(q_m[:, :, None], (bs, GROUP_W, page))
        top_idx = _moba_gate(page_table, moba_t, qcol, l, past_len // MOBA_BLOCK)
        os_moba = _moba_sample(page_table, top_idx.reshape(bs, LANES), moba_t, zs3, l, past_len // MOBA_BLOCK)
        x1s = _out_proj((ys_conv, os_nsa.reshape(bs, GROUP_W), os_moba.reshape(bs, GROUP_W), ys_pool), ys, w_out_b, l,
                        row(ln1_g[l]), row(ln1_b[l]))
        ys = _ffn(x1s, w_up_b, w_down_b, l, row(ln2_g[l]), row(ln2_b[l]))
        st_s.append((zs[:, OFF_KVMOBA:OFF_KVMOBA + 2 * GROUP_W].reshape(bs, 1, 2, N_HEADS, HEAD_DIM),
                     zs[:, OFF_KVNSA:OFF_KVNSA + GROUP_W].reshape(bs, 1, 4, HEAD_DIM),
                     jnp.transpose(win_new_t, (0, 3, 1, 2)),
                     jnp.transpose(conv_new_t, (1, 0, 2)),
                     jnp.transpose(pool_new_t, (1, 0, 2))))

    stk = lambda sts, i: jnp.stack([s[i] for s in sts], axis=0)
    moba_state = _kv_rows_to_state([s[0] for s in st_p]).reshape(depth, bp, 2, N_HEADS, HEAD_DIM, t)
    nsa_state = _kv_rows_to_state([s[1] for s in st_p]).reshape(depth, bp, 4, HEAD_DIM, t)
    return (yp.reshape(bp, t, D_MODEL), ys.reshape(bs, 1, D_MODEL),
            jnp.transpose(moba_state, (0, 1, 5, 2, 3, 4)), stk(st_s, 0),
            jnp.transpose(nsa_state, (0, 1, 4, 2, 3)), stk(st_s, 1), stk(st_p, 2), stk(st_s, 2),
            stk(st_p, 3), stk(st_s, 3), stk(st_p, 4), stk(st_s, 4))
```

```python
import functools

import numpy as np
import jax
import jax.numpy as jnp
from jax import lax
from jax.experimental import pallas as pl
from jax.experimental.pallas import tpu as pltpu

F32 = jnp.float32
BF16 = jnp.bfloat16
HIGHEST = lax.Precision.HIGHEST

D_MODEL = 1024
HEAD_DIM = 64
GROUP_W = 256
N_HEADS = 4
CONV_W = 31
L_CMP = 32
CMP_STRIDE = 16
L_SEL = 64
N_SEL = 16
WINDOW = 512
MOBA_BLOCK = 256
MOBA_TOPK = 3
POOL_WINDOWS = (2, 4, 8, 16)
POOL_GW = 64
POOL_MAX = 16
D_FF = 4096
DEPTH = 2
ALPHA = (2 * DEPTH) ** 0.25
LN_EPS = 1e-5
ATTN_SCALE = HEAD_DIM ** -0.5
NEG = -1e30

OFF_UCONV, OFF_QNSA, OFF_QMOBA, OFF_UPOOL, OFF_GATE = 0, 512, 768, 1024, 1280
N_MAIN = 1408
OFF_KVMOBA, OFF_KVNSA, OFF_KVWIN = 1408, 1920, 2176
N_IN_PAD = 2304
N_GATE = 12
LANES = 128
VMEM_LIMIT = 56 * 1024 * 1024


def _in_weights(w):
    main = np.concatenate([np.arange(0, 512), np.arange(512, 768), np.arange(1164, 1420), np.arange(1932, 2188),
                           np.arange(1152, 1164)])
    kv = np.concatenate([np.arange(1420, 1932), np.arange(768, 1024), np.arange(1024, 1152)])
    pad = jnp.zeros((w.shape[0], LANES - N_GATE), w.dtype)
    return jnp.concatenate([w[:, main], pad, w[:, kv]], axis=1).astype(BF16)


def _ln(x, g, b):
    mu = jnp.mean(x, axis=-1, keepdims=True)
    xc = x - mu
    var = jnp.mean(xc * xc, axis=-1, keepdims=True)
    return xc * lax.rsqrt(var + LN_EPS) * g + b


def _sigmoid(x):
    return 1.0 / (1.0 + jnp.exp(-x))


def _silu(x):
    return x * _sigmoid(x)


def _dot(a, b, **kw):
    return jnp.dot(a, b, preferred_element_type=F32, **kw)


def _dot_nt(a, b, **kw):
    return lax.dot_general(a, b, (((1,), (1,)), ((), ())), preferred_element_type=F32, **kw)


def _masked_softmax(s, mask):
    r = s.shape[0]
    s = jnp.where(mask, s, -jnp.inf)
    m = jnp.max(_lane_partial(s, jnp.maximum), axis=-1, keepdims=True)
    m = jnp.broadcast_to(jnp.where(m == -jnp.inf, 0.0, m), (r, LANES))
    e = _exp_shifted(s, m)
    d = jnp.sum(_lane_partial(e, jnp.add), axis=-1, keepdims=True)
    inv = jnp.broadcast_to(1.0 / jnp.where(d > 0, d, 1.0), (r, LANES))
    return jnp.concatenate([e[:, g * LANES:(g + 1) * LANES] * inv for g in range(s.shape[1] // LANES)], axis=-1)


def _lane_partial(s, op):
    out = s[:, 0:LANES]
    for g in range(1, s.shape[1] // LANES):
        out = op(out, s[:, g * LANES:(g + 1) * LANES])
    return out


def _exp_shifted(s, m):
    return jnp.concatenate([jnp.exp(s[:, g * LANES:(g + 1) * LANES] - m) for g in range(s.shape[1] // LANES)], axis=-1)


def _sweep(count, tiles_fn, width=2):
    def wide(jj, carry):
        tiles_fn(width * jj, width)
        return carry

    lax.fori_loop(0, count // width, wide, 0)
    done = (count // width) * width
    n = width // 2
    while n >= 1:
        take = ((count - done) // n) % 2 == 1

        @pl.when(take)
        def _(done=done, n=n):
            tiles_fn(done, n)

        done = done + jnp.where(take, n, 0)
        n //= 2


def _cparams(sem, vmem=None):
    return pltpu.CompilerParams(dimension_semantics=sem, vmem_limit_bytes=vmem or VMEM_LIMIT)


def _in_proj_kernel(x_ref, w_ref, o_ref):
    o_ref[...] = _dot(x_ref[...].astype(BF16), w_ref[...])


def _in_proj(x, w):
    m, k = x.shape
    n = w.shape[1]
    tm = min(m, 512)
    return pl.pallas_call(
        _in_proj_kernel,
        grid=(m // tm,),
        in_specs=[pl.BlockSpec((tm, k), lambda i: (i, 0)), pl.BlockSpec((k, n), lambda i: (0, 0))],
        out_specs=pl.BlockSpec((tm, n), lambda i: (i, 0)),
        out_shape=jax.ShapeDtypeStruct((m, n), F32),
        compiler_params=_cparams(("parallel",)),
        name="in_proj",
    )(x, w)


_IN_PARTS = ((0, N_MAIN), (OFF_KVMOBA, OFF_KVNSA), (OFF_KVNSA, OFF_KVWIN), (OFF_KVWIN, N_IN_PAD))


def _in_proj_split_kernel(x_ref, w_ref, *o_refs):
    z = _dot(x_ref[...].astype(BF16), w_ref[...])
    for (lo, hi), o_ref in zip(_IN_PARTS, o_refs):
        o_ref[...] = z[:, lo:hi]


def _in_proj_split(x, w):
    m, k = x.shape
    tm = min(m, 512)
    return pl.pallas_call(
        _in_proj_split_kernel,
        grid=(m // tm,),
        in_specs=[pl.BlockSpec((tm, k), lambda i: (i, 0)), pl.BlockSpec(w.shape, lambda i: (0, 0))],
        out_specs=[pl.BlockSpec((tm, hi - lo), lambda i: (i, 0)) for lo, hi in _IN_PARTS],
        out_shape=[jax.ShapeDtypeStruct((m, hi - lo), F32) for lo, hi in _IN_PARTS],
        compiler_params=_cparams(("parallel",)),
        name="in_proj_split",
    )(x, w)


def _out_proj_kernel(a_ref, b_ref, c_ref, d_ref, x_ref, w_ref, g_ref, bt_ref, o_ref):
    mix = jnp.concatenate([a_ref[...], b_ref[...], c_ref[...], d_ref[...]], axis=-1).astype(BF16)
    y = _dot(mix, w_ref[...])
    o_ref[...] = _ln(ALPHA * x_ref[...] + y, g_ref[...], bt_ref[...])


def _out_proj(pieces, x, w, layer, g, b):
    m = x.shape[0]
    tm = min(m, 512)
    pspec = pl.BlockSpec((tm, GROUP_W), lambda i: (i, 0))
    return pl.pallas_call(
        _out_proj_kernel,
        grid=(m // tm,),
        in_specs=[pspec, pspec, pspec, pspec,
                  pl.BlockSpec((tm, D_MODEL), lambda i: (i, 0)),
                  pl.BlockSpec((None, D_MODEL, D_MODEL), lambda i: (layer, 0, 0)),
                  pl.BlockSpec((1, D_MODEL), lambda i: (0, 0)),
                  pl.BlockSpec((1, D_MODEL), lambda i: (0, 0))],
        out_specs=pl.BlockSpec((tm, D_MODEL), lambda i: (i, 0)),
        out_shape=jax.ShapeDtypeStruct((m, D_MODEL), F32),
        compiler_params=_cparams(("parallel",)),
        name="out_proj_ln",
    )(*pieces, x, w, g, b)


def _ffn_kernel(x_ref, wu_ref, wd_ref, g_ref, b_ref, o_ref, acc_ref):
    j = pl.program_id(1)

    @pl.when(j == 0)
    def _():
        acc_ref[...] = jnp.zeros_like(acc_ref)

    h = jnp.maximum(_dot(x_ref[...].astype(BF16), wu_ref[...]), 0.0)
    acc_ref[...] += _dot((h * h).astype(BF16), wd_ref[...])

    @pl.when(j == pl.num_programs(1) - 1)
    def _():
        o_ref[...] = _ln(ALPHA * x_ref[...] + acc_ref[...], g_ref[...], b_ref[...])


def _ffn(x, wu, wd, layer, g, b):
    m = x.shape[0]
    tm = min(m, 1024)
    tf = 1024
    return pl.pallas_call(
        _ffn_kernel,
        grid=(m // tm, D_FF // tf),
        in_specs=[pl.BlockSpec((tm, D_MODEL), lambda i, j: (i, 0)),
                  pl.BlockSpec((None, D_MODEL, tf), lambda i, j: (layer, 0, j)),
                  pl.BlockSpec((None, tf, D_MODEL), lambda i, j: (layer, j, 0)),
                  pl.BlockSpec((1, D_MODEL), lambda i, j: (0, 0)),
                  pl.BlockSpec((1, D_MODEL), lambda i, j: (0, 0))],
        out_specs=pl.BlockSpec((tm, D_MODEL), lambda i, j: (i, 0)),
        out_shape=jax.ShapeDtypeStruct((m, D_MODEL), F32),
        scratch_shapes=[pltpu.VMEM((tm, D_MODEL), F32)],
        compiler_params=_cparams(("parallel", "arbitrary")),
        name="ffn_ln",
    )(x, wu, wd, g, b)


def _kv_rows_to_state_kernel(*refs):
    layer = pl.program_id(0)
    o_ref = refs[-1]
    for l, x_ref in enumerate(refs[:-1]):
        @pl.when(layer == l)
        def _():
            o_ref[...] = x_ref[...].T


def _kv_rows_to_state(per_layer):
    depth = len(per_layer)
    b, t, width = per_layer[0].shape
    tile = 1024 if t % 1024 == 0 else 512
    nt = t // tile

    def in_spec(l):
        def index(ll, bi, i):
            before, after = ll < l, ll > l
            return (jnp.where(before, 0, jnp.where(after, b - 1, bi)),
                    jnp.where(before, 0, jnp.where(after, nt - 1, i)), 0)
        return pl.BlockSpec((None, tile, width), index)

    return pl.pallas_call(
        _kv_rows_to_state_kernel,
        grid=(depth, b, nt),
        in_specs=[in_spec(l) for l in range(depth)],
        out_specs=pl.BlockSpec((None, None, width, tile), lambda ll, bi, i: (ll, bi, 0, i)),
        out_shape=jax.ShapeDtypeStruct((depth, b, width, t), F32),
        compiler_params=_cparams(("arbitrary", "arbitrary", "arbitrary")),
        name="kv_rows_to_state",
    )(*per_layer)


_CONV_HALO = 32
_POOL_HALO = 16


def _pool_groups(rows_ref, buf_ref, tt, cnt_fn):
    n = _POOL_HALO + tt
    lane = lax.broadcasted_iota(jnp.int32, (1, GROUP_W), 1)
    cur = rows_ref[pl.ds(_POOL_HALO, tt), :]
    out = jnp.zeros((tt, GROUP_W), F32)
    shift = 1
    for g, w in enumerate(POOL_WINDOWS):
        src = rows_ref if g == 0 else buf_ref.at[(g - 1) % 2]
        dst = buf_ref.at[g % 2]
        lo = 2 * shift - 1
        dst[pl.ds(lo, n - lo), :] = src[pl.ds(lo, n - lo), :] + src[pl.ds(lo - shift, n - lo), :]
        shift *= 2
        s = dst[pl.ds(_POOL_HALO, tt), :]
        mean = s / cnt_fn(w)
        out = jnp.where((lane >= g * POOL_GW) & (lane < (g + 1) * POOL_GW), mean, out)
    return out - cur


def _conv_pool_prompt_kernel(u_ref, uh_ref, p_ref, ph_ref, dw_ref, dwb_ref, lg_ref, lb_ref, pw_ref,
                             plw_ref, pls_ref, yc_ref, yp_ref, cnew_ref, zc_ref, res_ref, rows_ref, buf_ref, *, tt):
    i = pl.program_id(1)
    first = i == 0
    u = u_ref[...]
    zg = u[:, :GROUP_W] * _sigmoid(u[:, GROUP_W:])
    uh = uh_ref[...]
    zh = uh[:, :GROUP_W] * _sigmoid(uh[:, GROUP_W:])
    zc_ref[pl.ds(0, _CONV_HALO), :] = jnp.where(first, 0.0, zh)
    zc_ref[pl.ds(_CONV_HALO, tt), :] = zg
    zc_ref[pl.ds(_CONV_HALO + tt, 8), :] = jnp.zeros((8, GROUP_W), F32)
    off = _CONV_HALO - (CONV_W - 1)
    y = jnp.zeros((tt, GROUP_W), F32) + dwb_ref[...]
    for r in range(8):
        acc = None
        for a in range(-(-(off + CONV_W) // 8)):
            k = 8 * a + r - off
            if 0 <= k < CONV_W:
                term = dw_ref[k:k + 1, :] * zc_ref[pl.ds(8 * a, tt + 8), :]
                acc = term if acc is None else acc + term
        if r == 0:
            y = y + acc[0:tt]
        else:
            res_ref[r - 1] = acc
            y = y + res_ref[r - 1, pl.ds(r, tt), :]
    y = _silu(_ln(y, lg_ref[...], lb_ref[...]))
    yc_ref[...] = _dot(y.astype(BF16), pw_ref[...]).astype(yc_ref.dtype)

    @pl.when(i == pl.num_programs(1) - 1)
    def _():
        cnew_ref[...] = zc_ref[pl.ds(tt, _CONV_HALO), :]

    rows_ref[pl.ds(0, _POOL_HALO), :] = jnp.where(first, 0.0, ph_ref[...])
    rows_ref[pl.ds(_POOL_HALO, tt), :] = p_ref[...]
    qpos1 = i * tt + lax.broadcasted_iota(jnp.int32, (tt, 1), 0) + 1
    d = _pool_groups(rows_ref, buf_ref, tt, lambda w: jnp.minimum(w, qpos1).astype(F32))
    yp_ref[...] = (_dot(d.astype(BF16), plw_ref[...]) * pls_ref[...]).astype(yp_ref.dtype)


def _conv_pool_prompt(z3, dw, dwb, lg, lb, pw, plw, pls):
    b, t, _ = z3.shape
    tt = 512
    nt = t // tt
    kern = functools.partial(_conv_pool_prompt_kernel, tt=tt)
    cst = lambda shape: pl.BlockSpec(shape, lambda bi, i: (0,) * len(shape))
    return pl.pallas_call(
        kern,
        grid=(b, nt),
        in_specs=[
            pl.BlockSpec((None, tt, 512), lambda bi, i: (bi, i, OFF_UCONV // 512)),
            pl.BlockSpec((None, _CONV_HALO, 512),
                         lambda bi, i: (bi, jnp.maximum(i * (tt // _CONV_HALO) - 1, 0), OFF_UCONV // 512)),
            pl.BlockSpec((None, tt, GROUP_W), lambda bi, i: (bi, i, OFF_UPOOL // GROUP_W)),
            pl.BlockSpec((None, _POOL_HALO, GROUP_W),
                         lambda bi, i: (bi, jnp.maximum(i * (tt // _POOL_HALO) - 1, 0), OFF_UPOOL // GROUP_W)),
            cst((CONV_W, GROUP_W)), cst((1, GROUP_W)), cst((1, GROUP_W)), cst((1, GROUP_W)),
            cst((GROUP_W, GROUP_W)), cst((GROUP_W, GROUP_W)), cst((1, GROUP_W)),
        ],
        out_specs=[
            pl.BlockSpec((None, tt, GROUP_W), lambda bi, i: (bi, i, 0)),
            pl.BlockSpec((None, tt, GROUP_W), lambda bi, i: (bi, i, 0)),
            pl.BlockSpec((None, _CONV_HALO, GROUP_W), lambda bi, i: (bi, 0, 0)),
        ],
        out_shape=[jax.ShapeDtypeStruct((b, t, GROUP_W), BF16), jax.ShapeDtypeStruct((b, t, GROUP_W), BF16),
                   jax.ShapeDtypeStruct((b, _CONV_HALO, GROUP_W), F32)],
        scratch_shapes=[pltpu.VMEM((_CONV_HALO + tt + 8, GROUP_W), F32),
                        pltpu.VMEM((7, tt + 8, GROUP_W), F32),
                        pltpu.VMEM((_POOL_HALO + tt, GROUP_W), F32),
                        pltpu.VMEM((2, _POOL_HALO + tt, GROUP_W), F32)],
        compiler_params=_cparams(("parallel", "arbitrary")),
        name="conv_pool_prompt",
    )(z3, z3, z3, z3, dw, dwb, lg, lb, pw, plw, pls)


def _compress_tail(c, bias_k, bias_v, w2k, w2v):
    n = c.shape[0]

    def one(c0, c1, bias, w2):
        hid = c0 + pltpu.roll(c1, n - 1, 0) + bias
        return _dot(_silu(hid).astype(BF16), w2)

    kc = one(c[:, 0:64], c[:, 64:128], bias_k, w2k)
    vc = one(c[:, 128:192], c[:, 192:256], bias_v, w2v)
    return kc, vc


def _cmp_bias(pos_ref, w1_ref):
    bk = _dot(pos_ref[0].astype(BF16), w1_ref[0])[0:1]
    bv = _dot(pos_ref[1].astype(BF16), w1_ref[1])[0:1]
    return bk, bv


def _compress_prompt_kernel(kv_ref, pos_ref, w1_ref, w2_ref, ws_ref, kc_ref, vc_ref):
    bk, bv = _cmp_bias(pos_ref, w1_ref)
    nc = kv_ref.shape[0] // CMP_STRIDE
    c = jnp.zeros((nc, 4 * HEAD_DIM), F32)
    for r in range(0, CMP_STRIDE, 2):
        xr = jnp.concatenate([kv_ref[pl.ds(r, nc, stride=CMP_STRIDE), :],
                              kv_ref[pl.ds(r + 1, nc, stride=CMP_STRIDE), :]], axis=-1).astype(BF16)
        c = c + _dot(xr, ws_ref[r // 2])
    kc, vc = _compress_tail(c, bk, bv, w2_ref[0], w2_ref[1])
    kc_ref[...] = kc
    vc_ref[...] = vc


def _compress_prompt(kvn3, pos, w1, w2, wstack):
    b, t, width = kvn3.shape
    nc = t // CMP_STRIDE
    cst = lambda shape: pl.BlockSpec(shape, lambda bi: (0,) * len(shape))
    return pl.pallas_call(
        _compress_prompt_kernel,
        grid=(b,),
        in_specs=[pl.BlockSpec((None, t, 2 * HEAD_DIM), lambda bi: (bi, 0, 0)),
                  cst(pos.shape), cst(w1.shape), cst(w2.shape), cst(wstack.shape)],
        out_specs=[pl.BlockSpec((None, nc, HEAD_DIM), lambda bi: (bi, 0, 0)),
                   pl.BlockSpec((None, nc, HEAD_DIM), lambda bi: (bi, 0, 0))],
        out_shape=[jax.ShapeDtypeStruct((b, nc, HEAD_DIM), F32)] * 2,
        compiler_params=_cparams(("parallel",)),
        name="compress_prompt",
    )(kvn3, pos, w1, w2, wstack)


def _rank_desc(v, n):
    lane = lax.broadcasted_iota(jnp.int32, v.shape, 1)
    rank = jnp.zeros(v.shape, F32)
    for j in range(n):
        col = v[:, j:j + 1]
        rank = rank + jnp.where((col > v) | ((col == v) & (lane > j)), 1.0, 0.0)
    return rank


def _cover_matrix(n_cmp_pad, n_cmp, n_sel_pad):
    j = np.arange(n_cmp_pad)[:, None]
    i = np.arange(n_sel_pad)[None, :]
    cov = (j * CMP_STRIDE < (i + 1) * L_SEL) & (j * CMP_STRIDE + L_CMP > i * L_SEL) & (j < n_cmp)
    return jnp.asarray(cov.astype(np.float32))


def _nsa_prompt_kernel(q_ref, g_ref, kv_ref, win_ref, kc_ref, vc_ref, covert_ref, o_ref,
                       kt_ref, v_ref, kwt_ref, vw_ref, qa_ref, so_ref, sp_ref, m_ref, l_ref, acc_ref, ow_ref,
                       *, c, tk, n_cmp, ns):
    i = pl.program_id(1)
    qs = i * c
    t = kv_ref.shape[0]
    nsp = LANES - HEAD_DIM

    @pl.when(i == 0)
    def _():
        rows = lax.broadcasted_iota(jnp.int32, (nsp, t), 0)
        cols = lax.broadcasted_iota(jnp.int32, (nsp, t), 1)
        kt_ref[HEAD_DIM:LANES, :] = jnp.where(rows == cols // L_SEL, 1.0, 0.0).astype(BF16)
        rt = 256
        for n in range(t // rt):
            x = kv_ref[n * rt:(n + 1) * rt, :]
            kt_ref[0:HEAD_DIM, n * rt:(n + 1) * rt] = x.T[2 * HEAD_DIM:3 * HEAD_DIM].astype(BF16)
            v_ref[n * rt:(n + 1) * rt, :] = x[:, 3 * HEAD_DIM:4 * HEAD_DIM].astype(BF16)
            w = win_ref[n * rt:(n + 1) * rt, :]
            kwt_ref[:, n * rt:(n + 1) * rt] = w.T[0:HEAD_DIM].astype(BF16)
            vw_ref[n * rt:(n + 1) * rt, :] = w[:, HEAD_DIM:2 * HEAD_DIM].astype(BF16)

    q = q_ref[...] * ATTN_SCALE
    q4 = jnp.concatenate([q[:, h * HEAD_DIM:(h + 1) * HEAD_DIM] for h in range(N_HEADS)], axis=0).astype(BF16)
    qpos = qs + lax.broadcasted_iota(jnp.int32, (c, 1), 0)
    qpos4 = jnp.concatenate([qpos] * N_HEADS, axis=0)

    wl = WINDOW + c
    ks = pl.multiple_of(jnp.maximum(qs - WINDOW, 0), c)
    rel = qpos - (ks + lax.broadcasted_iota(jnp.int32, (1, wl), 1))
    band = jnp.where((rel >= 0) & (rel <= WINDOW), 0.0, NEG)
    sw = _dot(q4, kwt_ref[:, pl.ds(ks, wl)]) + jnp.concatenate([band] * N_HEADS, axis=0)
    mw = jnp.broadcast_to(jnp.max(_lane_partial(sw, jnp.maximum), axis=-1, keepdims=True), (N_HEADS * c, LANES))
    ew = _exp_shifted(sw, mw)
    lw = jnp.sum(_lane_partial(ew, jnp.add), axis=-1, keepdims=True)
    ow_ref[...] = _dot(ew.astype(BF16), vw_ref[pl.ds(ks, wl), :]) * (1.0 / lw)

    nc = kc_ref.shape[0]
    s = _dot_nt(q4, kc_ref[...].astype(BF16))
    jj = lax.broadcasted_iota(jnp.int32, (1, nc), 1)
    p = _masked_softmax(s, (jj * CMP_STRIDE + (L_CMP - 1) <= qpos4) & (jj < n_cmp))
    o_cmp = _dot(p.astype(BF16), vc_ref[...].astype(BF16))
    psum = p[0:c] + p[c:2 * c] + p[2 * c:3 * c] + p[3 * c:4 * c]

    imp = _dot_nt(covert_ref[...], psum, precision=HIGHEST)
    blk = lax.broadcasted_iota(jnp.int32, (nsp, 1), 0)
    qpos_l = qs + lax.broadcasted_iota(jnp.int32, (1, c), 1)
    cur = qpos_l // L_SEL
    imp = jnp.where((blk == 0) | (blk == cur) | (blk == cur - 1), jnp.inf, imp)
    imp = jnp.where(blk * L_SEL > qpos_l, -jnp.inf, imp)
    sub = lax.broadcasted_iota(jnp.int32, (8, 1), 0)
    groups = [imp[8 * g:8 * (g + 1), :] for g in range(nsp // 8)]
    ranks = [jnp.zeros((8, c), F32) for _ in groups]
    for b in range(ns):
        rb = imp[b:b + 1, :]
        for g, v in enumerate(groups):
            if 8 * g > b:
                ahead = rb >= v
            elif 8 * g + 7 <= b:
                ahead = rb > v
            else:
                ahead = (rb > v) | ((rb == v) & (sub + 8 * g > b))
            ranks[g] = ranks[g] + jnp.where(ahead, 1.0, 0.0)
    rank = jnp.concatenate(ranks, axis=0)
    sel = (rank < N_SEL) & (blk * L_SEL <= qpos_l)
    bias_t = jnp.where(sel, 0.0, NEG)
    bias = jnp.concatenate([bias_t, jnp.zeros((LANES - nsp, c), F32)], axis=0).T[:, 0:nsp]
    qa = jnp.concatenate([q4, jnp.concatenate([bias] * N_HEADS, axis=0).astype(BF16)], axis=-1)

    jt = qs // tk
    k_own = pl.multiple_of(jt * tk, tk)
    qa_ref[...] = qa
    kpos = k_own + lax.broadcasted_iota(jnp.int32, (1, tk), 1)
    causal = jnp.where(kpos <= qpos, 0.0, NEG)
    sc = _dot(qa, kt_ref[:, pl.ds(k_own, tk)]) + jnp.concatenate([causal] * N_HEADS, axis=0)
    so_ref[...] = sc
    m_ref[...] = _lane_partial(sc, jnp.maximum)

    def max_tiles(j0, n):
        k0 = pl.multiple_of(j0 * tk, tk)
        sc = _dot(qa_ref[...], kt_ref[:, pl.ds(k0, n * tk)])
        for u in range(n):
            sp_ref[j0 + u] = sc[:, u * tk:(u + 1) * tk]
        m_ref[...] = jnp.maximum(m_ref[...], _lane_partial(sc, jnp.maximum))

    _sweep(jt, max_tiles, 4)
    m = jnp.broadcast_to(jnp.max(m_ref[...], axis=-1, keepdims=True), m_ref.shape)
    m_ref[...] = m
    pe = _exp_shifted(so_ref[...], m)
    l_ref[...] = _lane_partial(pe, jnp.add)
    acc_ref[...] = _dot(pe.astype(BF16), v_ref[pl.ds(k_own, tk), :])

    def sum_tiles(j0, n):
        k0 = pl.multiple_of(j0 * tk, tk)
        pe = jnp.concatenate([_exp_shifted(sp_ref[j0 + u], m_ref[...]) for u in range(n)], axis=-1)
        l_ref[...] += _lane_partial(pe, jnp.add)
        acc_ref[...] += _dot(pe.astype(BF16), v_ref[pl.ds(k0, n * tk), :])

    _sweep(jt, sum_tiles)
    o_sel = acc_ref[...] * (1.0 / jnp.sum(l_ref[...], axis=-1, keepdims=True))

    o_win = ow_ref[...]
    g = _sigmoid(g_ref[...])

    def gate(br):
        return jnp.concatenate([g[:, br * N_HEADS + h:br * N_HEADS + h + 1] for h in range(N_HEADS)], axis=0)

    o = gate(0) * o_cmp + gate(1) * o_sel + gate(2) * o_win
    for h in range(N_HEADS):
        o_ref[:, h * HEAD_DIM:(h + 1) * HEAD_DIM] = o[h * c:(h + 1) * c].astype(o_ref.dtype)


def _nsa_prompt(z3, kvn3, win3, kc, vc):
    b, t, _ = z3.shape
    c = 128
    tk = 512
    nc = kc.shape[1]
    n_cmp = t // CMP_STRIDE - 1
    ns = t // L_SEL
    nsp = LANES - HEAD_DIM
    assert ns <= nsp and c == LANES
    covert = _cover_matrix(nc, n_cmp, nsp).T
    kern = functools.partial(_nsa_prompt_kernel, c=c, tk=tk, n_cmp=n_cmp, ns=ns)
    return pl.pallas_call(
        kern,
        grid=(b, t // c),
        in_specs=[
            pl.BlockSpec((None, c, GROUP_W), lambda bi, i: (bi, i, OFF_QNSA // GROUP_W)),
            pl.BlockSpec((None, c, LANES), lambda bi, i: (bi, i, OFF_GATE // LANES)),
            pl.BlockSpec((None, t, GROUP_W), lambda bi, i: (bi, 0, 0)),
            pl.BlockSpec((None, t, LANES), lambda bi, i: (bi, 0, 0)),
            pl.BlockSpec((None, nc, HEAD_DIM), lambda bi, i: (bi, 0, 0)),
            pl.BlockSpec((None, nc, HEAD_DIM), lambda bi, i: (bi, 0, 0)),
            pl.BlockSpec((nsp, nc), lambda bi, i: (0, 0)),
        ],
        out_specs=pl.BlockSpec((None, c, GROUP_W), lambda bi, i: (bi, i, 0)),
        out_shape=jax.ShapeDtypeStruct((b, t, GROUP_W), BF16),
        scratch_shapes=[pltpu.VMEM((LANES, t), BF16), pltpu.VMEM((t, HEAD_DIM), BF16),
                        pltpu.VMEM((HEAD_DIM, t), BF16), pltpu.VMEM((t, HEAD_DIM), BF16),
                        pltpu.VMEM((N_HEADS * c, LANES), BF16), pltpu.VMEM((N_HEADS * c, tk), F32),
                        pltpu.VMEM((t // tk - 1, N_HEADS * c, tk), F32),
                        pltpu.VMEM((N_HEADS * c, LANES), F32), pltpu.VMEM((N_HEADS * c, LANES), F32),
                        pltpu.VMEM((N_HEADS * c, HEAD_DIM), F32), pltpu.VMEM((N_HEADS * c, HEAD_DIM), F32)],
        compiler_params=_cparams(("parallel", "arbitrary")),
        name="nsa_prompt",
    )(z3, z3, kvn3, win3, kc, vc, covert)


def _moba_prompt_kernel(q_ref, kv_ref, o_ref, kmean_ref, kt_ref, v_ref, qa_ref, so_ref, sp_ref, m_ref, l_ref, acc_ref,
                        *, nb):
    i = pl.program_id(1)
    c = MOBA_BLOCK
    t = kv_ref.shape[0]

    @pl.when(i == 0)
    def _():
        rows = lax.broadcasted_iota(jnp.int32, (LANES - HEAD_DIM, t), 0)
        cols = lax.broadcasted_iota(jnp.int32, (LANES - HEAD_DIM, t), 1)
        onehot = jnp.where(rows == cols // c, 1.0, 0.0).astype(BF16)
        for h in range(N_HEADS):
            kt_ref[h, HEAD_DIM:LANES, :] = onehot
        kmean_ref[...] = jnp.zeros(kmean_ref.shape, F32)
        for n in range(nb):
            x = kv_ref[n * c:(n + 1) * c, :]
            kmean_ref[n:n + 1, :] = jnp.mean(x[:, 0:GROUP_W], axis=0, keepdims=True)
            xt = x[:, 0:GROUP_W].T
            for h in range(N_HEADS):
                lo, hi = h * HEAD_DIM, (h + 1) * HEAD_DIM
                kt_ref[h, 0:HEAD_DIM, n * c:(n + 1) * c] = xt[lo:hi].astype(BF16)
                v_ref[h, n * c:(n + 1) * c, :] = x[:, GROUP_W + lo:GROUP_W + hi].astype(BF16)

    q = q_ref[...]
    nbp = LANES - HEAD_DIM
    nbr = -(-nb // 8) * 8
    blk = lax.broadcasted_iota(jnp.int32, (nbr, 1), 0)
    fblk = blk.astype(F32)
    past = blk < i
    row = lax.broadcasted_iota(jnp.int32, (c, c), 0)
    col = lax.broadcasted_iota(jnp.int32, (c, c), 1)
    tri = col <= row
    own0 = pl.multiple_of(i * c, c)
    for h in range(N_HEADS):
        lo, hi = h * HEAD_DIM, (h + 1) * HEAD_DIM
        qh = q[:, lo:hi]
        work = jnp.where(past, _dot_nt(kmean_ref[0:nbr, lo:hi], qh, precision=HIGHEST), -jnp.inf)
        sel = jnp.zeros((nbr, c), jnp.bool_)
        for _ in range(MOBA_TOPK):
            mx = jnp.max(work, axis=0, keepdims=True)
            pick = jnp.min(jnp.where(work == mx, fblk, float(nbp)), axis=0, keepdims=True)
            hit = fblk == pick
            sel = sel | (hit & (mx > -jnp.inf))
            work = jnp.where(hit, -jnp.inf, work)
        bias_t = jnp.where(sel, 0.0, NEG)
        bias = jnp.concatenate([bias_t, jnp.full((LANES - nbr, c), NEG, F32)], axis=0).T[:, 0:nbp]
        qs = (qh * ATTN_SCALE).astype(BF16)
        qa_ref[h] = jnp.concatenate([qs, bias.astype(BF16)], axis=-1)
        sc = jnp.where(tri, _dot(qs, kt_ref[h, 0:HEAD_DIM, pl.ds(own0, c)]), NEG)
        so_ref[h] = sc
        m_ref[h] = _lane_partial(sc, jnp.maximum)

    def max_tiles(j0, n):
        k0 = pl.multiple_of(j0 * c, c)
        for h in range(N_HEADS):
            sc = _dot(qa_ref[h], kt_ref[h, :, pl.ds(k0, n * c)])
            for u in range(n):
                sp_ref[h, j0 + u] = sc[:, u * c:(u + 1) * c]
            m_ref[h] = jnp.maximum(m_ref[h], _lane_partial(sc, jnp.maximum))

    _sweep(i, max_tiles, 4)
    for h in range(N_HEADS):
        m = jnp.broadcast_to(jnp.max(m_ref[h], axis=-1, keepdims=True), (c, LANES))
        m_ref[h] = m
        pe = _exp_shifted(so_ref[h], m)
        l_ref[h] = _lane_partial(pe, jnp.add)
        acc_ref[h] = _dot(pe.astype(BF16), v_ref[h, pl.ds(own0, c), :])

    def sum_tiles(j0, n):
        k0 = pl.multiple_of(j0 * c, c)
        for h in range(N_HEADS):
            pe = jnp.concatenate([_exp_shifted(sp_ref[h, j0 + u], m_ref[h]) for u in range(n)], axis=-1)
            l_ref[h] += _lane_partial(pe, jnp.add)
            acc_ref[h] += _dot(pe.astype(BF16), v_ref[h, pl.ds(k0, n * c), :])

    _sweep(i, sum_tiles)
    for h in range(N_HEADS):
        l = jnp.sum(l_ref[h], axis=-1, keepdims=True)
        o_ref[:, h * HEAD_DIM:(h + 1) * HEAD_DIM] = (acc_ref[h] * (1.0 / l)).astype(o_ref.dtype)


def _moba_prompt(z3, kvm3):
    b, t, _ = z3.shape
    nb = t // MOBA_BLOCK
    assert nb <= LANES - HEAD_DIM
    kern = functools.partial(_moba_prompt_kernel, nb=nb)
    return pl.pallas_call(
        kern,
        grid=(b, nb),
        in_specs=[pl.BlockSpec((None, MOBA_BLOCK, GROUP_W), lambda bi, i: (bi, i, OFF_QMOBA // GROUP_W)),
                  pl.BlockSpec((None, t, 2 * GROUP_W), lambda bi, i: (bi, 0, 0))],
        out_specs=pl.BlockSpec((None, MOBA_BLOCK, GROUP_W), lambda bi, i: (bi, i, 0)),
        out_shape=jax.ShapeDtypeStruct((b, t, GROUP_W), BF16),
        scratch_shapes=[pltpu.VMEM((LANES - HEAD_DIM, GROUP_W), F32),
                        pltpu.VMEM((N_HEADS, LANES, t), BF16),
                        pltpu.VMEM((N_HEADS, t, HEAD_DIM), BF16),
                        pltpu.VMEM((N_HEADS, MOBA_BLOCK, LANES), BF16),
                        pltpu.VMEM((N_HEADS, MOBA_BLOCK, MOBA_BLOCK), F32),
                        pltpu.VMEM((N_HEADS, nb - 1, MOBA_BLOCK, MOBA_BLOCK), F32),
                        pltpu.VMEM((N_HEADS, MOBA_BLOCK, LANES), F32),
                        pltpu.VMEM((N_HEADS, MOBA_BLOCK, LANES), F32),
                        pltpu.VMEM((N_HEADS, MOBA_BLOCK, HEAD_DIM), F32)],
        compiler_params=_cparams(("parallel", "arbitrary")),
        name="moba_prompt",
    )(z3, kvm3)


def _conv_pool_sample_kernel(z_ref, cst_ref, pst_ref, dw_ref, dwb_ref, lg_ref, lb_ref, pw_ref, plw_ref, pls_ref,
                             yc_ref, yp_ref, cnew_ref, pnew_ref):
    u = z_ref[:, OFF_UCONV:OFF_UCONV + 2 * GROUP_W]
    zg = u[:, :GROUP_W] * _sigmoid(u[:, GROUP_W:])
    nst = CONV_W - 1
    y = dwb_ref[...] + dw_ref[nst:nst + 1, :] * zg
    for k in range(nst):
        y = y + dw_ref[k:k + 1, :] * cst_ref[k]
    y = _silu(_ln(y, lg_ref[...], lb_ref[...]))
    yc_ref[...] = _dot(y.astype(BF16), pw_ref[...])
    for k in range(nst - 1):
        cnew_ref[k] = cst_ref[k + 1]
    cnew_ref[nst - 1] = zg

    p = z_ref[:, OFF_UPOOL:OFF_UPOOL + GROUP_W]
    npst = POOL_MAX - 1
    lane = lax.broadcasted_iota(jnp.int32, (1, GROUP_W), 1)
    run = p
    mean = jnp.zeros_like(p)
    k = 1
    for g, w in enumerate(POOL_WINDOWS):
        while k < w:
            run = run + pst_ref[npst - k]
            k += 1
        mean = jnp.where((lane >= g * POOL_GW) & (lane < (g + 1) * POOL_GW), run / float(w), mean)
    d = mean - p
    yp_ref[...] = _dot(d.astype(BF16), plw_ref[...]) * pls_ref[...]
    for k in range(npst - 1):
        pnew_ref[k] = pst_ref[k + 1]
    pnew_ref[npst - 1] = p


def _conv_pool_sample(z, cst, pst, dw, dwb, lg, lb, pw, plw, pls):
    s = z.shape[0]
    return pl.pallas_call(
        _conv_pool_sample_kernel,
        out_shape=[jax.ShapeDtypeStruct((s, GROUP_W), F32), jax.ShapeDtypeStruct((s, GROUP_W), F32),
                   jax.ShapeDtypeStruct(cst.shape, F32), jax.ShapeDtypeStruct(pst.shape, F32)],
        compiler_params=pltpu.CompilerParams(vmem_limit_bytes=VMEM_LIMIT),
        name="conv_pool_sample",
    )(z, cst, pst, dw, dwb, lg, lb, pw, plw, pls)


_STREAM_SLOTS = 3
_CMP_PAGES = 16


def _compress_pages_kernel(pt_ref, cache_ref, w_ref, perm_ref, c_ref, buf_ref, rows_ref, sem_ref,
                           *, layer, n_steps, page):
    b = pl.program_id(0)
    s = pl.program_id(1)
    g = _CMP_PAGES
    step = b * n_steps + s
    total = pl.num_programs(0) * n_steps

    def copies(stp, slot):
        bb = stp // n_steps
        ss = stp % n_steps
        return [pltpu.make_async_copy(cache_ref.at[layer, pt_ref[bb, ss * g + k], pl.ds(0, 2)],
                                      buf_ref.at[slot, k], sem_ref.at[slot]) for k in range(g)]

    ahead = _STREAM_SLOTS - 1

    @pl.when(step == 0)
    def _():
        for d in range(ahead):
            @pl.when(d < total)
            def _():
                for cp in copies(d, d):
                    cp.start()

    slot = step % _STREAM_SLOTS

    @pl.when(step + ahead < total)
    def _():
        for cp in copies(step + ahead, (step + ahead) % _STREAM_SLOTS):
            cp.start()

    pltpu.make_async_copy(cache_ref.at[layer, pl.ds(0, g), pl.ds(0, 2)], buf_ref.at[slot], sem_ref.at[slot]).wait()

    nchunk = g * page // CMP_STRIDE
    for k in range(g):
        x = buf_ref[slot, k].reshape(2 * HEAD_DIM, page).astype(BF16)
        rows_ref[k] = _dot(x, perm_ref[...]).T.reshape(CMP_STRIDE, page // CMP_STRIDE, 2 * HEAD_DIM)
    acc = jnp.zeros((nchunk, 4 * HEAD_DIM), F32)
    for r in range(0, CMP_STRIDE, 2):
        xr = jnp.concatenate([rows_ref[:, r].reshape(nchunk, 2 * HEAD_DIM),
                              rows_ref[:, r + 1].reshape(nchunk, 2 * HEAD_DIM)], axis=-1).astype(BF16)
        acc = acc + _dot(xr, w_ref[r // 2])
    c_ref[...] = acc


def _row_perm(page):
    j = np.arange(page)
    src = CMP_STRIDE * (j % (page // CMP_STRIDE)) + j // (page // CMP_STRIDE)
    return jnp.asarray((np.arange(page)[:, None] == src[None, :]).astype(np.float32)).astype(BF16)


def _compress_pages(page_table, cache_t, wstack, layer):
    s, npg = page_table.shape
    page = cache_t.shape[-1]
    g = _CMP_PAGES
    n_steps = npg // g
    nchunk = g * page // CMP_STRIDE
    kern = functools.partial(_compress_pages_kernel, layer=layer, n_steps=n_steps, page=page)
    return pl.pallas_call(
        kern,
        grid_spec=pltpu.PrefetchScalarGridSpec(
            num_scalar_prefetch=1,
            grid=(s, n_steps),
            in_specs=[pl.BlockSpec(memory_space=pl.ANY),
                      pl.BlockSpec(wstack.shape, lambda b, i, pt: (0, 0, 0)),
                      pl.BlockSpec((page, page), lambda b, i, pt: (0, 0))],
            out_specs=pl.BlockSpec((None, nchunk, 4 * HEAD_DIM), lambda b, i, pt: (b, i, 0)),
            scratch_shapes=[pltpu.VMEM((_STREAM_SLOTS, g, 2, HEAD_DIM, page), F32),
                            pltpu.VMEM((g, CMP_STRIDE, page // CMP_STRIDE, 2 * HEAD_DIM), F32),
                            pltpu.SemaphoreType.DMA((_STREAM_SLOTS,))]),
        out_shape=jax.ShapeDtypeStruct((s, n_steps * nchunk, 4 * HEAD_DIM), F32),
        compiler_params=_cparams(("arbitrary", "arbitrary")),
        name="compress_pages",
    )(page_table, cache_t, wstack, _row_perm(page))


def _stack_heads(q):
    rows = [q[:, h * HEAD_DIM:(h + 1) * HEAD_DIM] for h in range(N_HEADS)]
    return jnp.concatenate(rows + [jnp.zeros((8 - N_HEADS, HEAD_DIM), q.dtype)], axis=0)


def _nsa_select_kernel(c_ref, z_ref, pos_ref, w1_ref, w2_ref, cover_ref, ocmp_ref, idx_ref, psum_ref,
                       *, n_cmp, n_sel, qpos):
    bk, bv = _cmp_bias(pos_ref, w1_ref)
    kc, vc = _compress_tail(c_ref[...], bk, bv, w2_ref[0], w2_ref[1])
    q = z_ref[:, OFF_QNSA:OFF_QNSA + GROUP_W]
    q4 = _stack_heads(q)
    nc = kc.shape[0]
    s = _dot_nt(q4.astype(BF16), kc.astype(BF16)) * ATTN_SCALE
    jj = lax.broadcasted_iota(jnp.int32, (1, nc), 1)
    p = _masked_softmax(s, (jj * CMP_STRIDE + (L_CMP - 1) <= qpos) & (jj < n_cmp))
    ocmp_ref[...] = _dot(p.astype(BF16), vc.astype(BF16))[0:N_HEADS]
    b = pl.program_id(0)
    psum_ref[pl.ds(b, 1), :] = jnp.sum(p[0:N_HEADS], axis=0, keepdims=True)

    @pl.when(b == pl.num_programs(0) - 1)
    def _():
        imp = _dot(psum_ref[...], cover_ref[...], precision=HIGHEST)
        ns, nsp = imp.shape
        blk = lax.broadcasted_iota(jnp.int32, (1, nsp), 1)
        fblk = blk.astype(F32)
        cur = qpos // L_SEL
        imp = jnp.where((blk == 0) | (blk == cur) | (blk == cur - 1), jnp.inf, imp)
        imp = jnp.where(blk * L_SEL > qpos, -jnp.inf, imp)
        work = jnp.where(blk < n_sel, jnp.maximum(imp, -3e38), -jnp.inf)
        out_lane = lax.broadcasted_iota(jnp.int32, (1, LANES), 1)
        idx = jnp.zeros((ns, LANES), jnp.int32)
        for t in range(N_SEL):
            mx = jnp.max(work, axis=-1, keepdims=True)
            pick = jnp.min(jnp.where(work == mx, fblk, float(nsp)), axis=-1, keepdims=True)
            idx = jnp.where(out_lane == t, pick.astype(jnp.int32), idx)
            work = jnp.where(fblk == pick, -jnp.inf, work)
        idx_ref[...] = idx


def _nsa_select(c, z3, pos, w1, w2, qpos):
    s, nc, _ = c.shape
    n_cmp = nc - 1
    n_sel = qpos // L_SEL + 1
    nsp = -(-n_sel // LANES) * LANES
    cover = _cover_matrix(nc, n_cmp, nsp)
    kern = functools.partial(_nsa_select_kernel, n_cmp=n_cmp, n_sel=n_sel, qpos=qpos)
    cst = lambda shape: pl.BlockSpec(shape, lambda b: (0,) * len(shape))
    return pl.pallas_call(
        kern,
        grid=(s,),
        in_specs=[pl.BlockSpec((None, nc, 4 * HEAD_DIM), lambda b: (b, 0, 0)),
                  pl.BlockSpec((None, 1, N_IN_PAD), lambda b: (b, 0, 0)),
                  cst(pos.shape), cst(w1.shape), cst(w2.shape), cst(cover.shape)],
        out_specs=[pl.BlockSpec((None, N_HEADS, HEAD_DIM), lambda b: (b, 0, 0)),
                   pl.BlockSpec((s, LANES), lambda b: (0, 0))],
        out_shape=[jax.ShapeDtypeStruct((s, N_HEADS, HEAD_DIM), F32), jax.ShapeDtypeStruct((s, LANES), jnp.int32)],
        scratch_shapes=[pltpu.VMEM((s, nc), F32)],
        compiler_params=_cparams(("arbitrary",)),
        name="nsa_select",
    )(c, z3, pos, w1, w2, cover)


def _col_from_row(row):
    n = row.shape[1]
    eye = lax.broadcasted_iota(jnp.int32, (n, n), 0) == lax.broadcasted_iota(jnp.int32, (n, n), 1)
    return jnp.sum(jnp.where(eye, row, 0.0), axis=-1, keepdims=True)


def _nsa_sample_kernel(pt_ref, idx_ref, cache_ref, z_ref, ocmp_ref, win_ref, o_ref, wnew_ref, buf_ref, sem_ref,
                       *, layer, page, qpos):
    b = pl.program_id(0)
    npg = pt_ref.shape[1]
    per_page = page // L_SEL

    slot = b % 2

    def copy(bb, t, sl):
        pg = jnp.minimum(idx_ref[bb, t] // per_page, npg - 1)
        return pltpu.make_async_copy(cache_ref.at[layer, pt_ref[bb, pg], pl.ds(2, 2)], buf_ref.at[sl, t],
                                     sem_ref.at[sl])

    @pl.when(b == 0)
    def _():
        for t in range(N_SEL):
            copy(b, t, 0).start()

    @pl.when(b + 1 < pl.num_programs(0))
    def _():
        for t in range(N_SEL):
            copy(b + 1, t, 1 - slot).start()

    z = z_ref[...]
    q = z[:, OFF_QNSA:OFF_QNSA + GROUP_W]
    q4b = _stack_heads(q).astype(BF16)
    kvn = z[:, OFF_KVNSA:OFF_KVNSA + GROUP_W]
    ks_new, vs_new = kvn[:, 2 * HEAD_DIM:3 * HEAD_DIM], kvn[:, 3 * HEAD_DIM:4 * HEAD_DIM]
    kvw = z[:, OFF_KVWIN:OFF_KVWIN + 2 * HEAD_DIM]
    kw_new, vw_new = kvw[:, 0:HEAD_DIM], kvw[:, HEAD_DIM:2 * HEAD_DIM]
    q4r = q4b.astype(F32)

    def new_score(k_new):
        kb = k_new.astype(BF16).astype(F32)
        return jnp.sum(q4r * kb, axis=-1, keepdims=True) * ATTN_SCALE

    wk = win_ref[0]
    wv = win_ref[1]
    sw = _dot(q4b, wk.astype(BF16)) * ATTN_SCALE
    sw_new = new_score(kw_new)
    mw = jnp.maximum(jnp.max(sw, axis=-1, keepdims=True), sw_new)
    ew = jnp.exp(sw - mw)
    ew_new = jnp.exp(sw_new - mw)
    dw = jnp.sum(ew, axis=-1, keepdims=True) + ew_new
    pw = ew * (1.0 / dw)
    pw_new = ew_new * (1.0 / dw)
    o_win = _dot_nt(pw.astype(BF16), wv.astype(BF16)) \
        + pw_new.astype(BF16).astype(F32) * vw_new.astype(BF16).astype(F32)
    wl = wk.shape[1]
    lane = lax.broadcasted_iota(jnp.int32, (1, wl), 1)
    wnew_ref[0] = jnp.where(lane == wl - 1, _col_from_row(kw_new), pltpu.roll(wk, wl - 1, 1))
    wnew_ref[1] = jnp.where(lane == wl - 1, _col_from_row(vw_new), pltpu.roll(wv, wl - 1, 1))

    pltpu.make_async_copy(cache_ref.at[layer, pl.ds(0, N_SEL), pl.ds(2, 2)], buf_ref.at[slot],
                          sem_ref.at[slot]).wait()

    n_past = (qpos // L_SEL)
    plane = lax.broadcasted_iota(jnp.int32, (1, page), 1)
    scores = []
    new_taken = jnp.int32(0)
    for t in range(N_SEL):
        bid = idx_ref[b, t]
        in_past = bid < n_past
        half = bid % per_page
        valid = (plane >= half * L_SEL) & (plane < (half + 1) * L_SEL) & in_past
        sc = _dot(q4b, buf_ref[slot, t, 0].astype(BF16)) * ATTN_SCALE
        scores.append(jnp.where(valid, sc, -jnp.inf))
        new_taken = new_taken + (bid == n_past).astype(jnp.int32)
    has_new = new_taken > 0
    ss_new = jnp.where(has_new, new_score(ks_new), -jnp.inf)
    ms = ss_new
    for sc in scores:
        ms = jnp.maximum(ms, jnp.max(sc, axis=-1, keepdims=True))
    ms = jnp.where(ms == -jnp.inf, 0.0, ms)
    es_new = jnp.exp(ss_new - ms)
    ds = es_new
    es = []
    for sc in scores:
        e = jnp.exp(sc - ms)
        es.append(e)
        ds = ds + jnp.sum(e, axis=-1, keepdims=True)
    inv = 1.0 / jnp.where(ds > 0, ds, 1.0)
    o_sel = (es_new * inv).astype(BF16).astype(F32) * vs_new.astype(BF16).astype(F32)
    for t in range(N_SEL):
        o_sel = o_sel + _dot_nt((es[t] * inv).astype(BF16), buf_ref[slot, t, 1].astype(BF16))

    g = _sigmoid(z[:, OFF_GATE:OFF_GATE + LANES])
    o_cmp = ocmp_ref[...]
    for h in range(N_HEADS):
        oh = (g[:, h:h + 1] * o_cmp[h:h + 1] + g[:, N_HEADS + h:N_HEADS + h + 1] * o_sel[h:h + 1]
              + g[:, 2 * N_HEADS + h:2 * N_HEADS + h + 1] * o_win[h:h + 1])
        o_ref[:, h * HEAD_DIM:(h + 1) * HEAD_DIM] = oh


def _nsa_sample(page_table, idx, cache_t, z3, ocmp, win_t, layer, qpos):
    s = z3.shape[0]
    page = cache_t.shape[-1]
    wl = win_t.shape[-1]
    kern = functools.partial(_nsa_sample_kernel, layer=layer, page=page, qpos=qpos)
    return pl.pallas_call(
        kern,
        grid_spec=pltpu.PrefetchScalarGridSpec(
            num_scalar_prefetch=2,
            grid=(s,),
            in_specs=[pl.BlockSpec(memory_space=pl.ANY),
                      pl.BlockSpec((None, 1, N_IN_PAD), lambda b, pt, ix: (b, 0, 0)),
                      pl.BlockSpec((None, N_HEADS, HEAD_DIM), lambda b, pt, ix: (b, 0, 0)),
                      pl.BlockSpec((None, None, 2, HEAD_DIM, wl), lambda b, pt, ix: (layer, b, 0, 0, 0))],
            out_specs=[pl.BlockSpec((None, 1, GROUP_W), lambda b, pt, ix: (b, 0, 0)),
                       pl.BlockSpec((None, 2, HEAD_DIM, wl), lambda b, pt, ix: (b, 0, 0, 0))],
            scratch_shapes=[pltpu.VMEM((2, N_SEL, 2, HEAD_DIM, page), F32), pltpu.SemaphoreType.DMA((2,))]),
        out_shape=[jax.ShapeDtypeStruct((s, 1, GROUP_W), F32), jax.ShapeDtypeStruct((s, 2, HEAD_DIM, wl), F32)],
        compiler_params=_cparams(("arbitrary",)),
        name="nsa_sample",
    )(page_table, idx, cache_t, z3, ocmp, win_t)


_MOBA_PAGES = 16


def _moba_gate_kernel(pt_ref, cache_ref, qcol_ref, idx_ref, buf_ref, part_ref, sem_ref,
                      *, layer, n_steps, page, nb, cur):
    b = pl.program_id(0)
    s = pl.program_id(1)
    g = _MOBA_PAGES
    step = b * n_steps + s
    total = pl.num_programs(0) * n_steps
    ppb = MOBA_BLOCK // page
    bps = g // ppb

    def copies(stp, slot):
        bb = stp // n_steps
        ss = stp % n_steps
        return [pltpu.make_async_copy(cache_ref.at[layer, pt_ref[bb, ss * g + k], 0],
                                      buf_ref.at[slot, k], sem_ref.at[slot]) for k in range(g)]

    ahead = _STREAM_SLOTS - 1

    @pl.when(step == 0)
    def _():
        for d in range(ahead):
            @pl.when(d < total)
            def _():
                for cp in copies(d, d):
                    cp.start()

    slot = step % _STREAM_SLOTS

    @pl.when(step + ahead < total)
    def _():
        for cp in copies(step + ahead, (step + ahead) % _STREAM_SLOTS):
            cp.start()

    pltpu.make_async_copy(cache_ref.at[layer, pl.ds(0, g), 0], buf_ref.at[slot], sem_ref.at[slot]).wait()

    for j in range(bps):
        for h in range(N_HEADS):
            qh = qcol_ref[h * HEAD_DIM:(h + 1) * HEAD_DIM, :]
            acc = buf_ref[slot, j * ppb, h] * qh
            for k in range(1, ppb):
                acc = acc + buf_ref[slot, j * ppb + k, h] * qh
            part_ref[h * nb + s * bps + j] = jnp.sum(acc.reshape(HEAD_DIM // 8, 8, page), axis=0)

    @pl.when(s == n_steps - 1)
    def _():
        ones = jnp.ones((8, page), F32)
        part = jnp.sum(part_ref[...], axis=1)
        gate = _dot_nt(ones, part, precision=HIGHEST)[0:1] * (1.0 / MOBA_BLOCK)
        lane = lax.broadcasted_iota(jnp.int32, gate.shape, 1)
        out_lane = lax.broadcasted_iota(jnp.int32, (1, LANES), 1)
        idx = jnp.zeros((1, LANES), jnp.int32)
        for h in range(N_HEADS):
            n_of = lane - h * nb
            f_of = n_of.astype(F32)
            inh = (n_of >= 0) & (n_of < nb) & (n_of < cur)
            work = jnp.where(inh, jnp.maximum(gate, -3e38), -jnp.inf)
            for t in range(MOBA_TOPK):
                mx = jnp.max(work, axis=-1, keepdims=True)
                pick = jnp.min(jnp.where((work == mx) & inh, f_of, float(4 * nb)), axis=-1, keepdims=True)
                idx = jnp.where(out_lane == h * MOBA_TOPK + t, pick.astype(jnp.int32), idx)
                work = jnp.where(f_of == pick, -jnp.inf, work)
        idx_ref[...] = idx


def _moba_gate(page_table, cache_t, qcol, layer, cur):
    s, npg = page_table.shape
    page = cache_t.shape[-1]
    g = _MOBA_PAGES
    n_steps = npg // g
    nb = npg * page // MOBA_BLOCK
    kern = functools.partial(_moba_gate_kernel, layer=layer, n_steps=n_steps, page=page, nb=nb, cur=cur)
    return pl.pallas_call(
        kern,
        grid_spec=pltpu.PrefetchScalarGridSpec(
            num_scalar_prefetch=1,
            grid=(s, n_steps),
            in_specs=[pl.BlockSpec(memory_space=pl.ANY),
                      pl.BlockSpec((None, N_HEADS * HEAD_DIM, page), lambda b, i, pt: (b, 0, 0))],
            out_specs=pl.BlockSpec((None, 1, LANES), lambda b, i, pt: (b, 0, 0)),
            scratch_shapes=[pltpu.VMEM((_STREAM_SLOTS, g, N_HEADS, HEAD_DIM, page), F32),
                            pltpu.VMEM((N_HEADS * nb, 8, page), F32),
                            pltpu.SemaphoreType.DMA((_STREAM_SLOTS,))]),
        out_shape=jax.ShapeDtypeStruct((s, 1, LANES), jnp.int32),
        compiler_params=_cparams(("arbitrary", "arbitrary")),
        name="moba_gate",
    )(page_table, cache_t, qcol)


def _moba_sample_kernel(pt_ref, idx_ref, cache_ref, z_ref, o_ref, buf_ref, sem_ref, *, layer, page, cur):
    b = pl.program_id(0)
    npg = pt_ref.shape[1]
    ppb = MOBA_BLOCK // page
    n_src = MOBA_TOPK * ppb

    slot = b % 2

    def copy(bb, sl, h, t, k, kv):
        blk = jnp.minimum(idx_ref[bb, h * MOBA_TOPK + t], npg // ppb - 1)
        return pltpu.make_async_copy(cache_ref.at[layer, pt_ref[bb, blk * ppb + k], kv, h],
                                     buf_ref.at[sl, t * ppb + k, kv, h], sem_ref.at[sl])

    every = [(h, t, k, kv) for h in range(N_HEADS) for t in range(MOBA_TOPK) for k in range(ppb) for kv in range(2)]

    @pl.when(b == 0)
    def _():
        for a in every:
            copy(b, 0, *a).start()

    @pl.when(b + 1 < pl.num_programs(0))
    def _():
        for a in every:
            copy(b + 1, 1 - slot, *a).start()

    z = z_ref[...]
    q = z[:, OFF_QMOBA:OFF_QMOBA + GROUP_W]
    k_new = z[:, OFF_KVMOBA:OFF_KVMOBA + GROUP_W]
    v_new = z[:, OFF_KVMOBA + GROUP_W:OFF_KVMOBA + 2 * GROUP_W]
    pltpu.make_async_copy(cache_ref.at[layer, pl.ds(0, n_src)], buf_ref.at[slot], sem_ref.at[slot]).wait()

    for h in range(N_HEADS):
        lo, hi = h * HEAD_DIM, (h + 1) * HEAD_DIM
        qh = jnp.concatenate([q[:, lo:hi]] * 8, axis=0).astype(BF16)
        qr = q[:, lo:hi].astype(BF16).astype(F32)
        s_new = jnp.sum(qr * k_new[:, lo:hi].astype(BF16).astype(F32), axis=-1, keepdims=True) * ATTN_SCALE
        scores = []
        m = s_new
        for t in range(MOBA_TOPK):
            ok = idx_ref[b, h * MOBA_TOPK + t] < cur
            for k in range(ppb):
                sc = _dot(qh, buf_ref[slot, t * ppb + k, 0, h].astype(BF16))[0:1] * ATTN_SCALE
                sc = jnp.where(ok, sc, -jnp.inf)
                scores.append(sc)
                m = jnp.maximum(m, jnp.max(sc, axis=-1, keepdims=True))
        e_new = jnp.exp(s_new - m)
        d = e_new
        es = []
        for sc in scores:
            e = jnp.exp(sc - m)
            es.append(e)
            d = d + jnp.sum(e, axis=-1, keepdims=True)
        inv = 1.0 / d
        o = (e_new * inv).astype(BF16).astype(F32) * v_new[:, lo:hi].astype(BF16).astype(F32)
        for n in range(n_src):
            pn = jnp.concatenate([es[n] * inv] * 8, axis=0).astype(BF16)
            o = o + _dot_nt(pn, buf_ref[slot, n, 1, h].astype(BF16))[0:1]
        o_ref[:, lo:hi] = o


def _moba_sample(page_table, idx, cache_t, z3, layer, cur):
    s = z3.shape[0]
    page = cache_t.shape[-1]
    ppb = MOBA_BLOCK // page
    kern = functools.partial(_moba_sample_kernel, layer=layer, page=page, cur=cur)
    return pl.pallas_call(
        kern,
        grid_spec=pltpu.PrefetchScalarGridSpec(
            num_scalar_prefetch=2,
            grid=(s,),
            in_specs=[pl.BlockSpec(memory_space=pl.ANY),
                      pl.BlockSpec((None, 1, N_IN_PAD), lambda b, pt, ix: (b, 0, 0))],
            out_specs=pl.BlockSpec((None, 1, GROUP_W), lambda b, pt, ix: (b, 0, 0)),
            scratch_shapes=[pltpu.VMEM((2, MOBA_TOPK * ppb, 2, N_HEADS, HEAD_DIM, page), F32),
                            pltpu.SemaphoreType.DMA((2,))]),
        out_shape=jax.ShapeDtypeStruct((s, 1, GROUP_W), F32),
        compiler_params=_cparams(("arbitrary",)),
        name="moba_sample",
    )(page_table, idx, cache_t, z3)


def _cmp_weight_stack(w1):
    w = w1.reshape(2, 2, CMP_STRIDE, HEAD_DIM, HEAD_DIM)
    zero = jnp.zeros((CMP_STRIDE, HEAD_DIM, 2 * HEAD_DIM), F32)
    top = jnp.concatenate([w[0, 0], w[0, 1], zero], axis=-1)
    bot = jnp.concatenate([zero, w[1, 0], w[1, 1]], axis=-1)
    per_r = jnp.concatenate([top, bot], axis=1)
    return per_r.reshape(CMP_STRIDE // 2, 4 * HEAD_DIM, 4 * HEAD_DIM).astype(BF16)


def kernel(x_prompt, x_sample, cache_moba_kv, cache_nsa_kv, state_nsa_win, state_conv, state_pool, page_table,
           w_in, conv_dw, conv_dw_b, conv_ln_g, conv_ln_b, conv_pw, nsa_cmp_pos, nsa_cmp_w1, nsa_cmp_w2,
           pool_w, pool_scale, w_out, ln1_g, ln1_b, w_up, w_down, ln2_g, ln2_b):
    bp, t, _ = x_prompt.shape
    bs = x_sample.shape[0]
    depth = w_in.shape[0]
    page = cache_nsa_kv.shape[2]
    past_len = page_table.shape[1] * page
    win_len = state_nsa_win.shape[2]
    assert x_sample.shape[1] == 1 and win_len == WINDOW and t % 512 == 0 and t >= WINDOW + 128
    assert page % L_SEL == 0 and MOBA_BLOCK % page == 0 and past_len % MOBA_BLOCK == 0
    assert past_len // MOBA_BLOCK >= MOBA_TOPK and past_len // L_SEL + 1 >= N_SEL and past_len >= POOL_MAX
    assert page_table.shape[1] % _CMP_PAGES == 0 and page_table.shape[1] % _MOBA_PAGES == 0

    nsa_t = jnp.transpose(cache_nsa_kv, (0, 1, 3, 4, 2))
    moba_t = jnp.transpose(cache_moba_kv, (0, 1, 3, 4, 5, 2))
    win_t = jnp.transpose(state_nsa_win, (0, 1, 3, 4, 2))
    conv_t = jnp.transpose(state_conv, (0, 2, 1, 3))
    pool_t = jnp.transpose(state_pool, (0, 2, 1, 3))

    row = lambda v: v.reshape(1, -1)
    yp = x_prompt.reshape(bp * t, D_MODEL)
    ys = x_sample.reshape(bs, D_MODEL)
    st_p, st_s = [], []
    w_out_b, w_up_b, w_down_b = w_out.astype(BF16), w_up.astype(BF16), w_down.astype(BF16)
    for l in range(depth):
        w_in_l = _in_weights(w_in[l])
        pw = conv_pw[l].astype(BF16)
        plw = jax.scipy.linalg.block_diag(*[pool_w[l, g] for g in range(len(POOL_WINDOWS))]).astype(BF16)
        w1 = nsa_cmp_w1[l].astype(BF16)
        w2 = nsa_cmp_w2[l].astype(BF16)
        pos = jnp.broadcast_to(nsa_cmp_pos[l].reshape(2, 1, L_CMP * HEAD_DIM), (2, 8, L_CMP * HEAD_DIM))
        small = (conv_dw[l], row(conv_dw_b[l]), row(conv_ln_g[l]), row(conv_ln_b[l]), pw, plw, row(pool_scale[l]))

        z, kvm, kvn, kvw = _in_proj_split(yp, w_in_l)
        z3 = z.reshape(bp, t, N_MAIN)
        kvm3 = kvm.reshape(bp, t, 2 * GROUP_W)
        kvn3 = kvn.reshape(bp, t, GROUP_W)
        win3 = kvw.reshape(bp, t, 2 * HEAD_DIM)
        y_conv, y_pool, conv_new32 = _conv_pool_prompt(z3, *small)
        wstack = _cmp_weight_stack(nsa_cmp_w1[l])
        kc, vc = _compress_prompt(kvn3, pos, w1, w2, wstack)
        o_nsa = _nsa_prompt(z3, kvn3, win3, kc, vc)
        o_moba = _moba_prompt(z3, kvm3)
        flat = lambda a: a.reshape(bp * t, GROUP_W)
        x1 = _out_proj((flat(y_conv), flat(o_nsa), flat(o_moba), flat(y_pool)), yp, w_out_b, l,
                       row(ln1_g[l]), row(ln1_b[l]))
        yp = _ffn(x1, w_up_b, w_down_b, l, row(ln2_g[l]), row(ln2_b[l]))
        moba_rows, nsa_rows = kvm3, kvn3
        win_new = win3[:, t - win_len:].reshape(bp, win_len, 2, HEAD_DIM)
        conv_new = conv_new32[:, _CONV_HALO - (CONV_W - 1):]
        pool_new = z3[:, t - (POOL_MAX - 1):, OFF_UPOOL:OFF_UPOOL + GROUP_W]
        st_p.append((moba_rows, nsa_rows, win_new, conv_new, pool_new))

        zs = _in_proj(ys, w_in_l)
        zs3 = zs.reshape(bs, 1, N_IN_PAD)
        ys_conv, ys_pool, conv_new_t, pool_new_t = _conv_pool_sample(zs, conv_t[l], pool_t[l], *small)
        c = _compress_pages(page_table, nsa_t, wstack, l)
        o_cmp, sel_idx = _nsa_select(c, zs3, pos, w1, w2, past_len)
        os_nsa, win_new_t = _nsa_sample(page_table, sel_idx, nsa_t, zs3, o_cmp, win_t, l, past_len)
        q_m = zs[:, OFF_QMOBA:OFF_QMOBA + GROUP_W]
        qcol = jnp.broadcast_to(q_m[:, :, None], (bs, GROUP_W, page))
        top_idx = _moba_gate(page_table, moba_t, qcol, l, past_len // MOBA_BLOCK)
        os_moba = _moba_sample(page_table, top_idx.reshape(bs, LANES), moba_t, zs3, l, past_len // MOBA_BLOCK)
        x1s = _out_proj((ys_conv, os_nsa.reshape(bs, GROUP_W), os_moba.reshape(bs, GROUP_W), ys_pool), ys, w_out_b, l,
                        row(ln1_g[l]), row(ln1_b[l]))
        ys = _ffn(x1s, w_up_b, w_down_b, l, row(ln2_g[l]), row(ln2_b[l]))
        st_s.append((zs[:, OFF_KVMOBA:OFF_KVMOBA + 2 * GROUP_W].reshape(bs, 1, 2, N_HEADS, HEAD_DIM),
                     zs[:, OFF_KVNSA:OFF_KVNSA + GROUP_W].reshape(bs, 1, 4, HEAD_DIM),
                     jnp.transpose(win_new_t, (0, 3, 1, 2)),
                     jnp.transpose(conv_new_t, (1, 0, 2)),
                     jnp.transpose(pool_new_t, (1, 0, 2))))

    stk = lambda sts, i: jnp.stack([s[i] for s in sts], axis=0)
    moba_state = _kv_rows_to_state([s[0] for s in st_p]).reshape(depth, bp, 2, N_HEADS, HEAD_DIM, t)
    nsa_state = _kv_rows_to_state([s[1] for s in st_p]).reshape(depth, bp, 4, HEAD_DIM, t)
    return (yp.reshape(bp, t, D_MODEL), ys.reshape(bs, 1, D_MODEL),
            jnp.transpose(moba_state, (0, 1, 5, 2, 3, 4)), stk(st_s, 0),
            jnp.transpose(nsa_state, (0, 1, 4, 2, 3)), stk(st_s, 1), stk(st_p, 2), stk(st_s, 2),
            stk(st_p, 3), stk(st_s, 3), stk(st_p, 4), stk(st_s, 4))
```

```python
import functools

import numpy as np
import jax
import jax.numpy as jnp
from jax import lax
from jax.experimental import pallas as pl
from jax.experimental.pallas import tpu as pltpu

F32 = jnp.float32
BF16 = jnp.bfloat16
HIGHEST = lax.Precision.HIGHEST

D_MODEL = 1024
HEAD_DIM = 64
GROUP_W = 256
N_HEADS = 4
CONV_W = 31
L_CMP = 32
CMP_STRIDE = 16
L_SEL = 64
N_SEL = 16
WINDOW = 512
MOBA_BLOCK = 256
MOBA_TOPK = 3
POOL_WINDOWS = (2, 4, 8, 16)
POOL_GW = 64
POOL_MAX = 16
D_FF = 4096
DEPTH = 2
ALPHA = (2 * DEPTH) ** 0.25
LN_EPS = 1e-5
ATTN_SCALE = HEAD_DIM ** -0.5
NEG = -1e30

OFF_UCONV, OFF_QNSA, OFF_QMOBA, OFF_UPOOL, OFF_GATE = 0, 512, 768, 1024, 1280
N_MAIN = 1408
OFF_KVMOBA, OFF_KVNSA, OFF_KVWIN = 1408, 1920, 2176
N_IN_PAD = 2304
N_GATE = 12
LANES = 128
VMEM_LIMIT = 56 * 1024 * 1024


def _in_weights(w):
    main = np.concatenate([np.arange(0, 512), np.arange(512, 768), np.arange(1164, 1420), np.arange(1932, 2188),
                           np.arange(1152, 1164)])
    kv = np.concatenate([np.arange(1420, 1932), np.arange(768, 1024), np.arange(1024, 1152)])
    pad = jnp.zeros((w.shape[0], LANES - N_GATE), w.dtype)
    return jnp.concatenate([w[:, main], pad, w[:, kv]], axis=1).astype(BF16)


def _ln(x, g, b):
    mu = jnp.mean(x, axis=-1, keepdims=True)
    xc = x - mu
    var = jnp.mean(xc * xc, axis=-1, keepdims=True)
    return xc * lax.rsqrt(var + LN_EPS) * g + b


def _sigmoid(x):
    return 1.0 / (1.0 + jnp.exp(-x))


def _silu(x):
    return x * _sigmoid(x)


def _dot(a, b, **kw):
    return jnp.dot(a, b, preferred_element_type=F32, **kw)


def _dot_nt(a, b, **kw):
    return lax.dot_general(a, b, (((1,), (1,)), ((), ())), preferred_element_type=F32, **kw)


def _masked_softmax(s, mask):
    r = s.shape[0]
    s = jnp.where(mask, s, -jnp.inf)
    m = jnp.max(_lane_partial(s, jnp.maximum), axis=-1, keepdims=True)
    m = jnp.broadcast_to(jnp.where(m == -jnp.inf, 0.0, m), (r, LANES))
    e = _exp_shifted(s, m)
    d = jnp.sum(_lane_partial(e, jnp.add), axis=-1, keepdims=True)
    inv = jnp.broadcast_to(1.0 / jnp.where(d > 0, d, 1.0), (r, LANES))
    return jnp.concatenate([e[:, g * LANES:(g + 1) * LANES] * inv for g in range(s.shape[1] // LANES)], axis=-1)


def _lane_partial(s, op):
    out = s[:, 0:LANES]
    for g in range(1, s.shape[1] // LANES):
        out = op(out, s[:, g * LANES:(g + 1) * LANES])
    return out


def _exp_shifted(s, m):
    return jnp.concatenate([jnp.exp(s[:, g * LANES:(g + 1) * LANES] - m) for g in range(s.shape[1] // LANES)], axis=-1)


def _sweep(count, tiles_fn, width=2):
    def wide(jj, carry):
        tiles_fn(width * jj, width)
        return carry

    lax.fori_loop(0, count // width, wide, 0)
    done = (count // width) * width
    n = width // 2
    while n >= 1:
        take = ((count - done) // n) % 2 == 1

        @pl.when(take)
        def _(done=done, n=n):
            tiles_fn(done, n)

        done = done + jnp.where(take, n, 0)
        n //= 2


def _cparams(sem, vmem=None):
    return pltpu.CompilerParams(dimension_semantics=sem, vmem_limit_bytes=vmem or VMEM_LIMIT)


def _in_proj_kernel(x_ref, w_ref, o_ref):
    o_ref[...] = _dot(x_ref[...].astype(BF16), w_ref[...])


def _in_proj(x, w):
    m, k = x.shape
    n = w.shape[1]
    tm = min(m, 512)
    return pl.pallas_call(
        _in_proj_kernel,
        grid=(m // tm,),
        in_specs=[pl.BlockSpec((tm, k), lambda i: (i, 0)), pl.BlockSpec((k, n), lambda i: (0, 0))],
        out_specs=pl.BlockSpec((tm, n), lambda i: (i, 0)),
        out_shape=jax.ShapeDtypeStruct((m, n), F32),
        compiler_params=_cparams(("parallel",)),
        name="in_proj",
    )(x, w)


_IN_PARTS = ((0, N_MAIN), (OFF_KVMOBA, OFF_KVNSA), (OFF_KVNSA, OFF_KVWIN), (OFF_KVWIN, N_IN_PAD))


def _in_proj_split_kernel(x_ref, w_ref, *o_refs):
    z = _dot(x_ref[...].astype(BF16), w_ref[...])
    for (lo, hi), o_ref in zip(_IN_PARTS, o_refs):
        o_ref[...] = z[:, lo:hi]


def _in_proj_split(x, w):
    m, k = x.shape
    tm = min(m, 512)
    return pl.pallas_call(
        _in_proj_split_kernel,
        grid=(m // tm,),
        in_specs=[pl.BlockSpec((tm, k), lambda i: (i, 0)), pl.BlockSpec(w.shape, lambda i: (0, 0))],
        out_specs=[pl.BlockSpec((tm, hi - lo), lambda i: (i, 0)) for lo, hi in _IN_PARTS],
        out_shape=[jax.ShapeDtypeStruct((m, hi - lo), F32) for lo, hi in _IN_PARTS],
        compiler_params=_cparams(("parallel",)),
        name="in_proj_split",
    )(x, w)


def _out_proj_kernel(a_ref, b_ref, c_ref, d_ref, x_ref, w_ref, g_ref, bt_ref, o_ref):
    mix = jnp.concatenate([a_ref[...], b_ref[...], c_ref[...], d_ref[...]], axis=-1).astype(BF16)
    y = _dot(mix, w_ref[...])
    o_ref[...] = _ln(ALPHA * x_ref[...] + y, g_ref[...], bt_ref[...])


def _out_proj(pieces, x, w, layer, g, b):
    m = x.shape[0]
    tm = min(m, 512)
    pspec = pl.BlockSpec((tm, GROUP_W), lambda i: (i, 0))
    return pl.pallas_call(
        _out_proj_kernel,
        grid=(m // tm,),
        in_specs=[pspec, pspec, pspec, pspec,
                  pl.BlockSpec((tm, D_MODEL), lambda i: (i, 0)),
                  pl.BlockSpec((None, D_MODEL, D_MODEL), lambda i: (layer, 0, 0)),
                  pl.BlockSpec((1, D_MODEL), lambda i: (0, 0)),
                  pl.BlockSpec((1, D_MODEL), lambda i: (0, 0))],
        out_specs=pl.BlockSpec((tm, D_MODEL), lambda i: (i, 0)),
        out_shape=jax.ShapeDtypeStruct((m, D_MODEL), F32),
        compiler_params=_cparams(("parallel",)),
        name="out_proj_ln",
    )(*pieces, x, w, g, b)


def _ffn_kernel(x_ref, wu_ref, wd_ref, g_ref, b_ref, o_ref, acc_ref):
    j = pl.program_id(1)

    @pl.when(j == 0)
    def _():
        acc_ref[...] = jnp.zeros_like(acc_ref)

    h = jnp.maximum(_dot(x_ref[...].astype(BF16), wu_ref[...]), 0.0)
    acc_ref[...] += _dot((h * h).astype(BF16), wd_ref[...])

    @pl.when(j == pl.num_programs(1) - 1)
    def _():
        o_ref[...] = _ln(ALPHA * x_ref[...] + acc_ref[...], g_ref[...], b_ref[...])


def _ffn(x, wu, wd, layer, g, b):
    m = x.shape[0]
    tm = min(m, 1024)
    tf = 1024
    return pl.pallas_call(
        _ffn_kernel,
        grid=(m // tm, D_FF // tf),
        in_specs=[pl.BlockSpec((tm, D_MODEL), lambda i, j: (i, 0)),
                  pl.BlockSpec((None, D_MODEL, tf), lambda i, j: (layer, 0, j)),
                  pl.BlockSpec((None, tf, D_MODEL), lambda i, j: (layer, j, 0)),
                  pl.BlockSpec((1, D_MODEL), lambda i, j: (0, 0)),
                  pl.BlockSpec((1, D_MODEL), lambda i, j: (0, 0))],
        out_specs=pl.BlockSpec((tm, D_MODEL), lambda i, j: (i, 0)),
        out_shape=jax.ShapeDtypeStruct((m, D_MODEL), F32),
        scratch_shapes=[pltpu.VMEM((tm, D_MODEL), F32)],
        compiler_params=_cparams(("parallel", "arbitrary")),
        name="ffn_ln",
    )(x, wu, wd, g, b)


def _kv_rows_to_state_kernel(*refs):
    layer = pl.program_id(0)
    o_ref = refs[-1]
    for l, x_ref in enumerate(refs[:-1]):
        @pl.when(layer == l)
        def _():
            o_ref[...] = x_ref[...].T


def _kv_rows_to_state(per_layer):
    depth = len(per_layer)
    b, t, width = per_layer[0].shape
    tile = 1024 if t % 1024 == 0 else 512
    nt = t // tile

    def in_spec(l):
        def index(ll, bi, i):
            before, after = ll < l, ll > l
            return (jnp.where(before, 0, jnp.where(after, b - 1, bi)),
                    jnp.where(before, 0, jnp.where(after, nt - 1, i)), 0)
        return pl.BlockSpec((None, tile, width), index)

    return pl.pallas_call(
        _kv_rows_to_state_kernel,
        grid=(depth, b, nt),
        in_specs=[in_spec(l) for l in range(depth)],
        out_specs=pl.BlockSpec((None, None, width, tile), lambda ll, bi, i: (ll, bi, 0, i)),
        out_shape=jax.ShapeDtypeStruct((depth, b, width, t), F32),
        compiler_params=_cparams(("arbitrary", "arbitrary", "arbitrary")),
        name="kv_rows_to_state",
    )(*per_layer)


_CONV_HALO = 32
_POOL_HALO = 16


def _pool_groups(rows_ref, buf_ref, tt, cnt_fn):
    n = _POOL_HALO + tt
    lane = lax.broadcasted_iota(jnp.int32, (1, GROUP_W), 1)
    cur = rows_ref[pl.ds(_POOL_HALO, tt), :]
    out = jnp.zeros((tt, GROUP_W), F32)
    shift = 1
    for g, w in enumerate(POOL_WINDOWS):
        src = rows_ref if g == 0 else buf_ref.at[(g - 1) % 2]
        dst = buf_ref.at[g % 2]
        lo = 2 * shift - 1
        dst[pl.ds(lo, n - lo), :] = src[pl.ds(lo, n - lo), :] + src[pl.ds(lo - shift, n - lo), :]
        shift *= 2
        s = dst[pl.ds(_POOL_HALO, tt), :]
        mean = s / cnt_fn(w)
        out = jnp.where((lane >= g * POOL_GW) & (lane < (g + 1) * POOL_GW), mean, out)
    return out - cur


def _conv_pool_prompt_kernel(u_ref, uh_ref, p_ref, ph_ref, dw_ref, dwb_ref, lg_ref, lb_ref, pw_ref,
                             plw_ref, pls_ref, yc_ref, yp_ref, cnew_ref, zc_ref, res_ref, rows_ref, buf_ref, *, tt):
    i = pl.program_id(1)
    first = i == 0
    u = u_ref[...]
    zg = u[:, :GROUP_W] * _sigmoid(u[:, GROUP_W:])
    uh = uh_ref[...]
    zh = uh[:, :GROUP_W] * _sigmoid(uh[:, GROUP_W:])
    zc_ref[pl.ds(0, _CONV_HALO), :] = jnp.where(first, 0.0, zh)
    zc_ref[pl.ds(_CONV_HALO, tt), :] = zg
    zc_ref[pl.ds(_CONV_HALO + tt, 8), :] = jnp.zeros((8, GROUP_W), F32)
    off = _CONV_HALO - (CONV_W - 1)
    y = jnp.zeros((tt, GROUP_W), F32) + dwb_ref[...]
    for r in range(8):
        acc = None
        for a in range(-(-(off + CONV_W) // 8)):
            k = 8 * a + r - off
            if 0 <= k < CONV_W:
                term = dw_ref[k:k + 1, :] * zc_ref[pl.ds(8 * a, tt + 8), :]
                acc = term if acc is None else acc + term
        if r == 0:
            y = y + acc[0:tt]
        else:
            res_ref[r - 1] = acc
            y = y + res_ref[r - 1, pl.ds(r, tt), :]
    y = _silu(_ln(y, lg_ref[...], lb_ref[...]))
    yc_ref[...] = _dot(y.astype(BF16), pw_ref[...]).astype(yc_ref.dtype)

    @pl.when(i == pl.num_programs(1) - 1)
    def _():
        cnew_ref[...] = zc_ref[pl.ds(tt, _CONV_HALO), :]

    rows_ref[pl.ds(0, _POOL_HALO), :] = jnp.where(first, 0.0, ph_ref[...])
    rows_ref[pl.ds(_POOL_HALO, tt), :] = p_ref[...]
    qpos1 = i * tt + lax.broadcasted_iota(jnp.int32, (tt, 1), 0) + 1
    d = _pool_groups(rows_ref, buf_ref, tt, lambda w: jnp.minimum(w, qpos1).astype(F32))
    yp_ref[...] = (_dot(d.astype(BF16), plw_ref[...]) * pls_ref[...]).astype(yp_ref.dtype)


def _conv_pool_prompt(z3, dw, dwb, lg, lb, pw, plw, pls):
    b, t, _ = z3.shape
    tt = 512
    nt = t // tt
    kern = functools.partial(_conv_pool_prompt_kernel, tt=tt)
    cst = lambda shape: pl.BlockSpec(shape, lambda bi, i: (0,) * len(shape))
    return pl.pallas_call(
        kern,
        grid=(b, nt),
        in_specs=[
            pl.BlockSpec((None, tt, 512), lambda bi, i: (bi, i, OFF_UCONV // 512)),
            pl.BlockSpec((None, _CONV_HALO, 512),
                         lambda bi, i: (bi, jnp.maximum(i * (tt // _CONV_HALO) - 1, 0), OFF_UCONV // 512)),
            pl.BlockSpec((None, tt, GROUP_W), lambda bi, i: (bi, i, OFF_UPOOL // GROUP_W)),
            pl.BlockSpec((None, _POOL_HALO, GROUP_W),
                         lambda bi, i: (bi, jnp.maximum(i * (tt // _POOL_HALO) - 1, 0), OFF_UPOOL // GROUP_W)),
            cst((CONV_W, GROUP_W)), cst((1, GROUP_W)), cst((1, GROUP_W)), cst((1, GROUP_W)),
            cst((GROUP_W, GROUP_W)), cst((GROUP_W, GROUP_W)), cst((1, GROUP_W)),
        ],
        out_specs=[
            pl.BlockSpec((None, tt, GROUP_W), lambda bi, i: (bi, i, 0)),
            pl.BlockSpec((None, tt, GROUP_W), lambda bi, i: (bi, i, 0)),
            pl.BlockSpec((None, _CONV_HALO, GROUP_W), lambda bi, i: (bi, 0, 0)),
        ],
        out_shape=[jax.ShapeDtypeStruct((b, t, GROUP_W), BF16), jax.ShapeDtypeStruct((b, t, GROUP_W), BF16),
                   jax.ShapeDtypeStruct((b, _CONV_HALO, GROUP_W), F32)],
        scratch_shapes=[pltpu.VMEM((_CONV_HALO + tt + 8, GROUP_W), F32),
                        pltpu.VMEM((7, tt + 8, GROUP_W), F32),
                        pltpu.VMEM((_POOL_HALO + tt, GROUP_W), F32),
                        pltpu.VMEM((2, _POOL_HALO + tt, GROUP_W), F32)],
        compiler_params=_cparams(("parallel", "arbitrary")),
        name="conv_pool_prompt",
    )(z3, z3, z3, z3, dw, dwb, lg, lb, pw, plw, pls)


def _compress_tail(c, bias_k, bias_v, w2k, w2v):
    n = c.shape[0]

    def one(c0, c1, bias, w2):
        hid = c0 + pltpu.roll(c1, n - 1, 0) + bias
        return _dot(_silu(hid).astype(BF16), w2)

    kc = one(c[:, 0:64], c[:, 64:128], bias_k, w2k)
    vc = one(c[:, 128:192], c[:, 192:256], bias_v, w2v)
    return kc, vc


def _cmp_bias(pos_ref, w1_ref):
    bk = _dot(pos_ref[0].astype(BF16), w1_ref[0])[0:1]
    bv = _dot(pos_ref[1].astype(BF16), w1_ref[1])[0:1]
    return bk, bv


def _compress_prompt_kernel(kv_ref, pos_ref, w1_ref, w2_ref, ws_ref, kc_ref, vc_ref):
    bk, bv = _cmp_bias(pos_ref, w1_ref)
    nc = kv_ref.shape[0] // CMP_STRIDE
    c = jnp.zeros((nc, 4 * HEAD_DIM), F32)
    for r in range(0, CMP_STRIDE, 2):
        xr = jnp.concatenate([kv_ref[pl.ds(r, nc, stride=CMP_STRIDE), :],
                              kv_ref[pl.ds(r + 1, nc, stride=CMP_STRIDE), :]], axis=-1).astype(BF16)
        c = c + _dot(xr, ws_ref[r // 2])
    kc, vc = _compress_tail(c, bk, bv, w2_ref[0], w2_ref[1])
    kc_ref[...] = kc
    vc_ref[...] = vc


def _compress_prompt(kvn3, pos, w1, w2, wstack):
    b, t, width = kvn3.shape
    nc = t // CMP_STRIDE
    cst = lambda shape: pl.BlockSpec(shape, lambda bi: (0,) * len(shape))
    return pl.pallas_call(
        _compress_prompt_kernel,
        grid=(b,),
        in_specs=[pl.BlockSpec((None, t, 2 * HEAD_DIM), lambda bi: (bi, 0, 0)),
                  cst(pos.shape), cst(w1.shape), cst(w2.shape), cst(wstack.shape)],
        out_specs=[pl.BlockSpec((None, nc, HEAD_DIM), lambda bi: (bi, 0, 0)),
                   pl.BlockSpec((None, nc, HEAD_DIM), lambda bi: (bi, 0, 0))],
        out_shape=[jax.ShapeDtypeStruct((b, nc, HEAD_DIM), F32)] * 2,
        compiler_params=_cparams(("parallel",)),
        name="compress_prompt",
    )(kvn3, pos, w1, w2, wstack)


def _cover_matrix(n_cmp_pad, n_cmp, n_sel_pad):
    j = np.arange(n_cmp_pad)[:, None]
    i = np.arange(n_sel_pad)[None, :]
    cov = (j * CMP_STRIDE < (i + 1) * L_SEL) & (j * CMP_STRIDE + L_CMP > i * L_SEL) & (j < n_cmp)
    return jnp.asarray(cov.astype(np.float32))


def _nsa_prompt_kernel(q_ref, g_ref, kv_ref, win_ref, kc_ref, vc_ref, covert_ref, o_ref,
                       kt_ref, v_ref, kwt_ref, vw_ref, qa_ref, so_ref, sp_ref, m_ref, l_ref, acc_ref, ow_ref,
                       *, c, tk, n_cmp, ns):
    i = pl.program_id(1)
    qs = i * c
    t = kv_ref.shape[0]
    nsp = LANES - HEAD_DIM

    @pl.when(i == 0)
    def _():
        rows = lax.broadcasted_iota(jnp.int32, (nsp, t), 0)
        cols = lax.broadcasted_iota(jnp.int32, (nsp, t), 1)
        kt_ref[HEAD_DIM:LANES, :] = jnp.where(rows == cols // L_SEL, 1.0, 0.0).astype(BF16)
        rt = 256
        for n in range(t // rt):
            x = kv_ref[n * rt:(n + 1) * rt, :]
            kt_ref[0:HEAD_DIM, n * rt:(n + 1) * rt] = x.T[2 * HEAD_DIM:3 * HEAD_DIM].astype(BF16)
            v_ref[n * rt:(n + 1) * rt, :] = x[:, 3 * HEAD_DIM:4 * HEAD_DIM].astype(BF16)
            w = win_ref[n * rt:(n + 1) * rt, :]
            kwt_ref[:, n * rt:(n + 1) * rt] = w.T[0:HEAD_DIM].astype(BF16)
            vw_ref[n * rt:(n + 1) * rt, :] = w[:, HEAD_DIM:2 * HEAD_DIM].astype(BF16)

    q = q_ref[...] * ATTN_SCALE
    q4 = jnp.concatenate([q[:, h * HEAD_DIM:(h + 1) * HEAD_DIM] for h in range(N_HEADS)], axis=0).astype(BF16)
    qpos = qs + lax.broadcasted_iota(jnp.int32, (c, 1), 0)
    qpos4 = jnp.concatenate([qpos] * N_HEADS, axis=0)

    wl = WINDOW + c
    ks = pl.multiple_of(jnp.maximum(qs - WINDOW, 0), c)
    rel = qpos - (ks + lax.broadcasted_iota(jnp.int32, (1, wl), 1))
    band = jnp.where((rel >= 0) & (rel <= WINDOW), 0.0, NEG)
    sw = _dot(q4, kwt_ref[:, pl.ds(ks, wl)]) + jnp.concatenate([band] * N_HEADS, axis=0)
    mw = jnp.broadcast_to(jnp.max(_lane_partial(sw, jnp.maximum), axis=-1, keepdims=True), (N_HEADS * c, LANES))
    ew = _exp_shifted(sw, mw)
    lw = jnp.sum(_lane_partial(ew, jnp.add), axis=-1, keepdims=True)
    ow_ref[...] = _dot(ew.astype(BF16), vw_ref[pl.ds(ks, wl), :]) * (1.0 / lw)

    nc = kc_ref.shape[0]
    s = _dot_nt(q4, kc_ref[...].astype(BF16))
    jj = lax.broadcasted_iota(jnp.int32, (1, nc), 1)
    p = _masked_softmax(s, (jj * CMP_STRIDE + (L_CMP - 1) <= qpos4) & (jj < n_cmp))
    o_cmp = _dot(p.astype(BF16), vc_ref[...].astype(BF16))
    psum = p[0:c] + p[c:2 * c] + p[2 * c:3 * c] + p[3 * c:4 * c]

    imp = _dot_nt(covert_ref[...], psum, precision=HIGHEST)
    blk = lax.broadcasted_iota(jnp.int32, (nsp, 1), 0)
    qpos_l = qs + lax.broadcasted_iota(jnp.int32, (1, c), 1)
    cur = qpos_l // L_SEL
    imp = jnp.where((blk == 0) | (blk == cur) | (blk == cur - 1), jnp.inf, imp)
    imp = jnp.where(blk * L_SEL > qpos_l, -jnp.inf, imp)
    sub = lax.broadcasted_iota(jnp.int32, (8, 1), 0)
    groups = [imp[8 * g:8 * (g + 1), :] for g in range(nsp // 8)]
    ranks = [jnp.zeros((8, c), F32) for _ in groups]
    for b in range(ns):
        rb = imp[b:b + 1, :]
        for g, v in enumerate(groups):
            if 8 * g > b:
                ahead = rb >= v
            elif 8 * g + 7 <= b:
                ahead = rb > v
            else:
                ahead = (rb > v) | ((rb == v) & (sub + 8 * g > b))
            ranks[g] = ranks[g] + jnp.where(ahead, 1.0, 0.0)
    rank = jnp.concatenate(ranks, axis=0)
    sel = (rank < N_SEL) & (blk * L_SEL <= qpos_l)
    bias_t = jnp.where(sel, 0.0, NEG)
    bias = jnp.concatenate([bias_t, jnp.zeros((LANES - nsp, c), F32)], axis=0).T[:, 0:nsp]
    qa = jnp.concatenate([q4, jnp.concatenate([bias] * N_HEADS, axis=0).astype(BF16)], axis=-1)

    jt = qs // tk
    k_own = pl.multiple_of(jt * tk, tk)
    qa_ref[...] = qa
    kpos = k_own + lax.broadcasted_iota(jnp.int32, (1, tk), 1)
    causal = jnp.where(kpos <= qpos, 0.0, NEG)
    sc = _dot(qa, kt_ref[:, pl.ds(k_own, tk)]) + jnp.concatenate([causal] * N_HEADS, axis=0)
    so_ref[...] = sc
    m_ref[...] = _lane_partial(sc, jnp.maximum)

    def max_tiles(j0, n):
        k0 = pl.multiple_of(j0 * tk, tk)
        sc = _dot(qa_ref[...], kt_ref[:, pl.ds(k0, n * tk)])
        for u in range(n):
            sp_ref[j0 + u] = sc[:, u * tk:(u + 1) * tk]
        m_ref[...] = jnp.maximum(m_ref[...], _lane_partial(sc, jnp.maximum))

    _sweep(jt, max_tiles, 4)
    m = jnp.broadcast_to(jnp.max(m_ref[...], axis=-1, keepdims=True), m_ref.shape)
    m_ref[...] = m
    pe = _exp_shifted(so_ref[...], m)
    l_ref[...] = _lane_partial(pe, jnp.add)
    acc_ref[...] = _dot(pe.astype(BF16), v_ref[pl.ds(k_own, tk), :])

    def sum_tiles(j0, n):
        k0 = pl.multiple_of(j0 * tk, tk)
        pe = jnp.concatenate([_exp_shifted(sp_ref[j0 + u], m_ref[...]) for u in range(n)], axis=-1)
        l_ref[...] += _lane_partial(pe, jnp.add)
        acc_ref[...] += _dot(pe.astype(BF16), v_ref[pl.ds(k0, n * tk), :])

    _sweep(jt, sum_tiles)
    o_sel = acc_ref[...] * (1.0 / jnp.sum(l_ref[...], axis=-1, keepdims=True))

    o_win = ow_ref[...]
    g = _sigmoid(g_ref[...])

    def gate(br):
        return jnp.concatenate([g[:, br * N_HEADS + h:br * N_HEADS + h + 1] for h in range(N_HEADS)], axis=0)

    o = gate(0) * o_cmp + gate(1) * o_sel + gate(2) * o_win
    for h in range(N_HEADS):
        o_ref[:, h * HEAD_DIM:(h + 1) * HEAD_DIM] = o[h * c:(h + 1) * c].astype(o_ref.dtype)


def _nsa_prompt(z3, kvn3, win3, kc, vc):
    b, t, _ = z3.shape
    c = 128
    tk = 512
    nc = kc.shape[1]
    n_cmp = t // CMP_STRIDE - 1
    ns = t // L_SEL
    nsp = LANES - HEAD_DIM
    assert ns <= nsp and c == LANES
    covert = _cover_matrix(nc, n_cmp, nsp).T
    kern = functools.partial(_nsa_prompt_kernel, c=c, tk=tk, n_cmp=n_cmp, ns=ns)
    return pl.pallas_call(
        kern,
        grid=(b, t // c),
        in_specs=[
            pl.BlockSpec((None, c, GROUP_W), lambda bi, i: (bi, i, OFF_QNSA // GROUP_W)),
            pl.BlockSpec((None, c, LANES), lambda bi, i: (bi, i, OFF_GATE // LANES)),
            pl.BlockSpec((None, t, GROUP_W), lambda bi, i: (bi, 0, 0)),
            pl.BlockSpec((None, t, LANES), lambda bi, i: (bi, 0, 0)),
            pl.BlockSpec((None, nc, HEAD_DIM), lambda bi, i: (bi, 0, 0)),
            pl.BlockSpec((None, nc, HEAD_DIM), lambda bi, i: (bi, 0, 0)),
            pl.BlockSpec((nsp, nc), lambda bi, i: (0, 0)),
        ],
        out_specs=pl.BlockSpec((None, c, GROUP_W), lambda bi, i: (bi, i, 0)),
        out_shape=jax.ShapeDtypeStruct((b, t, GROUP_W), BF16),
        scratch_shapes=[pltpu.VMEM((LANES, t), BF16), pltpu.VMEM((t, HEAD_DIM), BF16),
                        pltpu.VMEM((HEAD_DIM, t), BF16), pltpu.VMEM((t, HEAD_DIM), BF16),
                        pltpu.VMEM((N_HEADS * c, LANES), BF16), pltpu.VMEM((N_HEADS * c, tk), F32),
                        pltpu.VMEM((t // tk - 1, N_HEADS * c, tk), F32),
                        pltpu.VMEM((N_HEADS * c, LANES), F32), pltpu.VMEM((N_HEADS * c, LANES), F32),
                        pltpu.VMEM((N_HEADS * c, HEAD_DIM), F32), pltpu.VMEM((N_HEADS * c, HEAD_DIM), F32)],
        compiler_params=_cparams(("parallel", "arbitrary")),
        name="nsa_prompt",
    )(z3, z3, kvn3, win3, kc, vc, covert)


def _moba_prompt_kernel(q_ref, kv_ref, o_ref, kmean_ref, kt_ref, v_ref, qa_ref, so_ref, sp_ref, m_ref, l_ref, acc_ref,
                        *, nb):
    i = pl.program_id(1)
    c = MOBA_BLOCK
    t = kv_ref.shape[0]

    @pl.when(i == 0)
    def _():
        rows = lax.broadcasted_iota(jnp.int32, (LANES - HEAD_DIM, t), 0)
        cols = lax.broadcasted_iota(jnp.int32, (LANES - HEAD_DIM, t), 1)
        onehot = jnp.where(rows == cols // c, 1.0, 0.0).astype(BF16)
        for h in range(N_HEADS):
            kt_ref[h, HEAD_DIM:LANES, :] = onehot
        kmean_ref[...] = jnp.zeros(kmean_ref.shape, F32)
        for n in range(nb):
            x = kv_ref[n * c:(n + 1) * c, :]
            kmean_ref[n:n + 1, :] = jnp.mean(x[:, 0:GROUP_W], axis=0, keepdims=True)
            xt = x[:, 0:GROUP_W].T
            for h in range(N_HEADS):
                lo, hi = h * HEAD_DIM, (h + 1) * HEAD_DIM
                kt_ref[h, 0:HEAD_DIM, n * c:(n + 1) * c] = xt[lo:hi].astype(BF16)
                v_ref[h, n * c:(n + 1) * c, :] = x[:, GROUP_W + lo:GROUP_W + hi].astype(BF16)

    q = q_ref[...]
    nbp = LANES - HEAD_DIM
    nbr = -(-nb // 8) * 8
    blk = lax.broadcasted_iota(jnp.int32, (nbr, 1), 0)
    fblk = blk.astype(F32)
    past = blk < i
    row = lax.broadcasted_iota(jnp.int32, (c, c), 0)
    col = lax.broadcasted_iota(jnp.int32, (c, c), 1)
    tri = col <= row
    own0 = pl.multiple_of(i * c, c)
    for h in range(N_HEADS):
        lo, hi = h * HEAD_DIM, (h + 1) * HEAD_DIM
        qh = q[:, lo:hi]
        work = jnp.where(past, _dot_nt(kmean_ref[0:nbr, lo:hi], qh, precision=HIGHEST), -jnp.inf)
        sel = jnp.zeros((nbr, c), jnp.bool_)
        for _ in range(MOBA_TOPK):
            mx = jnp.max(work, axis=0, keepdims=True)
            pick = jnp.min(jnp.where(work == mx, fblk, float(nbp)), axis=0, keepdims=True)
            hit = fblk == pick
            sel = sel | (hit & (mx > -jnp.inf))
            work = jnp.where(hit, -jnp.inf, work)
        bias_t = jnp.where(sel, 0.0, NEG)
        bias = jnp.concatenate([bias_t, jnp.full((LANES - nbr, c), NEG, F32)], axis=0).T[:, 0:nbp]
        qs = (qh * ATTN_SCALE).astype(BF16)
        qa_ref[h] = jnp.concatenate([qs, bias.astype(BF16)], axis=-1)
        sc = jnp.where(tri, _dot(qs, kt_ref[h, 0:HEAD_DIM, pl.ds(own0, c)]), NEG)
        so_ref[h] = sc
        m_ref[h] = _lane_partial(sc, jnp.maximum)

    def max_tiles(j0, n):
        k0 = pl.multiple_of(j0 * c, c)
        for h in range(N_HEADS):
            sc = _dot(qa_ref[h], kt_ref[h, :, pl.ds(k0, n * c)])
            for u in range(n):
                sp_ref[h, j0 + u] = sc[:, u * c:(u + 1) * c]
            m_ref[h] = jnp.maximum(m_ref[h], _lane_partial(sc, jnp.maximum))

    _sweep(i, max_tiles, 4)
    for h in range(N_HEADS):
        m = jnp.broadcast_to(jnp.max(m_ref[h], axis=-1, keepdims=True), (c, LANES))
        m_ref[h] = m
        pe = _exp_shifted(so_ref[h], m)
        l_ref[h] = _lane_partial(pe, jnp.add)
        acc_ref[h] = _dot(pe.astype(BF16), v_ref[h, pl.ds(own0, c), :])

    def sum_tiles(j0, n):
        k0 = pl.multiple_of(j0 * c, c)
        for h in range(N_HEADS):
            pe = jnp.concatenate([_exp_shifted(sp_ref[h, j0 + u], m_ref[h]) for u in range(n)], axis=-1)
            l_ref[h] += _lane_partial(pe, jnp.add)
            acc_ref[h] += _dot(pe.astype(BF16), v_ref[h, pl.ds(k0, n * c), :])

    _sweep(i, sum_tiles)
    for h in range(N_HEADS):
        l = jnp.sum(l_ref[h], axis=-1, keepdims=True)
        o_ref[:, h * HEAD_DIM:(h + 1) * HEAD_DIM] = (acc_ref[h] * (1.0 / l)).astype(o_ref.dtype)


def _moba_prompt(z3, kvm3):
    b, t, _ = z3.shape
    nb = t // MOBA_BLOCK
    assert nb <= LANES - HEAD_DIM
    kern = functools.partial(_moba_prompt_kernel, nb=nb)
    return pl.pallas_call(
        kern,
        grid=(b, nb),
        in_specs=[pl.BlockSpec((None, MOBA_BLOCK, GROUP_W), lambda bi, i: (bi, i, OFF_QMOBA // GROUP_W)),
                  pl.BlockSpec((None, t, 2 * GROUP_W), lambda bi, i: (bi, 0, 0))],
        out_specs=pl.BlockSpec((None, MOBA_BLOCK, GROUP_W), lambda bi, i: (bi, i, 0)),
        out_shape=jax.ShapeDtypeStruct((b, t, GROUP_W), BF16),
        scratch_shapes=[pltpu.VMEM((LANES - HEAD_DIM, GROUP_W), F32),
                        pltpu.VMEM((N_HEADS, LANES, t), BF16),
                        pltpu.VMEM((N_HEADS, t, HEAD_DIM), BF16),
                        pltpu.VMEM((N_HEADS, MOBA_BLOCK, LANES), BF16),
                        pltpu.VMEM((N_HEADS, MOBA_BLOCK, MOBA_BLOCK), F32),
                        pltpu.VMEM((N_HEADS, nb - 1, MOBA_BLOCK, MOBA_BLOCK), F32),
                        pltpu.VMEM((N_HEADS, MOBA_BLOCK, LANES), F32),
                        pltpu.VMEM((N_HEADS, MOBA_BLOCK, LANES), F32),
                        pltpu.VMEM((N_HEADS, MOBA_BLOCK, HEAD_DIM), F32)],
        compiler_params=_cparams(("parallel", "arbitrary")),
        name="moba_prompt",
    )(z3, kvm3)


def _conv_pool_sample_kernel(z_ref, cst_ref, pst_ref, dw_ref, dwb_ref, lg_ref, lb_ref, pw_ref, plw_ref, pls_ref,
                             yc_ref, yp_ref, cnew_ref, pnew_ref):
    u = z_ref[:, OFF_UCONV:OFF_UCONV + 2 * GROUP_W]
    zg = u[:, :GROUP_W] * _sigmoid(u[:, GROUP_W:])
    nst = CONV_W - 1
    y = dwb_ref[...] + dw_ref[nst:nst + 1, :] * zg
    for k in range(nst):
        y = y + dw_ref[k:k + 1, :] * cst_ref[k]
    y = _silu(_ln(y, lg_ref[...], lb_ref[...]))
    yc_ref[...] = _dot(y.astype(BF16), pw_ref[...])
    for k in range(nst - 1):
        cnew_ref[k] = cst_ref[k + 1]
    cnew_ref[nst - 1] = zg

    p = z_ref[:, OFF_UPOOL:OFF_UPOOL + GROUP_W]
    npst = POOL_MAX - 1
    lane = lax.broadcasted_iota(jnp.int32, (1, GROUP_W), 1)
    run = p
    mean = jnp.zeros_like(p)
    k = 1
    for g, w in enumerate(POOL_WINDOWS):
        while k < w:
            run = run + pst_ref[npst - k]
            k += 1
        mean = jnp.where((lane >= g * POOL_GW) & (lane < (g + 1) * POOL_GW), run / float(w), mean)
    d = mean - p
    yp_ref[...] = _dot(d.astype(BF16), plw_ref[...]) * pls_ref[...]
    for k in range(npst - 1):
        pnew_ref[k] = pst_ref[k + 1]
    pnew_ref[npst - 1] = p


def _conv_pool_sample(z, cst, pst, dw, dwb, lg, lb, pw, plw, pls):
    s = z.shape[0]
    return pl.pallas_call(
        _conv_pool_sample_kernel,
        out_shape=[jax.ShapeDtypeStruct((s, GROUP_W), F32), jax.ShapeDtypeStruct((s, GROUP_W), F32),
                   jax.ShapeDtypeStruct(cst.shape, F32), jax.ShapeDtypeStruct(pst.shape, F32)],
        compiler_params=pltpu.CompilerParams(vmem_limit_bytes=VMEM_LIMIT),
        name="conv_pool_sample",
    )(z, cst, pst, dw, dwb, lg, lb, pw, plw, pls)


_STREAM_SLOTS = 4
_CMP_PAGES = 16


def _compress_pages_kernel(pt_ref, cache_ref, w_ref, perm_ref, c_ref, buf_ref, rows_ref, sem_ref,
                           *, layer, n_steps, page):
    b = pl.program_id(0)
    s = pl.program_id(1)
    g = _CMP_PAGES
    step = b * n_steps + s
    total = pl.num_programs(0) * n_steps

    def copies(stp, slot):
        bb = stp // n_steps
        ss = stp % n_steps
        return [pltpu.make_async_copy(cache_ref.at[layer, pt_ref[bb, ss * g + k], pl.ds(0, 2)],
                                      buf_ref.at[slot, k], sem_ref.at[slot]) for k in range(g)]

    ahead = _STREAM_SLOTS - 1

    @pl.when(step == 0)
    def _():
        for d in range(ahead):
            @pl.when(d < total)
            def _():
                for cp in copies(d, d):
                    cp.start()

    slot = step % _STREAM_SLOTS

    @pl.when(step + ahead < total)
    def _():
        for cp in copies(step + ahead, (step + ahead) % _STREAM_SLOTS):
            cp.start()

    pltpu.make_async_copy(cache_ref.at[layer, pl.ds(0, g), pl.ds(0, 2)], buf_ref.at[slot], sem_ref.at[slot]).wait()

    nchunk = g * page // CMP_STRIDE
    for k in range(g):
        x = buf_ref[slot, k].reshape(2 * HEAD_DIM, page).astype(BF16)
        rows_ref[k] = _dot(x, perm_ref[...]).T.reshape(CMP_STRIDE, page // CMP_STRIDE, 2 * HEAD_DIM)
    acc = jnp.zeros((nchunk, 4 * HEAD_DIM), F32)
    for r in range(0, CMP_STRIDE, 2):
        xr = jnp.concatenate([rows_ref[:, r].reshape(nchunk, 2 * HEAD_DIM),
                              rows_ref[:, r + 1].reshape(nchunk, 2 * HEAD_DIM)], axis=-1).astype(BF16)
        acc = acc + _dot(xr, w_ref[r // 2])
    c_ref[...] = acc


def _row_perm(page):
    j = np.arange(page)
    src = CMP_STRIDE * (j % (page // CMP_STRIDE)) + j // (page // CMP_STRIDE)
    return jnp.asarray((np.arange(page)[:, None] == src[None, :]).astype(np.float32)).astype(BF16)


def _compress_pages(page_table, cache_t, wstack, layer):
    s, npg = page_table.shape
    page = cache_t.shape[-1]
    g = _CMP_PAGES
    n_steps = npg // g
    nchunk = g * page // CMP_STRIDE
    kern = functools.partial(_compress_pages_kernel, layer=layer, n_steps=n_steps, page=page)
    return pl.pallas_call(
        kern,
        grid_spec=pltpu.PrefetchScalarGridSpec(
            num_scalar_prefetch=1,
            grid=(s, n_steps),
            in_specs=[pl.BlockSpec(memory_space=pl.ANY),
                      pl.BlockSpec(wstack.shape, lambda b, i, pt: (0, 0, 0)),
                      pl.BlockSpec((page, page), lambda b, i, pt: (0, 0))],
            out_specs=pl.BlockSpec((None, nchunk, 4 * HEAD_DIM), lambda b, i, pt: (b, i, 0)),
            scratch_shapes=[pltpu.VMEM((_STREAM_SLOTS, g, 2, HEAD_DIM, page), F32),
                            pltpu.VMEM((g, CMP_STRIDE, page // CMP_STRIDE, 2 * HEAD_DIM), F32),
                            pltpu.SemaphoreType.DMA((_STREAM_SLOTS,))]),
        out_shape=jax.ShapeDtypeStruct((s, n_steps * nchunk, 4 * HEAD_DIM), F32),
        compiler_params=_cparams(("arbitrary", "arbitrary")),
        name="compress_pages",
    )(page_table, cache_t, wstack, _row_perm(page))


def _stack_heads(q):
    rows = [q[:, h * HEAD_DIM:(h + 1) * HEAD_DIM] for h in range(N_HEADS)]
    return jnp.concatenate(rows + [jnp.zeros((8 - N_HEADS, HEAD_DIM), q.dtype)], axis=0)


def _nsa_select_kernel(c_ref, z_ref, pos_ref, w1_ref, w2_ref, cover_ref, ocmp_ref, idx_ref, psum_ref,
                       *, n_cmp, n_sel, qpos):
    bk, bv = _cmp_bias(pos_ref, w1_ref)
    kc, vc = _compress_tail(c_ref[...], bk, bv, w2_ref[0], w2_ref[1])
    q = z_ref[:, OFF_QNSA:OFF_QNSA + GROUP_W]
    q4 = _stack_heads(q)
    nc = kc.shape[0]
    s = _dot_nt(q4.astype(BF16), kc.astype(BF16)) * ATTN_SCALE
    jj = lax.broadcasted_iota(jnp.int32, (1, nc), 1)
    p = _masked_softmax(s, (jj * CMP_STRIDE + (L_CMP - 1) <= qpos) & (jj < n_cmp))
    ocmp_ref[...] = _dot(p.astype(BF16), vc.astype(BF16))[0:N_HEADS]
    b = pl.program_id(0)
    psum_ref[pl.ds(b, 1), :] = jnp.sum(p[0:N_HEADS], axis=0, keepdims=True)

    @pl.when(b == pl.num_programs(0) - 1)
    def _():
        imp = _dot(psum_ref[...], cover_ref[...], precision=HIGHEST)
        ns, nsp = imp.shape
        blk = lax.broadcasted_iota(jnp.int32, (1, nsp), 1)
        fblk = blk.astype(F32)
        cur = qpos // L_SEL
        imp = jnp.where((blk == 0) | (blk == cur) | (blk == cur - 1), jnp.inf, imp)
        imp = jnp.where(blk * L_SEL > qpos, -jnp.inf, imp)
        work = jnp.where(blk < n_sel, jnp.maximum(imp, -3e38), -jnp.inf)
        out_lane = lax.broadcasted_iota(jnp.int32, (1, LANES), 1)
        idx = jnp.zeros((ns, LANES), jnp.int32)
        for t in range(N_SEL):
            mx = jnp.max(work, axis=-1, keepdims=True)
            pick = jnp.min(jnp.where(work == mx, fblk, float(nsp)), axis=-1, keepdims=True)
            idx = jnp.where(out_lane == t, pick.astype(jnp.int32), idx)
            work = jnp.where(fblk == pick, -jnp.inf, work)
        idx_ref[...] = idx


def _nsa_select(c, z3, pos, w1, w2, qpos):
    s, nc, _ = c.shape
    n_cmp = nc - 1
    n_sel = qpos // L_SEL + 1
    nsp = -(-n_sel // LANES) * LANES
    cover = _cover_matrix(nc, n_cmp, nsp)
    kern = functools.partial(_nsa_select_kernel, n_cmp=n_cmp, n_sel=n_sel, qpos=qpos)
    cst = lambda shape: pl.BlockSpec(shape, lambda b: (0,) * len(shape))
    return pl.pallas_call(
        kern,
        grid=(s,),
        in_specs=[pl.BlockSpec((None, nc, 4 * HEAD_DIM), lambda b: (b, 0, 0)),
                  pl.BlockSpec((None, 1, N_IN_PAD), lambda b: (b, 0, 0)),
                  cst(pos.shape), cst(w1.shape), cst(w2.shape), cst(cover.shape)],
        out_specs=[pl.BlockSpec((None, N_HEADS, HEAD_DIM), lambda b: (b, 0, 0)),
                   pl.BlockSpec((s, LANES), lambda b: (0, 0))],
        out_shape=[jax.ShapeDtypeStruct((s, N_HEADS, HEAD_DIM), F32), jax.ShapeDtypeStruct((s, LANES), jnp.int32)],
        scratch_shapes=[pltpu.VMEM((s, nc), F32)],
        compiler_params=_cparams(("arbitrary",)),
        name="nsa_select",
    )(c, z3, pos, w1, w2, cover)


def _col_from_row(row):
    n = row.shape[1]
    eye = lax.broadcasted_iota(jnp.int32, (n, n), 0) == lax.broadcasted_iota(jnp.int32, (n, n), 1)
    return jnp.sum(jnp.where(eye, row, 0.0), axis=-1, keepdims=True)


def _nsa_sample_kernel(pt_ref, idx_ref, cache_ref, z_ref, ocmp_ref, win_ref, o_ref, wnew_ref, buf_ref, sem_ref,
                       *, layer, page, qpos):
    b = pl.program_id(0)
    npg = pt_ref.shape[1]
    per_page = page // L_SEL

    slot = b % 2

    def copy(bb, t, sl):
        pg = jnp.minimum(idx_ref[bb, t] // per_page, npg - 1)
        return pltpu.make_async_copy(cache_ref.at[layer, pt_ref[bb, pg], pl.ds(2, 2)], buf_ref.at[sl, t],
                                     sem_ref.at[sl])

    @pl.when(b == 0)
    def _():
        for t in range(N_SEL):
            copy(b, t, 0).start()

    @pl.when(b + 1 < pl.num_programs(0))
    def _():
        for t in range(N_SEL):
            copy(b + 1, t, 1 - slot).start()

    z = z_ref[...]
    q = z[:, OFF_QNSA:OFF_QNSA + GROUP_W]
    q4b = _stack_heads(q).astype(BF16)
    kvn = z[:, OFF_KVNSA:OFF_KVNSA + GROUP_W]
    ks_new, vs_new = kvn[:, 2 * HEAD_DIM:3 * HEAD_DIM], kvn[:, 3 * HEAD_DIM:4 * HEAD_DIM]
    kvw = z[:, OFF_KVWIN:OFF_KVWIN + 2 * HEAD_DIM]
    kw_new, vw_new = kvw[:, 0:HEAD_DIM], kvw[:, HEAD_DIM:2 * HEAD_DIM]
    q4r = q4b.astype(F32)

    def new_score(k_new):
        kb = k_new.astype(BF16).astype(F32)
        return jnp.sum(q4r * kb, axis=-1, keepdims=True) * ATTN_SCALE

    wk = win_ref[0]
    wv = win_ref[1]
    sw = _dot(q4b, wk.astype(BF16)) * ATTN_SCALE
    sw_new = new_score(kw_new)
    mw = jnp.maximum(jnp.max(sw, axis=-1, keepdims=True), sw_new)
    ew = jnp.exp(sw - mw)
    ew_new = jnp.exp(sw_new - mw)
    dw = jnp.sum(ew, axis=-1, keepdims=True) + ew_new
    pw = ew * (1.0 / dw)
    pw_new = ew_new * (1.0 / dw)
    o_win = _dot_nt(pw.astype(BF16), wv.astype(BF16)) \
        + pw_new.astype(BF16).astype(F32) * vw_new.astype(BF16).astype(F32)
    wl = wk.shape[1]
    lane = lax.broadcasted_iota(jnp.int32, (1, wl), 1)
    wnew_ref[0] = jnp.where(lane == wl - 1, _col_from_row(kw_new), pltpu.roll(wk, wl - 1, 1))
    wnew_ref[1] = jnp.where(lane == wl - 1, _col_from_row(vw_new), pltpu.roll(wv, wl - 1, 1))

    pltpu.make_async_copy(cache_ref.at[layer, pl.ds(0, N_SEL), pl.ds(2, 2)], buf_ref.at[slot],
                          sem_ref.at[slot]).wait()

    n_past = (qpos // L_SEL)
    plane = lax.broadcasted_iota(jnp.int32, (1, page), 1)
    scores = []
    new_taken = jnp.int32(0)
    for t in range(N_SEL):
        bid = idx_ref[b, t]
        in_past = bid < n_past
        half = bid % per_page
        valid = (plane >= half * L_SEL) & (plane < (half + 1) * L_SEL) & in_past
        sc = _dot(q4b, buf_ref[slot, t, 0].astype(BF16)) * ATTN_SCALE
        scores.append(jnp.where(valid, sc, -jnp.inf))
        new_taken = new_taken + (bid == n_past).astype(jnp.int32)
    has_new = new_taken > 0
    ss_new = jnp.where(has_new, new_score(ks_new), -jnp.inf)
    ms = ss_new
    for sc in scores:
        ms = jnp.maximum(ms, jnp.max(sc, axis=-1, keepdims=True))
    ms = jnp.where(ms == -jnp.inf, 0.0, ms)
    es_new = jnp.exp(ss_new - ms)
    ds = es_new
    es = []
    for sc in scores:
        e = jnp.exp(sc - ms)
        es.append(e)
        ds = ds + jnp.sum(e, axis=-1, keepdims=True)
    inv = 1.0 / jnp.where(ds > 0, ds, 1.0)
    o_sel = (es_new * inv).astype(BF16).astype(F32) * vs_new.astype(BF16).astype(F32)
    for t in range(N_SEL):
        o_sel = o_sel + _dot_nt((es[t] * inv).astype(BF16), buf_ref[slot, t, 1].astype(BF16))

    g = _sigmoid(z[:, OFF_GATE:OFF_GATE + LANES])
    o_cmp = ocmp_ref[...]
    for h in range(N_HEADS):
        oh = (g[:, h:h + 1] * o_cmp[h:h + 1] + g[:, N_HEADS + h:N_HEADS + h + 1] * o_sel[h:h + 1]
              + g[:, 2 * N_HEADS + h:2 * N_HEADS + h + 1] * o_win[h:h + 1])
        o_ref[:, h * HEAD_DIM:(h + 1) * HEAD_DIM] = oh


def _nsa_sample(page_table, idx, cache_t, z3, ocmp, win_t, layer, qpos):
    s = z3.shape[0]
    page = cache_t.shape[-1]
    wl = win_t.shape[-1]
    kern = functools.partial(_nsa_sample_kernel, layer=layer, page=page, qpos=qpos)
    return pl.pallas_call(
        kern,
        grid_spec=pltpu.PrefetchScalarGridSpec(
            num_scalar_prefetch=2,
            grid=(s,),
            in_specs=[pl.BlockSpec(memory_space=pl.ANY),
                      pl.BlockSpec((None, 1, N_IN_PAD), lambda b, pt, ix: (b, 0, 0)),
                      pl.BlockSpec((None, N_HEADS, HEAD_DIM), lambda b, pt, ix: (b, 0, 0)),
                      pl.BlockSpec((None, None, 2, HEAD_DIM, wl), lambda b, pt, ix: (layer, b, 0, 0, 0))],
            out_specs=[pl.BlockSpec((None, 1, GROUP_W), lambda b, pt, ix: (b, 0, 0)),
                       pl.BlockSpec((None, 2, HEAD_DIM, wl), lambda b, pt, ix: (b, 0, 0, 0))],
            scratch_shapes=[pltpu.VMEM((2, N_SEL, 2, HEAD_DIM, page), F32), pltpu.SemaphoreType.DMA((2,))]),
        out_shape=[jax.ShapeDtypeStruct((s, 1, GROUP_W), F32), jax.ShapeDtypeStruct((s, 2, HEAD_DIM, wl), F32)],
        compiler_params=_cparams(("arbitrary",)),
        name="nsa_sample",
    )(page_table, idx, cache_t, z3, ocmp, win_t)


_MOBA_PAGES = 16


def _moba_gate_kernel(pt_ref, cache_ref, qcol_ref, idx_ref, buf_ref, part_ref, sem_ref,
                      *, layer, n_steps, page, nb, cur):
    b = pl.program_id(0)
    s = pl.program_id(1)
    g = _MOBA_PAGES
    step = b * n_steps + s
    total = pl.num_programs(0) * n_steps
    ppb = MOBA_BLOCK // page
    bps = g // ppb

    def copies(stp, slot):
        bb = stp // n_steps
        ss = stp % n_steps
        return [pltpu.make_async_copy(cache_ref.at[layer, pt_ref[bb, ss * g + k], 0],
                                      buf_ref.at[slot, k], sem_ref.at[slot]) for k in range(g)]

    ahead = _STREAM_SLOTS - 1

    @pl.when(step == 0)
    def _():
        for d in range(ahead):
            @pl.when(d < total)
            def _():
                for cp in copies(d, d):
                    cp.start()

    slot = step % _STREAM_SLOTS

    @pl.when(step + ahead < total)
    def _():
        for cp in copies(step + ahead, (step + ahead) % _STREAM_SLOTS):
            cp.start()

    pltpu.make_async_copy(cache_ref.at[layer, pl.ds(0, g), 0], buf_ref.at[slot], sem_ref.at[slot]).wait()

    for j in range(bps):
        for h in range(N_HEADS):
            qh = qcol_ref[h * HEAD_DIM:(h + 1) * HEAD_DIM, :]
            acc = buf_ref[slot, j * ppb, h] * qh
            for k in range(1, ppb):
                acc = acc + buf_ref[slot, j * ppb + k, h] * qh
            part_ref[h * nb + s * bps + j] = jnp.sum(acc.reshape(HEAD_DIM // 8, 8, page), axis=0)

    @pl.when(s == n_steps - 1)
    def _():
        ones = jnp.ones((8, page), F32)
        part = jnp.sum(part_ref[...], axis=1)
        gate = _dot_nt(ones, part, precision=HIGHEST)[0:1] * (1.0 / MOBA_BLOCK)
        lane = lax.broadcasted_iota(jnp.int32, gate.shape, 1)
        out_lane = lax.broadcasted_iota(jnp.int32, (1, LANES), 1)
        idx = jnp.zeros((1, LANES), jnp.int32)
        for h in range(N_HEADS):
            n_of = lane - h * nb
            f_of = n_of.astype(F32)
            inh = (n_of >= 0) & (n_of < nb) & (n_of < cur)
            work = jnp.where(inh, jnp.maximum(gate, -3e38), -jnp.inf)
            for t in range(MOBA_TOPK):
                mx = jnp.max(work, axis=-1, keepdims=True)
                pick = jnp.min(jnp.where((work == mx) & inh, f_of, float(4 * nb)), axis=-1, keepdims=True)
                idx = jnp.where(out_lane == h * MOBA_TOPK + t, pick.astype(jnp.int32), idx)
                work = jnp.where(f_of == pick, -jnp.inf, work)
        idx_ref[...] = idx


def _moba_gate(page_table, cache_t, qcol, layer, cur):
    s, npg = page_table.shape
    page = cache_t.shape[-1]
    g = _MOBA_PAGES
    n_steps = npg // g
    nb = npg * page // MOBA_BLOCK
    kern = functools.partial(_moba_gate_kernel, layer=layer, n_steps=n_steps, page=page, nb=nb, cur=cur)
    return pl.pallas_call(
        kern,
        grid_spec=pltpu.PrefetchScalarGridSpec(
            num_scalar_prefetch=1,
            grid=(s, n_steps),
            in_specs=[pl.BlockSpec(memory_space=pl.ANY),
                      pl.BlockSpec((None, N_HEADS * HEAD_DIM, page), lambda b, i, pt: (b, 0, 0))],
            out_specs=pl.BlockSpec((None, 1, LANES), lambda b, i, pt: (b, 0, 0)),
            scratch_shapes=[pltpu.VMEM((_STREAM_SLOTS, g, N_HEADS, HEAD_DIM, page), F32),
                            pltpu.VMEM((N_HEADS * nb, 8, page), F32),
                            pltpu.SemaphoreType.DMA((_STREAM_SLOTS,))]),
        out_shape=jax.ShapeDtypeStruct((s, 1, LANES), jnp.int32),
        compiler_params=_cparams(("arbitrary", "arbitrary")),
        name="moba_gate",
    )(page_table, cache_t, qcol)


def _moba_sample_kernel(pt_ref, idx_ref, cache_ref, z_ref, o_ref, buf_ref, sem_ref, *, layer, page, cur):
    b = pl.program_id(0)
    npg = pt_ref.shape[1]
    ppb = MOBA_BLOCK // page
    n_src = MOBA_TOPK * ppb

    slot = b % 2

    def copy(bb, sl, h, t, k, kv):
        blk = jnp.minimum(idx_ref[bb, h * MOBA_TOPK + t], npg // ppb - 1)
        return pltpu.make_async_copy(cache_ref.at[layer, pt_ref[bb, blk * ppb + k], kv, h],
                                     buf_ref.at[sl, t * ppb + k, kv, h], sem_ref.at[sl])

    every = [(h, t, k, kv) for h in range(N_HEADS) for t in range(MOBA_TOPK) for k in range(ppb) for kv in range(2)]

    @pl.when(b == 0)
    def _():
        for a in every:
            copy(b, 0, *a).start()

    @pl.when(b + 1 < pl.num_programs(0))
    def _():
        for a in every:
            copy(b + 1, 1 - slot, *a).start()

    z = z_ref[...]
    q = z[:, OFF_QMOBA:OFF_QMOBA + GROUP_W]
    k_new = z[:, OFF_KVMOBA:OFF_KVMOBA + GROUP_W]
    v_new = z[:, OFF_KVMOBA + GROUP_W:OFF_KVMOBA + 2 * GROUP_W]
    pltpu.make_async_copy(cache_ref.at[layer, pl.ds(0, n_src)], buf_ref.at[slot], sem_ref.at[slot]).wait()

    for h in range(N_HEADS):
        lo, hi = h * HEAD_DIM, (h + 1) * HEAD_DIM
        qh = jnp.concatenate([q[:, lo:hi]] * 8, axis=0).astype(BF16)
        qr = q[:, lo:hi].astype(BF16).astype(F32)
        s_new = jnp.sum(qr * k_new[:, lo:hi].astype(BF16).astype(F32), axis=-1, keepdims=True) * ATTN_SCALE
        scores = []
        m = s_new
        for t in range(MOBA_TOPK):
            ok = idx_ref[b, h * MOBA_TOPK + t] < cur
            for k in range(ppb):
                sc = _dot(qh, buf_ref[slot, t * ppb + k, 0, h].astype(BF16))[0:1] * ATTN_SCALE
                sc = jnp.where(ok, sc, -jnp.inf)
                scores.append(sc)
                m = jnp.maximum(m, jnp.max(sc, axis=-1, keepdims=True))
        e_new = jnp.exp(s_new - m)
        d = e_new
        es = []
        for sc in scores:
            e = jnp.exp(sc - m)
            es.append(e)
            d = d + jnp.sum(e, axis=-1, keepdims=True)
        inv = 1.0 / d
        o = (e_new * inv).astype(BF16).astype(F32) * v_new[:, lo:hi].astype(BF16).astype(F32)
        for n in range(n_src):
            pn = jnp.concatenate([es[n] * inv] * 8, axis=0).astype(BF16)
            o = o + _dot_nt(pn, buf_ref[slot, n, 1, h].astype(BF16))[0:1]
        o_ref[:, lo:hi] = o


def _moba_sample(page_table, idx, cache_t, z3, layer, cur):
    s = z3.shape[0]
    page = cache_t.shape[-1]
    ppb = MOBA_BLOCK // page
    kern = functools.partial(_moba_sample_kernel, layer=layer, page=page, cur=cur)
    return pl.pallas_call(
        kern,
        grid_spec=pltpu.PrefetchScalarGridSpec(
            num_scalar_prefetch=2,
            grid=(s,),
            in_specs=[pl.BlockSpec(memory_space=pl.ANY),
                      pl.BlockSpec((None, 1, N_IN_PAD), lambda b, pt, ix: (b, 0, 0))],
            out_specs=pl.BlockSpec((None, 1, GROUP_W), lambda b, pt, ix: (b, 0, 0)),
            scratch_shapes=[pltpu.VMEM((2, MOBA_TOPK * ppb, 2, N_HEADS, HEAD_DIM, page), F32),
                            pltpu.SemaphoreType.DMA((2,))]),
        out_shape=jax.ShapeDtypeStruct((s, 1, GROUP_W), F32),
        compiler_params=_cparams(("arbitrary",)),
        name="moba_sample",
    )(page_table, idx, cache_t, z3)


def _cmp_weight_stack(w1):
    w = w1.reshape(2, 2, CMP_STRIDE, HEAD_DIM, HEAD_DIM)
    zero = jnp.zeros((CMP_STRIDE, HEAD_DIM, 2 * HEAD_DIM), F32)
    top = jnp.concatenate([w[0, 0], w[0, 1], zero], axis=-1)
    bot = jnp.concatenate([zero, w[1, 0], w[1, 1]], axis=-1)
    per_r = jnp.concatenate([top, bot], axis=1)
    return per_r.reshape(CMP_STRIDE // 2, 4 * HEAD_DIM, 4 * HEAD_DIM).astype(BF16)


def kernel(x_prompt, x_sample, cache_moba_kv, cache_nsa_kv, state_nsa_win, state_conv, state_pool, page_table,
           w_in, conv_dw, conv_dw_b, conv_ln_g, conv_ln_b, conv_pw, nsa_cmp_pos, nsa_cmp_w1, nsa_cmp_w2,
           pool_w, pool_scale, w_out, ln1_g, ln1_b, w_up, w_down, ln2_g, ln2_b):
    bp, t, _ = x_prompt.shape
    bs = x_sample.shape[0]
    depth = w_in.shape[0]
    page = cache_nsa_kv.shape[2]
    past_len = page_table.shape[1] * page
    win_len = state_nsa_win.shape[2]
    assert x_sample.shape[1] == 1 and win_len == WINDOW and t % 512 == 0 and t >= WINDOW + 128
    assert page % L_SEL == 0 and MOBA_BLOCK % page == 0 and past_len % MOBA_BLOCK == 0
    assert past_len // MOBA_BLOCK >= MOBA_TOPK and past_len // L_SEL + 1 >= N_SEL and past_len >= POOL_MAX
    assert page_table.shape[1] % _CMP_PAGES == 0 and page_table.shape[1] % _MOBA_PAGES == 0

    nsa_t = jnp.transpose(cache_nsa_kv, (0, 1, 3, 4, 2))
    moba_t = jnp.transpose(cache_moba_kv, (0, 1, 3, 4, 5, 2))
    win_t = jnp.transpose(state_nsa_win, (0, 1, 3, 4, 2))
    conv_t = jnp.transpose(state_conv, (0, 2, 1, 3))
    pool_t = jnp.transpose(state_pool, (0, 2, 1, 3))

    row = lambda v: v.reshape(1, -1)
    yp = x_prompt.reshape(bp * t, D_MODEL)
    ys = x_sample.reshape(bs, D_MODEL)
    st_p, st_s = [], []
    w_out_b, w_up_b, w_down_b = w_out.astype(BF16), w_up.astype(BF16), w_down.astype(BF16)
    for l in range(depth):
        w_in_l = _in_weights(w_in[l])
        pw = conv_pw[l].astype(BF16)
        plw = jax.scipy.linalg.block_diag(*[pool_w[l, g] for g in range(len(POOL_WINDOWS))]).astype(BF16)
        w1 = nsa_cmp_w1[l].astype(BF16)
        w2 = nsa_cmp_w2[l].astype(BF16)
        pos = jnp.broadcast_to(nsa_cmp_pos[l].reshape(2, 1, L_CMP * HEAD_DIM), (2, 8, L_CMP * HEAD_DIM))
        small = (conv_dw[l], row(conv_dw_b[l]), row(conv_ln_g[l]), row(conv_ln_b[l]), pw, plw, row(pool_scale[l]))

        z, kvm, kvn, kvw = _in_proj_split(yp, w_in_l)
        z3 = z.reshape(bp, t, N_MAIN)
        kvm3 = kvm.reshape(bp, t, 2 * GROUP_W)
        kvn3 = kvn.reshape(bp, t, GROUP_W)
        win3 = kvw.reshape(bp, t, 2 * HEAD_DIM)
        y_conv, y_pool, conv_new32 = _conv_pool_prompt(z3, *small)
        wstack = _cmp_weight_stack(nsa_cmp_w1[l])
        kc, vc = _compress_prompt(kvn3, pos, w1, w2, wstack)
        o_nsa = _nsa_prompt(z3, kvn3, win3, kc, vc)
        o_moba = _moba_prompt(z3, kvm3)
        flat = lambda a: a.reshape(bp * t, GROUP_W)
        x1 = _out_proj((flat(y_conv), flat(o_nsa), flat(o_moba), flat(y_pool)), yp, w_out_b, l,
                       row(ln1_g[l]), row(ln1_b[l]))
        yp = _ffn(x1, w_up_b, w_down_b, l, row(ln2_g[l]), row(ln2_b[l]))
        moba_rows, nsa_rows = kvm3, kvn3
        win_new = win3[:, t - win_len:].reshape(bp, win_len, 2, HEAD_DIM)
        conv_new = conv_new32[:, _CONV_HALO - (CONV_W - 1):]
        pool_new = z3[:, t - (POOL_MAX - 1):, OFF_UPOOL:OFF_UPOOL + GROUP_W]
        st_p.append((moba_rows, nsa_rows, win_new, conv_new, pool_new))

        zs = _in_proj(ys, w_in_l)
        zs3 = zs.reshape(bs, 1, N_IN_PAD)
        ys_conv, ys_pool, conv_new_t, pool_new_t = _conv_pool_sample(zs, conv_t[l], pool_t[l], *small)
        c = _compress_pages(page_table, nsa_t, wstack, l)
        o_cmp, sel_idx = _nsa_select(c, zs3, pos, w1, w2, past_len)
        os_nsa, win_new_t = _nsa_sample(page_table, sel_idx, nsa_t, zs3, o_cmp, win_t, l, past_len)
        q_m = zs[:, OFF_QMOBA:OFF_QMOBA + GROUP_W]
        qcol = jnp.broadcast_to(q_m[:, :, None], (bs, GROUP_W, page))
        top_idx = _moba_gate(page_table, moba_t, qcol, l, past_len // MOBA_BLOCK)
        os_moba = _moba_sample(page_table, top_idx.reshape(bs, LANES), moba_t, zs3, l, past_len // MOBA_BLOCK)
        x1s = _out_proj((ys_conv, os_nsa.reshape(bs, GROUP_W), os_moba.reshape(bs, GROUP_W), ys_pool), ys, w_out_b, l,
                        row(ln1_g[l]), row(ln1_b[l]))
        ys = _ffn(x1s, w_up_b, w_down_b, l, row(ln2_g[l]), row(ln2_b[l]))
        st_s.append((zs[:, OFF_KVMOBA:OFF_KVMOBA + 2 * GROUP_W].reshape(bs, 1, 2, N_HEADS, HEAD_DIM),
                     zs[:, OFF_KVNSA:OFF_KVNSA + GROUP_W].reshape(bs, 1, 4, HEAD_DIM),
                     jnp.transpose(win_new_t, (0, 3, 1, 2)),
                     jnp.transpose(conv_new_t, (1, 0, 2)),
                     jnp.transpose(pool_new_t, (1, 0, 2))))

    stk = lambda sts, i: jnp.stack([s[i] for s in sts], axis=0)
    moba_state = _kv_rows_to_state([s[0] for s in st_p]).reshape(depth, bp, 2, N_HEADS, HEAD_DIM, t)
    nsa_state = _kv_rows_to_state([s[1] for s in st_p]).reshape(depth, bp, 4, HEAD_DIM, t)
    return (yp.reshape(bp, t, D_MODEL), ys.reshape(bs, 1, D_MODEL),
            jnp.transpose(moba_state, (0, 1, 5, 2, 3, 4)), stk(st_s, 0),
            jnp.transpose(nsa_state, (0, 1, 4, 2, 3)), stk(st_s, 1), stk(st_p, 2), stk(st_s, 2),
            stk(st_p, 3), stk(st_s, 3), stk(st_p, 4), stk(st_s, 4))
```
